```python
import math
import jax, jax.numpy as jnp
from jax import lax
import numpy as np

D_MODEL = 1024
BATCH = 4
SEQ = 8192
DEPTH = 4

HEAD_DIM = 64
ROPE_THETA = 10000.0
EPS = 1e-6
NEG_INF = -1e30
Q_BLOCK = 128
MLA_HEADS = 4
MLA_Q_RANK = 256
MLA_KV_RANK = 128
MLA_NOPE = 64
MLA_ROPE = 32
MLA_V = 64
SWA_HEADS = 4
SWA_KV_HEADS = 2
SWA_WINDOW = 128
DIFF_HEADS = 4
DIFF_QK = 32
DIFF_V = 2 * DIFF_QK
HY_CH = 256
HY_ORDER = 2
HY_DIRS = 2
HY_SHORT = 3
HY_EMB = 33
HY_BANDS = (HY_EMB - 1) // 2
HY_FFN = 64
HY_TARGET = 1e-2
HY_MIN_DECAY = math.log(HY_TARGET) / 1.5
HY_MAX_DECAY = math.log(HY_TARGET) / 0.3
MLA_COLS = MLA_Q_RANK + MLA_KV_RANK + MLA_ROPE
SWA_COLS = (SWA_HEADS + 2 * SWA_KV_HEADS) * HEAD_DIM
DIFF_COLS = DIFF_HEADS * (4 * DIFF_QK + DIFF_V)
HY_COLS = (HY_ORDER + 1) * HY_CH
IN_COLS = MLA_COLS + SWA_COLS + DIFF_COLS + HY_COLS
D_MIX = MLA_HEADS * MLA_V + SWA_HEADS * HEAD_DIM + DIFF_HEADS * DIFF_V + HY_CH
N_GROUPS = 4
EXPERTS_PER_GROUP = 8
N_EXPERTS = N_GROUPS * EXPERTS_PER_GROUP
TOP_K = 2
D_EXPERT = 256
MOE_BLOCK = 128

kernel_name = 'hybrid_parallel_heads_hier_moe_encoder'


def rms_norm(x, g):
    x32 = x.astype(jnp.float32)
    y = x32 * lax.rsqrt(jnp.mean(x32 * x32, axis=-1, keepdims=True) + EPS)
    return (y * g.astype(jnp.float32)).astype(x.dtype)


def rope(x, positions):
    half = x.shape[-1] // 2
    inv_freq = jnp.exp(-math.log(ROPE_THETA) * jnp.arange(half, dtype=jnp.float32) / half)
    ang = positions.astype(jnp.float32)[:, None, :, None] * inv_freq
    cos, sin = jnp.cos(ang), jnp.sin(ang)
    x32 = x.astype(jnp.float32)
    x1, x2 = x32[..., :half], x32[..., half:]
    return jnp.concatenate([x1 * cos - x2 * sin, x1 * sin + x2 * cos], axis=-1).astype(x.dtype)


def to_heads(t, n_heads):
    B, S, _ = t.shape
    return t.reshape(B, S, n_heads, -1).transpose(0, 2, 1, 3)


def from_heads(t):
    B, H, S, d = t.shape
    return t.transpose(0, 2, 1, 3).reshape(B, S, H * d)


def query_blocks(q):
    B, H, S, d = q.shape
    return jnp.moveaxis(q.reshape(B, H, S // Q_BLOCK, Q_BLOCK, d), 2, 0)


def merge_query_blocks(o):
    nb, B, H, qb, d = o.shape
    return jnp.moveaxis(o, 0, 2).reshape(B, H, nb * qb, d)


def dense_attention(q, k, v, scale):
    def attend(qi):
        s = jnp.einsum('bhqd,bhkd->bhqk', qi, k).astype(jnp.float32) * scale
        p = jax.nn.softmax(s, axis=-1)
        return jnp.einsum('bhqk,bhkd->bhqd', p.astype(v.dtype), v)
    return merge_query_blocks(lax.map(attend, query_blocks(q)))


def mla_mixer(cols, positions, q_norm, kv_norm, w_uq, w_ukv):
    c_q, c_kv, k_pe = jnp.split(cols, [MLA_Q_RANK, MLA_Q_RANK + MLA_KV_RANK], axis=-1)
    q = to_heads(rms_norm(c_q, q_norm) @ w_uq, MLA_HEADS)
    kv = to_heads(rms_norm(c_kv, kv_norm) @ w_ukv, MLA_HEADS)
    q_nope, q_pe = q[..., :MLA_NOPE], q[..., MLA_NOPE:]
    k_nope, v = kv[..., :MLA_NOPE], kv[..., MLA_NOPE:]
    k_pe = rope(k_pe[:, None], positions)
    q = jnp.concatenate([q_nope, rope(q_pe, positions)], axis=-1)
    k = jnp.concatenate([k_nope, jnp.broadcast_to(k_pe, k_nope.shape[:-1] + (MLA_ROPE,))], axis=-1)
    o = dense_attention(q, k, v, (MLA_NOPE + MLA_ROPE) ** -0.5)
    return from_heads(o)


def band_blocks(t):
    B, H, S, d = t.shape
    nb = S // SWA_WINDOW
    tp = jnp.pad(t, ((0, 0), (0, 0), (SWA_WINDOW, SWA_WINDOW), (0, 0)))
    tp = tp.reshape(B, H, nb + 2, SWA_WINDOW, d)
    return jnp.concatenate([tp[:, :, :-2], tp[:, :, 1:-1], tp[:, :, 2:]], axis=3)


def swa_mixer(cols, positions, sink):
    B, S, _ = cols.shape
    W = SWA_WINDOW
    nb = S // W
    G = SWA_HEADS // SWA_KV_HEADS
    q, k, v = jnp.split(cols, [SWA_HEADS * HEAD_DIM, (SWA_HEADS + SWA_KV_HEADS) * HEAD_DIM], axis=-1)
    q = rope(to_heads(q, SWA_HEADS), positions).reshape(B, SWA_KV_HEADS, G, nb, W, HEAD_DIM)
    kb = band_blocks(rope(to_heads(k, SWA_KV_HEADS), positions))
    vb = band_blocks(to_heads(v, SWA_KV_HEADS))
    s = jnp.einsum('bhgnqd,bhnkd->bhgnqk', q, kb).astype(jnp.float32) * HEAD_DIM ** -0.5
    qi = jnp.arange(W)[:, None]
    kj = jnp.arange(3 * W)[None, :]
    band = jnp.abs(kj - W - qi) <= W
    kpos = (jnp.arange(nb)[:, None] - 1) * W + jnp.arange(3 * W)[None, :]
    valid = band[None] & ((kpos >= 0) & (kpos < S))[:, None, :]
    s = jnp.where(valid, s, NEG_INF)
    sink32 = sink.astype(jnp.float32).reshape(1, SWA_KV_HEADS, G, 1, 1, 1)
    m = jnp.maximum(jnp.max(s, axis=-1, keepdims=True), sink32)
    e = jnp.exp(s - m)
    p = e / (jnp.sum(e, axis=-1, keepdims=True) + jnp.exp(sink32 - m))
    o = jnp.einsum('bhgnqk,bhnkd->bhgnqd', p.astype(vb.dtype), vb)
    return from_heads(o.reshape(B, SWA_HEADS, S, HEAD_DIM))


def diff_mixer(cols, positions, lam_vecs, norm_g, lambda_init):
    B, S, _ = cols.shape
    q, k, v = jnp.split(cols, [DIFF_HEADS * 2 * DIFF_QK, DIFF_HEADS * 4 * DIFF_QK], axis=-1)
    q = q.reshape(B, S, DIFF_HEADS, 2, DIFF_QK)
    k = k.reshape(B, S, DIFF_HEADS, 2, DIFF_QK)
    q1 = rope(q[..., 0, :].transpose(0, 2, 1, 3), positions)
    q2 = rope(q[..., 1, :].transpose(0, 2, 1, 3), positions)
    k1 = rope(k[..., 0, :].transpose(0, 2, 1, 3), positions)
    k2 = rope(k[..., 1, :].transpose(0, 2, 1, 3), positions)
    v = to_heads(v, DIFF_HEADS)
    lv = lam_vecs.astype(jnp.float32)
    lam = jnp.exp(jnp.sum(lv[0] * lv[1])) - jnp.exp(jnp.sum(lv[2] * lv[3])) + lambda_init
    scale = DIFF_QK ** -0.5

    def attend(qs):
        q1i, q2i = qs
        p1 = jax.nn.softmax(jnp.einsum('bhqd,bhkd->bhqk', q1i, k1).astype(jnp.float32) * scale, axis=-1)
        p2 = jax.nn.softmax(jnp.einsum('bhqd,bhkd->bhqk', q2i, k2).astype(jnp.float32) * scale, axis=-1)
        return jnp.einsum('bhqk,bhkd->bhqd', (p1 - lam * p2).astype(v.dtype), v)

    o = merge_query_blocks(lax.map(attend, (query_blocks(q1), query_blocks(q2))))
    o = rms_norm(o, norm_g) * (1.0 - lambda_init)
    return from_heads(o)


def hyena_filters(L, w1, b1, w2, b2, w3, b3, w4, sin_freq):
    t = jnp.linspace(0.0, 1.0, L, dtype=jnp.float32)
    w = 2.0 * math.pi * jnp.arange(L, dtype=jnp.float32) / L
    f = jnp.linspace(1e-4, HY_BANDS - 1, HY_BANDS, dtype=jnp.float32)
    z = jnp.concatenate([t[:, None], jnp.cos(w[:, None] * f), -jnp.sin(w[:, None] * f)], axis=-1)
    z = z.astype(w1.dtype)
    h = jnp.sin(sin_freq[0] * (z @ w1 + b1))
    h = jnp.sin(sin_freq[1] * (h @ w2 + b2))
    h = jnp.sin(sin_freq[2] * (h @ w3 + b3))
    h = (h @ w4).astype(jnp.float32).reshape(L, HY_DIRS, HY_ORDER, HY_CH)
    decay = jnp.linspace(HY_MIN_DECAY, HY_MAX_DECAY, HY_CH, dtype=jnp.float32)
    window = jnp.exp(-t[:, None] * jnp.abs(decay))
    return h * window[:, None, None, :]


def two_sided_filter(h_fwd, h_bwd):
    C = h_fwd.shape[-1]
    filt = jnp.concatenate([h_fwd[:1] + h_bwd[:1], h_fwd[1:], jnp.zeros((1, C), h_fwd.dtype), h_bwd[:0:-1]], axis=0)
    return filt * lax.rsqrt(jnp.sum(filt * filt, axis=0, keepdims=True) + EPS)


def fft_long_conv(u, filt, bias):
    L = u.shape[1]
    u32 = u.astype(jnp.float32)
    U = jnp.fft.rfft(u32, n=2 * L, axis=1)
    F = jnp.fft.rfft(filt, axis=0)
    y = jnp.fft.irfft(U * F[None], n=2 * L, axis=1)[:, :L]
    return (y + u32 * bias.astype(jnp.float32)).astype(u.dtype)


def hyena_mixer(cols, conv_w, conv_b, w1, b1, w2, b2, w3, b3, w4, sin_freq, bias):
    B, S, C_in = cols.shape
    u = lax.conv_general_dilated(cols, conv_w[:, None, :], (1,), [(HY_SHORT // 2, HY_SHORT // 2)],
                                 dimension_numbers=('NWC', 'WIO', 'NWC'),
                                 feature_group_count=C_in) + conv_b
    v, x1, x2 = jnp.split(u, HY_ORDER + 1, axis=-1)
    filters = hyena_filters(S, w1, b1, w2, b2, w3, b3, w4, sin_freq)
    z = v
    for o, gate in enumerate((x1, x2)):
        z = gate * fft_long_conv(z, two_sided_filter(filters[:, 0, o], filters[:, 1, o]), bias[o])
    return z


def grouped_expert_ffn(ht, expert_idx, gate, w_gate, w_up, w_down):
    T, D = ht.shape
    E = w_gate.shape[0]
    A = T * TOP_K
    flat_e = expert_idx.reshape(A)
    flat_tok = jnp.repeat(jnp.arange(T, dtype=jnp.int32), TOP_K)
    flat_g = gate.reshape(A)
    order = jnp.argsort(flat_e)
    se = flat_e[order]
    counts = jnp.bincount(flat_e, length=E)
    padded = (counts + MOE_BLOCK - 1) // MOE_BLOCK * MOE_BLOCK
    pad_end = jnp.cumsum(padded)
    pad_start = pad_end - padded
    start = jnp.cumsum(counts) - counts
    dest = pad_start[se] + (jnp.arange(A, dtype=jnp.int32) - start[se])
    n_blocks = (A + E * (MOE_BLOCK - 1) + MOE_BLOCK - 1) // MOE_BLOCK
    P = n_blocks * MOE_BLOCK
    slot_tok = jnp.full((P,), T, jnp.int32).at[dest].set(flat_tok[order])
    slot_gate = jnp.zeros((P,), flat_g.dtype).at[dest].set(flat_g[order])
    block_expert = jnp.minimum(
        jnp.searchsorted(pad_end, jnp.arange(n_blocks, dtype=jnp.int32) * MOE_BLOCK, side='right'), E - 1)
    h_pad = jnp.concatenate([ht, jnp.zeros((1, D), ht.dtype)], axis=0)
    xb = h_pad[slot_tok].reshape(n_blocks, MOE_BLOCK, D)

    def expert_block(args):
        xe, e = args
        return (jax.nn.silu(xe @ w_gate[e]) * (xe @ w_up[e])) @ w_down[e]

    yb = lax.map(expert_block, (xb, block_expert)).reshape(P, D)
    y = jnp.zeros((T + 1, D), ht.dtype).at[slot_tok].add(yb * slot_gate[:, None].astype(yb.dtype))
    return y[:T]


def hier_moe(h, rg_w, rg_b, re_w, re_b, w_gate, w_up, w_down):
    B, S, D = h.shape
    T = B * S
    ht = h.reshape(T, D)
    pg = jax.nn.softmax((ht @ rg_w + rg_b).astype(jnp.float32), axis=-1)
    pg_top, g_sel = lax.top_k(pg, 1)
    le = (ht @ re_w + re_b).astype(jnp.float32).reshape(T, N_GROUPS, EXPERTS_PER_GROUP)
    idx = jnp.broadcast_to(g_sel[:, :, None], (T, 1, EXPERTS_PER_GROUP))
    le_sel = jnp.take_along_axis(le, idx, axis=1)[:, 0]
    top_v, top_i = lax.top_k(le_sel, TOP_K)
    expert_idx = g_sel * EXPERTS_PER_GROUP + top_i
    gate = pg_top * jax.nn.softmax(top_v, axis=-1)
    return grouped_expert_ffn(ht, expert_idx, gate, w_gate, w_up, w_down).reshape(B, S, D)


def setup_inputs(seed: int = 0) -> dict:
    key = jax.random.key(seed)
    ks = iter(jax.random.split(key, 48))

    def nrm(shape, scale):
        return scale * jax.random.normal(next(ks), shape, jnp.float32)

    L, D = DEPTH, D_MODEL
    return {
        'x': nrm((BATCH, SEQ, D), 1.0),
        'c': nrm((BATCH, D), 1.0),
        'positions': jnp.broadcast_to(jnp.arange(SEQ, dtype=jnp.int32), (BATCH, SEQ)),
        'ln1_g': 1.0 + nrm((L, D), 0.02),
        'ln2_g': 1.0 + nrm((L, D), 0.02),
        'ada_w': nrm((L, D, 6 * D), 0.5 * D ** -0.5),
        'ada_b': nrm((L, 6 * D), 0.02),
        'w_in': nrm((L, D, IN_COLS), D ** -0.5),
        'mla_q_norm': 1.0 + nrm((L, MLA_Q_RANK), 0.02),
        'mla_kv_norm': 1.0 + nrm((L, MLA_KV_RANK), 0.02),
        'mla_w_uq': nrm((L, MLA_Q_RANK, MLA_HEADS * (MLA_NOPE + MLA_ROPE)), MLA_Q_RANK ** -0.5),
        'mla_w_ukv': nrm((L, MLA_KV_RANK, MLA_HEADS * (MLA_NOPE + MLA_V)), MLA_KV_RANK ** -0.5),
        'swa_sink': nrm((L, SWA_HEADS), 0.5),
        'diff_lambda': nrm((L, 4, DIFF_QK), 0.1),
        'diff_norm_g': 1.0 + nrm((L, DIFF_V), 0.02),
        'hy_conv_w': nrm((L, HY_SHORT, HY_COLS), HY_SHORT ** -0.5),
        'hy_conv_b': nrm((L, HY_COLS), 0.02),
        'hy_w1': nrm((L, HY_EMB, HY_FFN), HY_EMB ** -0.5),
        'hy_b1': nrm((L, HY_FFN), 0.02),
        'hy_w2': nrm((L, HY_FFN, HY_FFN), HY_FFN ** -0.5),
        'hy_b2': nrm((L, HY_FFN), 0.02),
        'hy_w3': nrm((L, HY_FFN, HY_FFN), HY_FFN ** -0.5),
        'hy_b3': nrm((L, HY_FFN), 0.02),
        'hy_w4': nrm((L, HY_FFN, HY_DIRS * HY_ORDER * HY_CH), HY_FFN ** -0.5),
        'hy_sin_freq': 1.0 + nrm((L, 3, HY_FFN), 0.02),
        'hy_bias': nrm((L, HY_ORDER, HY_CH), 0.1),
        'w_out': nrm((L, D_MIX, D), D_MIX ** -0.5),
        'router_g_w': nrm((L, D, N_GROUPS), D ** -0.5),
        'router_g_b': nrm((L, N_GROUPS), 0.01),
        'router_e_w': nrm((L, D, N_EXPERTS), D ** -0.5),
        'router_e_b': nrm((L, N_EXPERTS), 0.01),
        'moe_w_gate': nrm((L, N_EXPERTS, D, D_EXPERT), D ** -0.5),
        'moe_w_up': nrm((L, N_EXPERTS, D, D_EXPERT), D ** -0.5),
        'moe_w_down': nrm((L, N_EXPERTS, D_EXPERT, D), D_EXPERT ** -0.5),
        'final_g': 1.0 + nrm((D,), 0.02),
    }


def reference(x, c, positions, ln1_g, ln2_g, ada_w, ada_b, w_in, mla_q_norm, mla_kv_norm,
              mla_w_uq, mla_w_ukv, swa_sink, diff_lambda, diff_norm_g, hy_conv_w, hy_conv_b,
              hy_w1, hy_b1, hy_w2, hy_b2, hy_w3, hy_b3, hy_w4, hy_sin_freq, hy_bias, w_out,
              router_g_w, router_g_b, router_e_w, router_e_b, moe_w_gate, moe_w_up, moe_w_down,
              final_g):
    c_act = jax.nn.silu(c)
    split_at = [MLA_COLS, MLA_COLS + SWA_COLS, MLA_COLS + SWA_COLS + DIFF_COLS]
    for l in range(DEPTH):
        mod = (c_act @ ada_w[l] + ada_b[l])[:, None, :]
        sh1, sc1, g1, sh2, sc2, g2 = jnp.split(mod, 6, axis=-1)
        h = rms_norm(x, ln1_g[l]) * (1.0 + sc1) + sh1
        cols = h @ w_in[l]
        c_a, c_b, c_c, c_d = jnp.split(cols, split_at, axis=-1)
        y_a = mla_mixer(c_a, positions, mla_q_norm[l], mla_kv_norm[l], mla_w_uq[l], mla_w_ukv[l])
        y_b = swa_mixer(c_b, positions, swa_sink[l])
        lambda_init = 0.8 - 0.6 * math.exp(-0.3 * l)
        y_c = diff_mixer(c_c, positions, diff_lambda[l], diff_norm_g[l], lambda_init)
        y_d = hyena_mixer(c_d, hy_conv_w[l], hy_conv_b[l], hy_w1[l], hy_b1[l], hy_w2[l], hy_b2[l],
                          hy_w3[l], hy_b3[l], hy_w4[l], hy_sin_freq[l], hy_bias[l])
        mix = jnp.concatenate([y_a, y_b, y_c, y_d], axis=-1)
        x = x + g1 * (mix @ w_out[l])
        h = rms_norm(x, ln2_g[l]) * (1.0 + sc2) + sh2
        x = x + g2 * hier_moe(h, router_g_w[l], router_g_b[l], router_e_w[l], router_e_b[l],
                              moe_w_gate[l], moe_w_up[l], moe_w_down[l])
    return rms_norm(x, final_g)
```

```python
import functools
import math

import numpy as np
import jax
import jax.numpy as jnp
from jax import lax
from jax.experimental import pallas as pl
from jax.experimental.pallas import tpu as pltpu

F32 = jnp.float32
BF16 = jnp.bfloat16
HIGHEST = lax.Precision.HIGHEST

HEAD_DIM = 64
ROPE_THETA = 10000.0
EPS = 1e-6
NEG_INF = -1e30
LOG2E = 1.4426950408889634
MLA_HEADS, MLA_Q_RANK, MLA_KV_RANK, MLA_NOPE, MLA_ROPE, MLA_V = 4, 256, 128, 64, 32, 64
SWA_HEADS, SWA_KV_HEADS, SWA_WINDOW = 4, 2, 128
DIFF_HEADS, DIFF_QK, DIFF_V = 4, 32, 64
HY_CH, HY_ORDER, HY_DIRS, HY_SHORT, HY_EMB, HY_FFN = 256, 2, 2, 3, 33, 64
HY_BANDS = (HY_EMB - 1) // 2
HY_MIN_DECAY = math.log(1e-2) / 1.5
HY_MAX_DECAY = math.log(1e-2) / 0.3
MLA_COLS = MLA_Q_RANK + MLA_KV_RANK + MLA_ROPE
SWA_COLS = (SWA_HEADS + 2 * SWA_KV_HEADS) * HEAD_DIM
DIFF_COLS = DIFF_HEADS * (4 * DIFF_QK + DIFF_V)
HY_COLS = (HY_ORDER + 1) * HY_CH
N_GROUPS, EXPERTS_PER_GROUP, TOP_K, D_EXPERT = 4, 8, 2, 256
N_EXPERTS = N_GROUPS * EXPERTS_PER_GROUP

V7X_VMEM_LIMIT_BYTES = 56 * 1024 * 1024
LANES = 128
DFT_P2 = 128
MOE_ROWS = 512


def _params(*sem):
    return pltpu.CompilerParams(dimension_semantics=sem, vmem_limit_bytes=V7X_VMEM_LIMIT_BYTES)


def _nt(a, b):
    return lax.dot_general(a, b, (((1,), (1,)), ((), ())), preferred_element_type=F32)


def _tn(a, b):
    return lax.dot_general(a, b, (((0,), (0,)), ((), ())), preferred_element_type=F32)


def _ada_kernel(c_ref, w_ref, b_ref, o_ref):
    c = c_ref[...]
    act = c * jax.nn.sigmoid(c)
    o_ref[0] = jnp.dot(act, w_ref[0], preferred_element_type=F32, precision=HIGHEST) + b_ref[0]


def _ada_modulation(c, ada_w, ada_b):
    L, D, N = ada_w.shape
    B = c.shape[0]
    c8 = jnp.zeros((8, D), F32).at[:B].set(c)
    tn = 1024
    return pl.pallas_call(
        _ada_kernel, name="ada_modulation", grid=(L, N // tn),
        in_specs=[pl.BlockSpec((8, D), lambda l, j: (0, 0)),
                  pl.BlockSpec((1, D, tn), lambda l, j: (l, 0, j)),
                  pl.BlockSpec((1, 1, tn), lambda l, j: (l, 0, j))],
        out_specs=pl.BlockSpec((1, 8, tn), lambda l, j: (l, 0, j)),
        out_shape=jax.ShapeDtypeStruct((L, 8, N), F32),
        compiler_params=_params("parallel", "parallel"))(c8, ada_w, ada_b.reshape(L, 1, N))


def _rope_table_kernel(p_ref, o_ref):
    pos = p_ref[0].astype(F32)
    for half, base in ((32, 0), (16, 128)):
        j = lax.broadcasted_iota(jnp.int32, (half, 1), 0).astype(F32)
        inv = jnp.exp(-math.log(ROPE_THETA) * j / half)
        ang = inv * pos
        cs, sn = jnp.cos(ang), jnp.sin(ang)
        o_ref[0, base:base + half] = cs
        o_ref[0, base + half:base + 2 * half] = cs
        o_ref[0, base + 2 * half:base + 3 * half] = -sn
        o_ref[0, base + 3 * half:base + 4 * half] = sn


def _rope_tables(positions):
    B, S = positions.shape
    return pl.pallas_call(
        _rope_table_kernel, name="rope_tables", grid=(B,),
        in_specs=[pl.BlockSpec((1, 1, S), lambda b: (b, 0, 0))],
        out_specs=pl.BlockSpec((1, 192, S), lambda b: (b, 0, 0)),
        out_shape=jax.ShapeDtypeStruct((B, 192, S), F32),
        compiler_params=_params("parallel"))(positions.reshape(B, 1, S))


def _partner(n_heads, dim):
    r = np.arange(n_heads * dim)
    return (r // dim) * dim + (r % dim + dim // 2) % dim


def _layer_weight_layouts(w_in, mla_q_norm, mla_kv_norm, mla_w_uq, mla_w_ukv):
    L, D, _ = w_in.shape
    oB = MLA_COLS
    oC = oB + SWA_COLS
    oD = oC + DIFF_COLS
    Z = w_in.shape[2]
    wz = jnp.concatenate([w_in, jnp.zeros((L, D, 1), F32)], axis=2)

    kpe = MLA_Q_RANK + MLA_KV_RANK + np.arange(MLA_ROPE)
    pad96 = np.full(LANES - MLA_ROPE, Z)
    swa_k = oB + SWA_HEADS * HEAD_DIM + np.arange(SWA_KV_HEADS * HEAD_DIM)
    h, j, r = np.meshgrid(np.arange(DIFF_HEADS), np.arange(2), np.arange(DIFF_QK), indexing="ij")
    diff_src = (h * 2 * DIFF_QK + j * DIFF_QK + r).transpose(1, 0, 2).reshape(-1)
    diff_q = oC + diff_src
    diff_k = oC + DIFF_HEADS * 2 * DIFF_QK + diff_src
    nat_idx = np.concatenate([
        np.arange(MLA_Q_RANK + MLA_KV_RANK),
        kpe, pad96, kpe[_partner(1, MLA_ROPE)], pad96,
        swa_k, swa_k[_partner(SWA_KV_HEADS, HEAD_DIM)],
        diff_k, diff_k[_partner(2 * DIFF_HEADS, DIFF_QK)],
        oD + np.arange(HY_COLS)])
    w_nat = jnp.take(wz, jnp.asarray(nat_idx), axis=2).astype(BF16)

    swa_q = oB + np.arange(SWA_HEADS * HEAD_DIM)
    swa_v = oB + (SWA_HEADS + SWA_KV_HEADS) * HEAD_DIM + np.arange(SWA_KV_HEADS * HEAD_DIM)
    diff_v = oC + DIFF_HEADS * 4 * DIFF_QK + np.arange(DIFF_HEADS * DIFF_V)
    fm_idx = np.concatenate([
        swa_q, swa_q[_partner(SWA_HEADS, HEAD_DIM)], swa_v,
        diff_q, diff_q[_partner(2 * DIFF_HEADS, DIFF_QK)], diff_v])
    s_swa = HEAD_DIM ** -0.5 * LOG2E
    s_diff = DIFF_QK ** -0.5 * LOG2E
    fm_scale = np.concatenate([np.full(512, s_swa), np.ones(128), np.full(512, s_diff), np.ones(256)]).astype(np.float32)
    w_fm = (jnp.take(wz, jnp.asarray(fm_idx), axis=2) * fm_scale).transpose(0, 2, 1).astype(BF16)

    s_mla = (MLA_NOPE + MLA_ROPE) ** -0.5 * LOG2E
    uq = mla_w_uq * mla_q_norm[:, :, None] * s_mla
    hq = np.arange(MLA_HEADS)[:, None] * (MLA_NOPE + MLA_ROPE)
    qn_idx = (hq + np.arange(MLA_NOPE)[None]).reshape(-1)
    qp_idx = (hq + MLA_NOPE + np.arange(MLA_ROPE)[None]).reshape(-1)
    w_qn = jnp.take(uq, jnp.asarray(qn_idx), axis=2).transpose(0, 2, 1).astype(BF16)
    w_qpa = jnp.take(uq, jnp.asarray(qp_idx), axis=2).transpose(0, 2, 1).astype(BF16)
    w_qpb = jnp.take(uq, jnp.asarray(qp_idx[_partner(MLA_HEADS, MLA_ROPE)]), axis=2).transpose(0, 2, 1).astype(BF16)
    ukv = mla_w_ukv * mla_kv_norm[:, :, None]
    hk = np.arange(MLA_HEADS)[:, None] * (MLA_NOPE + MLA_V)
    kn_idx = (hk + np.arange(MLA_NOPE)[None]).reshape(-1)
    v_idx = (hk + MLA_NOPE + np.arange(MLA_V)[None]).reshape(-1)
    w_kn = jnp.take(ukv, jnp.asarray(kn_idx), axis=2).astype(BF16)
    w_v = jnp.take(ukv, jnp.asarray(v_idx), axis=2).transpose(0, 2, 1).astype(BF16)
    return w_nat, w_fm, w_qn, w_qpa, w_qpb, w_kn, w_v


def _natural_rope_table(tab_fm):
    t = tab_fm.transpose(0, 2, 1)
    cos64, sin64, cos32, sin32 = t[..., 0:64], t[..., 64:128], t[..., 128:160], t[..., 160:192]
    z = jnp.zeros(t.shape[:2] + (LANES - MLA_ROPE,), F32)
    return jnp.concatenate([cos32, z, sin32, z,
                            jnp.tile(cos64, (1, 1, 2)), jnp.tile(sin64, (1, 1, 2)),
                            jnp.tile(cos32, (1, 1, 8)), jnp.tile(sin32, (1, 1, 8))], axis=-1)


def _rms(x):
    return x * lax.rsqrt(jnp.mean(x * x, axis=-1, keepdims=True) + EPS)


def _proj_kernel(x_ref, sc_ref, sh_ref, g_ref, wn_ref, wf_ref, wqn_ref, wqpa_ref, wqpb_ref, wkn_ref, wv_ref,
                 tn_ref, tf_ref,
                 kmla_ref, qn_ref, qp_ref, vmla_ref, kswa_ref, qswa_ref, vswa_ref, kdiff_ref, qdiff_ref, vdiff_ref, hy_ref):
    x = x_ref[0]
    t = x.shape[0]
    h = _rms(x) * g_ref[...] * (1.0 + sc_ref[0]) + sh_ref[0]
    hb = h.astype(BF16)
    cn = jnp.dot(hb, wn_ref[0], preferred_element_type=F32)
    tab = tn_ref[0]
    cqn = _rms(cn[:, 0:256]).astype(BF16)
    ckvn = _rms(cn[:, 256:384]).astype(BF16)
    kpe = (cn[:, 384:512] * tab[:, 0:128] + cn[:, 512:640] * tab[:, 128:256]).astype(BF16)
    kswa_ref[0] = (cn[:, 640:768] * tab[:, 256:384] + cn[:, 768:896] * tab[:, 384:512]).astype(BF16)
    kdiff_ref[0] = (cn[:, 896:1152] * tab[:, 512:768] + cn[:, 1152:1408] * tab[:, 768:1024]).astype(BF16)
    hy_ref[0] = cn[:, 1408:2176]
    kn = jnp.dot(ckvn, wkn_ref[0], preferred_element_type=F32).astype(BF16)
    kmla_ref[0, 0] = jnp.concatenate([kn[:, 0:128], kpe], axis=1)
    kmla_ref[0, 1] = jnp.concatenate([kn[:, 128:256], kpe], axis=1)

    ft = _nt(wf_ref[0], hb)
    tf = tf_ref[0]
    cos64, sin64, cos32, sin32 = tf[0:64], tf[64:128], tf[128:160], tf[160:192]
    qswa = ft[0:256].reshape(4, 64, t) * cos64[None] + ft[256:512].reshape(4, 64, t) * sin64[None]
    qswa_ref[0] = qswa.reshape(256, t).astype(BF16)
    vswa_ref[0] = ft[512:640].astype(BF16)
    qdiff = ft[640:896].reshape(8, 32, t) * cos32[None] + ft[896:1152].reshape(8, 32, t) * sin32[None]
    qdiff_ref[0] = qdiff.reshape(256, t).astype(BF16)
    vdiff_ref[0] = ft[1152:1408].astype(BF16)
    qn_ref[0] = _nt(wqn_ref[0], cqn).astype(BF16)
    qp = _nt(wqpa_ref[0], cqn).reshape(4, 32, t) * cos32[None] + _nt(wqpb_ref[0], cqn).reshape(4, 32, t) * sin32[None]
    qp_ref[0] = qp.reshape(128, t).astype(BF16)
    vmla_ref[0] = _nt(wv_ref[0], ckvn).astype(BF16)


def _project(x, sc1, sh1, ln_g, layer, weights, tab_nat, tab_fm, t=256):
    B, S, D = x.shape
    w_nat, w_fm, w_qn, w_qpa, w_qpb, w_kn, w_v = weights
    wspec = lambda a: pl.BlockSpec((1,) + a.shape[1:], lambda b, i: (layer,) + (0,) * (a.ndim - 1))
    nat = lambda n: pl.BlockSpec((1, t, n), lambda b, i: (b, i, 0))
    fm = lambda n: pl.BlockSpec((1, n, t), lambda b, i: (b, 0, i))
    outs = [
        (jax.ShapeDtypeStruct((B, 2, S, 256), BF16), pl.BlockSpec((1, 2, t, 256), lambda b, i: (b, 0, i, 0))),
        (jax.ShapeDtypeStruct((B, 256, S), BF16), fm(256)),
        (jax.ShapeDtypeStruct((B, 128, S), BF16), fm(128)),
        (jax.ShapeDtypeStruct((B, 256, S), BF16), fm(256)),
        (jax.ShapeDtypeStruct((B, S, 128), BF16), nat(128)),
        (jax.ShapeDtypeStruct((B, 256, S), BF16), fm(256)),
        (jax.ShapeDtypeStruct((B, 128, S), BF16), fm(128)),
        (jax.ShapeDtypeStruct((B, S, 256), BF16), nat(256)),
        (jax.ShapeDtypeStruct((B, 256, S), BF16), fm(256)),
        (jax.ShapeDtypeStruct((B, 256, S), BF16), fm(256)),
        (jax.ShapeDtypeStruct((B, S, HY_COLS), F32), nat(HY_COLS)),
    ]
    return pl.pallas_call(
        _proj_kernel, name="mixer_projection", grid=(B, S // t),
        in_specs=[nat(D),
                  pl.BlockSpec((1, 1, D), lambda b, i: (b, 0, 0)),
                  pl.BlockSpec((1, 1, D), lambda b, i: (b, 0, 0)),
                  pl.BlockSpec((1, D), lambda b, i: (0, 0)),
                  wspec(w_nat), wspec(w_fm), wspec(w_qn), wspec(w_qpa), wspec(w_qpb), wspec(w_kn), wspec(w_v),
                  nat(1024), fm(192)],
        out_specs=[o[1] for o in outs], out_shape=[o[0] for o in outs],
        compiler_params=_params("parallel", "parallel"),
    )(x, sc1, sh1, ln_g[layer].reshape(1, D), w_nat, w_fm, w_qn, w_qpa, w_qpb, w_kn, w_v, tab_nat, tab_fm)


def _flash_sweep(k_at, v_at, q_pad, n_blocks, acc_ref, m_ref, l_ref):
    m_ref[...] = jnp.full(m_ref.shape, NEG_INF, F32)
    l_ref[...] = jnp.zeros(l_ref.shape, F32)
    acc_ref[...] = jnp.zeros(acc_ref.shape, F32)

    def body(i, carry):
        s = jnp.dot(k_at(i), q_pad, preferred_element_type=F32)
        m_old = m_ref[...]
        m_new = jnp.maximum(m_old, jnp.max(s, axis=0, keepdims=True))
        alpha = jnp.exp2(m_old - m_new)
        p = jnp.exp2(s - m_new)
        l_ref[...] = alpha * l_ref[...] + jnp.sum(p, axis=0, keepdims=True)
        acc_ref[...] = alpha * acc_ref[...] + jnp.dot(v_at(i), p.astype(BF16), preferred_element_type=F32)
        m_ref[...] = m_new
        return carry

    lax.fori_loop(0, n_blocks, body, 0)
    return acc_ref[...] / l_ref[...]


def _mla_kernel(k_ref, qn_ref, qp_ref, v_ref, o_ref, qpad_ref, acc_ref, m_ref, l_ref, *, tk):
    slot = pl.program_id(1) % 2
    S = k_ref.shape[2]
    tq = qn_ref.shape[2]
    qn = qn_ref[0]
    zero = jnp.zeros_like(qn)
    qpad_ref[0:64, :] = jnp.where(slot == 0, qn, zero)
    qpad_ref[64:128, :] = jnp.where(slot == 1, qn, zero)
    qpad_ref[128:160, :] = qp_ref[0]
    qpad_ref[160:256, :] = jnp.zeros((96, tq), BF16)
    o_ref[0] = _flash_sweep(lambda i: k_ref[0, 0, pl.ds(pl.multiple_of(i * tk, tk), tk), :],
                            lambda i: v_ref[0, :, pl.ds(pl.multiple_of(i * tk, tk), tk)],
                            qpad_ref[...], S // tk, acc_ref, m_ref, l_ref)


def _mla_attention(k, qn, qp, v, tq=512, tk=512):
    B, _, S, _ = k.shape
    tq, tk = min(tq, S), min(tk, S)
    return pl.pallas_call(
        functools.partial(_mla_kernel, tk=tk), name="mla_attention", grid=(B, MLA_HEADS, S // tq),
        in_specs=[pl.BlockSpec((1, 1, S, 256), lambda b, h, i: (b, h // 2, 0, 0)),
                  pl.BlockSpec((1, 64, tq), lambda b, h, i: (b, h, i)),
                  pl.BlockSpec((1, 32, tq), lambda b, h, i: (b, h, i)),
                  pl.BlockSpec((1, 64, S), lambda b, h, i: (b, h, 0))],
        out_specs=pl.BlockSpec((1, 64, tq), lambda b, h, i: (b, h, i)),
        out_shape=jax.ShapeDtypeStruct((B, 256, S), F32),
        scratch_shapes=[pltpu.VMEM((256, tq), BF16), pltpu.VMEM((64, tq), F32),
                        pltpu.VMEM((1, tq), F32), pltpu.VMEM((1, tq), F32)],
        compiler_params=_params("parallel", "parallel", "arbitrary"))(k, qn, qp, v)


def _diff_kernel(lam_ref, g_ref, k_ref, q1_ref, q2_ref, v_ref, o_ref, qpad_ref, acc_ref, m_ref, l_ref, *, tk, lambda_init):
    head = pl.program_id(1)
    S = k_ref.shape[1]
    maps = []
    for j, q_ref in enumerate((q1_ref, q2_ref)):
        q = q_ref[0]
        qpad_ref[...] = jnp.zeros(qpad_ref.shape, BF16)
        for s in range(DIFF_HEADS):
            r0 = j * 128 + s * DIFF_QK
            qpad_ref[r0:r0 + DIFF_QK, :] = jnp.where(head == s, q, jnp.zeros_like(q))
        maps.append(_flash_sweep(lambda i: k_ref[0, pl.ds(pl.multiple_of(i * tk, tk), tk), :],
                                 lambda i: v_ref[0, :, pl.ds(pl.multiple_of(i * tk, tk), tk)],
                                 qpad_ref[...], S // tk, acc_ref, m_ref, l_ref))
    lv = lam_ref[...]
    lam = (jnp.exp(jnp.sum(lv[0:1] * lv[1:2], axis=1, keepdims=True))
           - jnp.exp(jnp.sum(lv[2:3] * lv[3:4], axis=1, keepdims=True)) + lambda_init)
    o = maps[0] - lam * maps[1]
    o = o * lax.rsqrt(jnp.mean(o * o, axis=0, keepdims=True) + EPS)
    o_ref[0] = o * g_ref[...] * (1.0 - lambda_init)


def _diff_attention(k, q, v, lam_vecs, norm_g, lambda_init, tq=512, tk=512):
    B, S, _ = k.shape
    tq, tk = min(tq, S), min(tk, S)
    return pl.pallas_call(
        functools.partial(_diff_kernel, tk=tk, lambda_init=lambda_init), name="diff_attention",
        grid=(B, DIFF_HEADS, S // tq),
        in_specs=[pl.BlockSpec((4, DIFF_QK), lambda b, h, i: (0, 0)),
                  pl.BlockSpec((DIFF_V, 1), lambda b, h, i: (0, 0)),
                  pl.BlockSpec((1, S, 256), lambda b, h, i: (b, 0, 0)),
                  pl.BlockSpec((1, 32, tq), lambda b, h, i: (b, h, i)),
                  pl.BlockSpec((1, 32, tq), lambda b, h, i: (b, DIFF_HEADS + h, i)),
                  pl.BlockSpec((1, 64, S), lambda b, h, i: (b, h, 0))],
        out_specs=pl.BlockSpec((1, 64, tq), lambda b, h, i: (b, h, i)),
        out_shape=jax.ShapeDtypeStruct((B, 256, S), F32),
        scratch_shapes=[pltpu.VMEM((256, tq), BF16), pltpu.VMEM((64, tq), F32),
                        pltpu.VMEM((1, tq), F32), pltpu.VMEM((1, tq), F32)],
        compiler_params=_params("parallel", "parallel", "arbitrary"))(lam_vecs, norm_g.reshape(DIFF_V, 1), k, q, q, v)


def _swa_kernel(sink_ref, q_ref, k0, k1, k2, k3, v0, v1, v2, v3, o_ref, *, S, tq):
    head = pl.program_id(1)
    qi = pl.program_id(2)
    q = q_ref[0]
    zero = jnp.zeros_like(q)
    q_pad = jnp.concatenate([jnp.where(head // 2 == 0, q, zero), jnp.where(head // 2 == 1, q, zero)], axis=0)
    k = jnp.concatenate([k0[0], k1[0], k2[0], k3[0]], axis=0)
    v = jnp.concatenate([v0[0], v1[0], v2[0], v3[0]], axis=1)
    s = jnp.dot(k, q_pad, preferred_element_type=F32)
    kpos = qi * tq - SWA_WINDOW + lax.broadcasted_iota(jnp.int32, s.shape, 0)
    qpos = qi * tq + lax.broadcasted_iota(jnp.int32, s.shape, 1)
    valid = (jnp.abs(kpos - qpos) <= SWA_WINDOW) & (kpos >= 0) & (kpos < S)
    s = jnp.where(valid, s, NEG_INF)
    sink = sink_ref[head] * LOG2E
    m = jnp.maximum(jnp.max(s, axis=0, keepdims=True), sink)
    e = jnp.exp2(s - m)
    denom = jnp.sum(e, axis=0, keepdims=True) + jnp.exp2(sink - m)
    o_ref[0] = jnp.dot(v, e.astype(BF16), preferred_element_type=F32) / denom


def _swa_attention(k, q, v, sink):
    B, S, _ = k.shape
    W = SWA_WINDOW
    tq = 2 * W
    nb = S // W

    def kidx(d):
        return lambda b, h, i: (b, jnp.clip(i * 2 - 1 + d, 0, nb - 1), 0)

    def vidx(d):
        return lambda b, h, i: (b, h // 2, jnp.clip(i * 2 - 1 + d, 0, nb - 1))

    return pl.pallas_call(
        functools.partial(_swa_kernel, S=S, tq=tq), name="swa_attention", grid=(B, SWA_HEADS, S // tq),
        in_specs=[pl.BlockSpec(memory_space=pltpu.SMEM),
                  pl.BlockSpec((1, 64, tq), lambda b, h, i: (b, h, i))]
                 + [pl.BlockSpec((1, W, 128), kidx(d)) for d in range(4)]
                 + [pl.BlockSpec((1, 64, W), vidx(d)) for d in range(4)],
        out_specs=pl.BlockSpec((1, 64, tq), lambda b, h, i: (b, h, i)),
        out_shape=jax.ShapeDtypeStruct((B, 256, S), F32),
        compiler_params=_params("parallel", "parallel", "parallel"))(sink, q, k, k, k, k, v, v, v, v)


def _hyena_constants(S):
    N = 2 * S
    P2 = DFT_P2
    P1 = N // P2
    t = np.linspace(0.0, 1.0, S)
    w = 2.0 * math.pi * np.arange(S) / S
    f = np.linspace(1e-4, HY_BANDS - 1, HY_BANDS)
    z = np.concatenate([t[:, None], np.cos(w[:, None] * f), -np.sin(w[:, None] * f), np.zeros((S, 40 - HY_EMB))], axis=1)
    decay = np.linspace(HY_MIN_DECAY, HY_MAX_DECAY, HY_CH)
    window = np.exp(-t[:, None] * np.abs(decay))
    k1 = np.arange(P1)
    n1 = np.arange(P1 // 2)
    a1 = 2.0 * math.pi * np.outer(k1, n1) / P1
    m_fwd = np.concatenate([np.cos(a1), -np.sin(a1)], axis=0)
    m_inv = np.concatenate([np.cos(a1.T), -np.sin(a1.T)], axis=1)
    a2 = 2.0 * math.pi * np.outer(np.arange(P2), np.arange(P2)) / P2
    c2, s2 = np.cos(a2), np.sin(a2)
    g = np.block([[c2, s2], [-s2, c2]])
    at = 2.0 * math.pi * np.outer(k1, np.arange(P2)) / N
    return dict(z=jnp.asarray(z, F32), window=jnp.asarray(window, F32),
                m_fwd=jnp.asarray(m_fwd, BF16), m_inv=jnp.asarray(m_inv, BF16),
                g=jnp.asarray(g, BF16), gt=jnp.asarray(g.T, BF16),
                tw_r=jnp.asarray(np.cos(at)[:, :, None], F32), tw_i=jnp.asarray(-np.sin(at)[:, :, None], F32))


def _filter_kernel(z_ref, win_ref, w1_ref, b1_ref, w2_ref, b2_ref, w3_ref, b3_ref, w4_ref, f_ref, o_ref, st_ref):
    dot = functools.partial(jnp.dot, preferred_element_type=F32, precision=HIGHEST)
    fr = f_ref[0]
    h = jnp.sin(fr[0:1] * (dot(z_ref[...], w1_ref[0]) + b1_ref[0]))
    h = jnp.sin(fr[1:2] * (dot(h, w2_ref[0]) + b2_ref[0]))
    h = jnp.sin(fr[2:3] * (dot(h, w3_ref[0]) + b3_ref[0]))
    out = dot(h, w4_ref[0])
    win = win_ref[...]
    parts = [out[:, q * HY_CH:(q + 1) * HY_CH] * win for q in range(4)]
    for q in range(4):
        o_ref[0, q] = parts[q]
    filt = jnp.concatenate(parts, axis=1)
    ssq = jnp.sum(filt * filt, axis=0, keepdims=True)

    @pl.when(pl.program_id(1) == 0)
    def _():
        cross = filt[0:1, 0:512] * filt[0:1, 512:1024]
        st_ref[0] = jnp.concatenate([ssq, jnp.concatenate([cross, jnp.zeros_like(cross)], axis=1),
                                     jnp.zeros((6, 1024), F32)], axis=0)

    @pl.when(pl.program_id(1) != 0)
    def _():
        st_ref[0, 0:1] = st_ref[0, 0:1] + ssq


def _hyena_filters(consts, hy_w1, hy_b1, hy_w2, hy_b2, hy_w3, hy_b3, hy_w4, hy_sin_freq, S):
    L = hy_w1.shape[0]
    ts = min(512, S)
    w1 = jnp.concatenate([hy_w1, jnp.zeros((L, 40 - HY_EMB, HY_FFN), F32)], axis=1)
    lw = lambda a: pl.BlockSpec((1,) + a.shape[1:], lambda l, i: (l,) + (0,) * (a.ndim - 1))
    b1, b2, b3 = (b.reshape(L, 1, HY_FFN) for b in (hy_b1, hy_b2, hy_b3))
    return pl.pallas_call(
        _filter_kernel, name="hyena_filters", grid=(L, S // ts),
        in_specs=[pl.BlockSpec((ts, 40), lambda l, i: (i, 0)), pl.BlockSpec((ts, HY_CH), lambda l, i: (i, 0)),
                  lw(w1), lw(b1), lw(hy_w2), lw(b2), lw(hy_w3), lw(b3), lw(hy_w4), lw(hy_sin_freq)],
        out_specs=[pl.BlockSpec((1, 4, ts, HY_CH), lambda l, i: (l, 0, i, 0)),
                   pl.BlockSpec((1, 8, 1024), lambda l, i: (l, 0, 0))],
        out_shape=[jax.ShapeDtypeStruct((L, 4, S, HY_CH), F32), jax.ShapeDtypeStruct((L, 8, 1024), F32)],
        compiler_params=_params("parallel", "arbitrary"),
    )(consts["z"], consts["window"], w1, b1, hy_w2, b2, hy_w3, b3, hy_w4, hy_sin_freq)


def _dft_rows_kernel(m_ref, x_ref, o_ref):
    a = jnp.dot(m_ref[...], x_ref[0].astype(BF16), preferred_element_type=F32)
    o_ref[0] = a.reshape(o_ref.shape[1:])


def _dft_rows(consts, x, lanes_per_step=4096):
    nb, p1h, width = x.shape
    tl = min(lanes_per_step, width)
    return pl.pallas_call(
        _dft_rows_kernel, name="hyena_dft_rows", grid=(nb, width // tl),
        in_specs=[pl.BlockSpec((4 * p1h, p1h), lambda b, j: (0, 0)),
                  pl.BlockSpec((1, p1h, tl), lambda b, j: (b, 0, j))],
        out_specs=pl.BlockSpec((1, 2, 2 * p1h, tl), lambda b, j: (b, 0, 0, j)),
        out_shape=jax.ShapeDtypeStruct((nb, 2, 2 * p1h, width), F32),
        compiler_params=_params("parallel", "parallel"))(consts["m_fwd"], x)


def _twiddled_spectrum(a_ref, kk, tr, ti, g):
    ar, ai = a_ref[0, 0, kk], a_ref[0, 1, kk]
    pre = jnp.concatenate([ar * tr - ai * ti, ar * ti + ai * tr], axis=0).astype(BF16)
    return jnp.dot(g, pre, preferred_element_type=F32)


def _filter_spectrum_kernel(g_ref, twr_ref, twi_ref, st_ref, af_ref, ab_ref, h_ref, *, order, n_total):
    P2 = DFT_P2
    st = st_ref[0]
    c0 = order * HY_CH
    ssq = st[0:1, c0:c0 + HY_CH] + st[0:1, 512 + c0:512 + c0 + HY_CH] + 2.0 * st[1:2, c0:c0 + HY_CH]
    scale = lax.rsqrt(ssq + EPS) * (1.0 / n_total)
    g = g_ref[...]
    for kk in range(af_ref.shape[2]):
        tr, ti = twr_ref[kk], twi_ref[kk]
        xf = _twiddled_spectrum(af_ref, kk, tr, ti, g)
        xb = _twiddled_spectrum(ab_ref, kk, tr, ti, g)
        h_ref[0, 0, kk] = (xf[:P2] + xb[:P2]) * scale
        h_ref[0, 1, kk] = (xf[P2:] - xb[P2:]) * scale


def _filter_spectra(consts, filters, stats, S, kb=8):
    L = filters.shape[0]
    P2, C = DFT_P2, HY_CH
    P1 = 2 * S // P2
    kb = min(kb, P1)
    a = _dft_rows(consts, filters.reshape(L * 4, P1 // 2, P2 * C)).reshape(L * 4, 2, P1, P2, C)
    outs = []
    for order in range(HY_ORDER):
        outs.append(pl.pallas_call(
            functools.partial(_filter_spectrum_kernel, order=order, n_total=2 * S),
            name=f"hyena_filter_spectrum_{order}", grid=(L, P1 // kb),
            in_specs=[pl.BlockSpec((2 * P2, 2 * P2), lambda l, i: (0, 0)),
                      pl.BlockSpec((kb, P2, 1), lambda l, i: (i, 0, 0)),
                      pl.BlockSpec((kb, P2, 1), lambda l, i: (i, 0, 0)),
                      pl.BlockSpec((1, 8, 1024), lambda l, i: (l, 0, 0)),
                      pl.BlockSpec((1, 2, kb, P2, C), lambda l, i: (l * 4 + order, 0, i, 0, 0)),
                      pl.BlockSpec((1, 2, kb, P2, C), lambda l, i: (l * 4 + 2 + order, 0, i, 0, 0))],
            out_specs=pl.BlockSpec((1, 2, kb, P2, C), lambda l, i: (l, 0, i, 0, 0)),
            out_shape=jax.ShapeDtypeStruct((L, 2, P1, P2, C), F32),
            compiler_params=_params("parallel", "parallel"),
        )(consts["g"], consts["tw_r"], consts["tw_i"], stats, a, a))
    return outs


def _spectral_filter_kernel(g_ref, gt_ref, twr_ref, twi_ref, h_ref, a_ref, o_ref):
    P2 = DFT_P2
    g, gt = g_ref[...], gt_ref[...]
    for kk in range(a_ref.shape[2]):
        tr, ti = twr_ref[kk], twi_ref[kk]
        x = _twiddled_spectrum(a_ref, kk, tr, ti, g)
        xr, xi = x[:P2], x[P2:]
        hr, hi = h_ref[0, 0, kk], h_ref[0, 1, kk]
        y = jnp.concatenate([xr * hr - xi * hi, xr * hi + xi * hr], axis=0).astype(BF16)
        bm = jnp.dot(gt, y, preferred_element_type=F32)
        br, bi = bm[:P2], bm[P2:]
        o_ref[0, 0, kk] = br * tr + bi * ti
        o_ref[0, 1, kk] = bi * tr - br * ti


def _spectral_filter(consts, a, h, layer, kb=8):
    B, _, P1, P2, C = a.shape
    kb = min(kb, P1)
    blk = lambda f: pl.BlockSpec((1, 2, kb, P2, C), f)
    return pl.pallas_call(
        _spectral_filter_kernel, name="hyena_spectral_filter", grid=(B, P1 // kb),
        in_specs=[pl.BlockSpec((2 * P2, 2 * P2), lambda b, i: (0, 0)),
                  pl.BlockSpec((2 * P2, 2 * P2), lambda b, i: (0, 0)),
                  pl.BlockSpec((kb, P2, 1), lambda b, i: (i, 0, 0)),
                  pl.BlockSpec((kb, P2, 1), lambda b, i: (i, 0, 0)),
                  blk(lambda b, i: (layer, 0, i, 0, 0)), blk(lambda b, i: (b, 0, i, 0, 0))],
        out_specs=blk(lambda b, i: (b, 0, i, 0, 0)),
        out_shape=jax.ShapeDtypeStruct(a.shape, F32),
        compiler_params=_params("parallel", "parallel"),
    )(consts["g"], consts["gt"], consts["tw_r"], consts["tw_i"], h, a)


def _idft_rows_kernel(m_ref, b_ref, u_ref, gate_ref, bias_ref, o_ref):
    bm = b_ref[0]
    y = jnp.dot(m_ref[...], bm.reshape(bm.shape[0] * bm.shape[1], bm.shape[2]).astype(BF16), preferred_element_type=F32)
    o_ref[0] = gate_ref[0] * (y + bias_ref[...] * u_ref[0])


def _idft_rows_gated(consts, bm, u, gate, bias, lanes_per_step=4096):
    B, p1h, width = u.shape
    tl = min(lanes_per_step, width)
    row = lambda: pl.BlockSpec((1, p1h, tl), lambda b, j: (b, 0, j))
    return pl.pallas_call(
        _idft_rows_kernel, name="hyena_idft_rows", grid=(B, width // tl),
        in_specs=[pl.BlockSpec((p1h, 4 * p1h), lambda b, j: (0, 0)),
                  pl.BlockSpec((1, 2, 2 * p1h, tl), lambda b, j: (b, 0, 0, j)),
                  row(), row(), pl.BlockSpec((1, tl), lambda b, j: (0, 0))],
        out_specs=row(), out_shape=jax.ShapeDtypeStruct(u.shape, F32),
        compiler_params=_params("parallel", "parallel"),
    )(consts["m_inv"], bm, u, gate, jnp.tile(bias.reshape(1, HY_CH), (1, tl // HY_CH)))


def _short_conv_kernel(x_ref, p_ref, n_ref, w_ref, b_ref, v_ref, x1_ref, x2_ref):
    i = pl.program_id(1)
    x = x_ref[0]
    t = x.shape[0]
    row = lax.broadcasted_iota(jnp.int32, x.shape, 0)
    prev_row = jnp.where(i == 0, 0.0, p_ref[0, 7:8, :])
    next_row = jnp.where(i == pl.num_programs(1) - 1, 0.0, n_ref[0, 0:1, :])
    before = jnp.where(row == 0, prev_row, pltpu.roll(x, 1, axis=0))
    after = jnp.where(row == t - 1, next_row, pltpu.roll(x, t - 1, axis=0))
    w = w_ref[...]
    u = before * w[0:1] + x * w[1:2] + after * w[2:3] + b_ref[...]
    v_ref[0] = u[:, 0:HY_CH]
    x1_ref[0] = u[:, HY_CH:2 * HY_CH]
    x2_ref[0] = u[:, 2 * HY_CH:3 * HY_CH]


def _short_conv(cols, conv_w, conv_b, t=512):
    B, S, W = cols.shape
    t = min(t, S)
    r = t // 8
    out = pl.BlockSpec((1, t, HY_CH), lambda b, i: (b, i, 0))
    return pl.pallas_call(
        _short_conv_kernel, name="hyena_short_conv", grid=(B, S // t),
        in_specs=[pl.BlockSpec((1, t, W), lambda b, i: (b, i, 0)),
                  pl.BlockSpec((1, 8, W), lambda b, i: (b, jnp.maximum(i * r - 1, 0), 0)),
                  pl.BlockSpec((1, 8, W), lambda b, i: (b, jnp.minimum((i + 1) * r, S // 8 - 1), 0)),
                  pl.BlockSpec((HY_SHORT, W), lambda b, i: (0, 0)), pl.BlockSpec((1, W), lambda b, i: (0, 0))],
        out_specs=[out, out, out], out_shape=[jax.ShapeDtypeStruct((B, S, HY_CH), F32)] * 3,
        compiler_params=_params("parallel", "parallel"))(cols, cols, cols, conv_w, conv_b.reshape(1, W))


def _hyena_mixer(consts, cols, conv_w, conv_b, spectra, bias, layer):
    B, S, _ = cols.shape
    P2, C = DFT_P2, HY_CH
    P1 = 2 * S // P2
    v, x1, x2 = _short_conv(cols, conv_w, conv_b)
    rows = lambda a: a.reshape(B, P1 // 2, P2 * C)
    z = rows(v)
    for order, gate in enumerate((x1, x2)):
        a = _dft_rows(consts, z).reshape(B, 2, P1, P2, C)
        bm = _spectral_filter(consts, a, spectra[order], layer).reshape(B, 2, P1, P2 * C)
        z = _idft_rows_gated(consts, bm, z, rows(gate), bias[order])
    return z.reshape(B, S, C)


def _route(lt):
    t = lt.shape[1]
    row = lax.broadcasted_iota(jnp.int32, (8, t), 0)
    lg = jnp.where(row < N_GROUPS, lt[0:8], NEG_INF)
    mg = jnp.max(lg, axis=0, keepdims=True)
    p_top = 1.0 / jnp.sum(jnp.exp(lg - mg), axis=0, keepdims=True)
    g_sel = jnp.min(jnp.where(lg == mg, row, 8), axis=0, keepdims=True)
    le = jnp.zeros((8, t), F32)
    for g in range(N_GROUPS):
        le = jnp.where(g_sel == g, lt[8 + 8 * g:16 + 8 * g], le)
    v1 = jnp.max(le, axis=0, keepdims=True)
    i1 = jnp.min(jnp.where(le == v1, row, 8), axis=0, keepdims=True)
    rest = jnp.where(row == i1, NEG_INF, le)
    v2 = jnp.max(rest, axis=0, keepdims=True)
    i2 = jnp.min(jnp.where(rest == v2, row, 8), axis=0, keepdims=True)
    e = jnp.exp(v2 - v1)
    w1 = 1.0 / (1.0 + e)
    base = g_sel * EXPERTS_PER_GROUP
    idx = jnp.where(row == 0, base + i1, jnp.where(row == 1, base + i2, 0))
    gate = jnp.where(row == 0, p_top * w1, jnp.where(row == 1, p_top * (e * w1), 0.0))
    return idx, gate


def _post_kernel(x_ref, g1_ref, sc_ref, sh_ref, ln_ref, a_ref, b_ref, c_ref, d_ref, wo_ref, wr_ref, br_ref,
                 xo_ref, h_ref, idx_ref, gate_ref):
    wo = wo_ref[0]
    y = (_tn(a_ref[0].astype(BF16), wo[0:256]) + _tn(b_ref[0].astype(BF16), wo[256:512])
         + _tn(c_ref[0].astype(BF16), wo[512:768])
         + jnp.dot(d_ref[0].astype(BF16), wo[768:1024], preferred_element_type=F32))
    x = x_ref[0] + g1_ref[0] * y
    xo_ref[0] = x
    h = _rms(x) * ln_ref[...] * (1.0 + sc_ref[0]) + sh_ref[0]
    h_ref[0] = h.astype(BF16)
    lt = lax.dot_general(wr_ref[0], h, (((1,), (1,)), ((), ())), preferred_element_type=F32, precision=HIGHEST) + br_ref[0]
    idx, gate = _route(lt)
    idx_ref[...] = idx
    gate_ref[...] = gate


def _post_mixer(x, g1, sc2, sh2, ln_g, layer, mix_a, mix_b, mix_c, mix_d, w_out, w_router, b_router, t=256):
    B, S, D = x.shape
    t = min(t, S)
    nt = S // t
    nat = lambda n: pl.BlockSpec((1, t, n), lambda b, i: (b, i, 0))
    fm = pl.BlockSpec((1, 256, t), lambda b, i: (b, 0, i))
    mod = pl.BlockSpec((1, 1, D), lambda b, i: (b, 0, 0))
    lw = lambda a: pl.BlockSpec((1,) + a.shape[1:], lambda b, i: (layer,) + (0,) * (a.ndim - 1))
    tok = pl.BlockSpec((8, t), lambda b, i: (0, b * nt + i))
    return pl.pallas_call(
        _post_kernel, name="mixer_output_router", grid=(B, nt),
        in_specs=[nat(D), mod, mod, mod, pl.BlockSpec((1, D), lambda b, i: (0, 0)), fm, fm, fm, nat(256),
                  lw(w_out), lw(w_router), lw(b_router)],
        out_specs=[nat(D), nat(D), tok, tok],
        out_shape=[jax.ShapeDtypeStruct((B, S, D), F32), jax.ShapeDtypeStruct((B, S, D), BF16),
                   jax.ShapeDtypeStruct((8, B * S), jnp.int32), jax.ShapeDtypeStruct((8, B * S), F32)],
        compiler_params=_params("parallel", "parallel"),
    )(x, g1, sc2, sh2, ln_g[layer].reshape(1, D), mix_a, mix_b, mix_c, mix_d, w_out, w_router, b_router)


def _expert_kernel(be_ref, x_ref, wg_ref, wu_ref, wd_ref, o_ref):
    x = x_ref[...]
    g = jnp.dot(x, wg_ref[0, 0].astype(BF16), preferred_element_type=F32)
    u = jnp.dot(x, wu_ref[0, 0].astype(BF16), preferred_element_type=F32)
    a = (g * jax.nn.sigmoid(g) * u).astype(BF16)
    o_ref[...] = jnp.dot(a, wd_ref[0, 0].astype(BF16), preferred_element_type=F32)


def _expert_ffn(xg, block_expert, w_gate, w_up, w_down, layer):
    P, D = xg.shape
    n_blocks = P // MOE_ROWS
    wmap = lambda i, be: (layer, be[i], 0, 0)
    return pl.pallas_call(
        _expert_kernel, name="expert_ffn",
        grid_spec=pltpu.PrefetchScalarGridSpec(
            num_scalar_prefetch=1, grid=(n_blocks,),
            in_specs=[pl.BlockSpec((MOE_ROWS, D), lambda i, be: (i, 0)),
                      pl.BlockSpec((1, 1, D, D_EXPERT), wmap), pl.BlockSpec((1, 1, D, D_EXPERT), wmap),
                      pl.BlockSpec((1, 1, D_EXPERT, D), wmap)],
            out_specs=pl.BlockSpec((MOE_ROWS, D), lambda i, be: (i, 0))),
        out_shape=jax.ShapeDtypeStruct((P, D), F32),
        compiler_params=_params("arbitrary"))(block_expert, xg, w_gate, w_up, w_down)


def _dispatch_plan(idx, T):
    e = idx[0:TOP_K]
    onehot = (e.reshape(-1)[:, None] == jnp.arange(N_EXPERTS)[None]).astype(jnp.int32)
    before = jnp.cumsum(onehot, axis=0) - onehot
    rank = jnp.sum(before * onehot, axis=1)
    counts = jnp.sum(onehot, axis=0)
    padded = (counts + MOE_ROWS - 1) // MOE_ROWS * MOE_ROWS
    pad_end = jnp.cumsum(padded)
    dest = ((pad_end - padded)[e.reshape(-1)] + rank).reshape(TOP_K, T)
    n_blocks = (TOP_K * T + N_EXPERTS * (MOE_ROWS - 1) + MOE_ROWS - 1) // MOE_ROWS
    tok = jnp.tile(jnp.arange(T, dtype=jnp.int32), TOP_K)
    slot_tok = jnp.full((n_blocks * MOE_ROWS,), T, jnp.int32).at[dest.reshape(-1)].set(tok)
    block_expert = jnp.minimum(
        jnp.searchsorted(pad_end, jnp.arange(n_blocks, dtype=jnp.int32) * MOE_ROWS, side="right"), N_EXPERTS - 1)
    return dest, slot_tok, block_expert.astype(jnp.int32)


def _combine_kernel(x_ref, g2_ref, y0_ref, y1_ref, gate_ref, fg_ref, o_ref, *, final):
    gt = gate_ref[...]
    x = x_ref[0] + g2_ref[0] * (gt[:, 0:1] * y0_ref[0] + gt[:, 1:2] * y1_ref[0])
    if final:
        x = _rms(x) * fg_ref[...]
    o_ref[0] = x


def _combine(x, g2, y0, y1, gate_t, final_g, final, t=512):
    B, S, D = x.shape
    t = min(t, S)
    nt = S // t
    nat = pl.BlockSpec((1, t, D), lambda b, i: (b, i, 0))
    return pl.pallas_call(
        functools.partial(_combine_kernel, final=final), name="moe_combine", grid=(B, nt),
        in_specs=[nat, pl.BlockSpec((1, 1, D), lambda b, i: (b, 0, 0)), nat, nat,
                  pl.BlockSpec((t, 8), lambda b, i: (b * nt + i, 0)), pl.BlockSpec((1, D), lambda b, i: (0, 0))],
        out_specs=nat, out_shape=jax.ShapeDtypeStruct((B, S, D), F32),
        compiler_params=_params("parallel", "parallel"))(x, g2, y0, y1, gate_t, final_g.reshape(1, D))


def _moe(x, h, idx, gate, g2, w_gate, w_up, w_down, layer, final_g, final):
    B, S, D = x.shape
    T = B * S
    dest, slot_tok, block_expert = _dispatch_plan(idx, T)
    h_pad = jnp.concatenate([h.reshape(T, D), jnp.zeros((1, D), h.dtype)], axis=0)
    yb = _expert_ffn(h_pad[slot_tok], block_expert, w_gate, w_up, w_down, layer)
    y0 = yb[dest[0]].reshape(B, S, D)
    y1 = yb[dest[1]].reshape(B, S, D)
    return _combine(x, g2, y0, y1, gate.T, final_g, final)


def kernel(x, c, positions, ln1_g, ln2_g, ada_w, ada_b, w_in, mla_q_norm, mla_kv_norm, mla_w_uq, mla_w_ukv, swa_sink, diff_lambda, diff_norm_g, hy_conv_w, hy_conv_b, hy_w1, hy_b1, hy_w2, hy_b2, hy_w3, hy_b3, hy_w4, hy_sin_freq, hy_bias, w_out, router_g_w, router_g_b, router_e_w, router_e_b, moe_w_gate, moe_w_up, moe_w_down, final_g):
    B, S, D = x.shape
    L = w_in.shape[0]
    mod = _ada_modulation(c, ada_w, ada_b)[:, :B].reshape(L, B, 6, 1, D)
    tab_fm = _rope_tables(positions)
    tab_nat = _natural_rope_table(tab_fm)
    weights = _layer_weight_layouts(w_in, mla_q_norm, mla_kv_norm, mla_w_uq, mla_w_ukv)
    w_out_b = w_out.astype(BF16)
    w_router = jnp.concatenate([router_g_w.transpose(0, 2, 1), jnp.zeros((L, 8 - N_GROUPS, D), F32),
                                router_e_w.transpose(0, 2, 1)], axis=1)
    b_router = jnp.concatenate([router_g_b, jnp.zeros((L, 8 - N_GROUPS), F32), router_e_b], axis=1)[:, :, None]
    consts = _hyena_constants(S)
    filters, stats = _hyena_filters(consts, hy_w1, hy_b1, hy_w2, hy_b2, hy_w3, hy_b3, hy_w4, hy_sin_freq, S)
    spectra = _filter_spectra(consts, filters, stats, S)
    for l in range(L):
        sh1, sc1, g1, sh2, sc2, g2 = (mod[l, :, i] for i in range(6))
        kmla, qn, qp, vmla, kswa, qswa, vswa, kdiff, qdiff, vdiff, hy = _project(
            x, sc1, sh1, ln1_g, l, weights, tab_nat, tab_fm)
        mix_a = _mla_attention(kmla, qn, qp, vmla)
        mix_b = _swa_attention(kswa, qswa, vswa, swa_sink[l])
        lambda_init = 0.8 - 0.6 * math.exp(-0.3 * l)
        mix_c = _diff_attention(kdiff, qdiff, vdiff, diff_lambda[l], diff_norm_g[l], lambda_init)
        mix_d = _hyena_mixer(consts, hy, hy_conv_w[l], hy_conv_b[l], spectra, hy_bias[l], l)
        x, h, idx, gate = _post_mixer(x, g1, sc2, sh2, ln2_g, l, mix_a, mix_b, mix_c, mix_d, w_out_b, w_router, b_router)
        x = _moe(x, h, idx, gate, g2, moe_w_gate, moe_w_up, moe_w_down, l, final_g, final=(l == L - 1))
    return x
```

```python
import functools
import math

import numpy as np
import jax
import jax.numpy as jnp
from jax import lax
from jax.experimental import pallas as pl
from jax.experimental.pallas import tpu as pltpu

F32 = jnp.float32
BF16 = jnp.bfloat16
HIGHEST = lax.Precision.HIGHEST

HEAD_DIM = 64
ROPE_THETA = 10000.0
EPS = 1e-6
NEG_INF = -1e30
LOG2E = 1.4426950408889634
MLA_HEADS, MLA_Q_RANK, MLA_KV_RANK, MLA_NOPE, MLA_ROPE, MLA_V = 4, 256, 128, 64, 32, 64
SWA_HEADS, SWA_KV_HEADS, SWA_WINDOW = 4, 2, 128
DIFF_HEADS, DIFF_QK, DIFF_V = 4, 32, 64
HY_CH, HY_ORDER, HY_DIRS, HY_SHORT, HY_EMB, HY_FFN = 256, 2, 2, 3, 33, 64
HY_BANDS = (HY_EMB - 1) // 2
HY_MIN_DECAY = math.log(1e-2) / 1.5
HY_MAX_DECAY = math.log(1e-2) / 0.3
MLA_COLS = MLA_Q_RANK + MLA_KV_RANK + MLA_ROPE
SWA_COLS = (SWA_HEADS + 2 * SWA_KV_HEADS) * HEAD_DIM
DIFF_COLS = DIFF_HEADS * (4 * DIFF_QK + DIFF_V)
HY_COLS = (HY_ORDER + 1) * HY_CH
N_GROUPS, EXPERTS_PER_GROUP, TOP_K, D_EXPERT = 4, 8, 2, 256
N_EXPERTS = N_GROUPS * EXPERTS_PER_GROUP

V7X_VMEM_LIMIT_BYTES = 56 * 1024 * 1024
LANES = 128
DFT_P2 = 128
MOE_ROWS = 512


def _params(*sem):
    return pltpu.CompilerParams(dimension_semantics=sem, vmem_limit_bytes=V7X_VMEM_LIMIT_BYTES)


def _nt(a, b):
    return lax.dot_general(a, b, (((1,), (1,)), ((), ())), preferred_element_type=F32)


def _tn(a, b):
    return lax.dot_general(a, b, (((0,), (0,)), ((), ())), preferred_element_type=F32)


def _ada_kernel(c_ref, w_ref, b_ref, o_ref):
    c = c_ref[...]
    act = c * jax.nn.sigmoid(c)
    o_ref[0] = jnp.dot(act, w_ref[0], preferred_element_type=F32, precision=HIGHEST) + b_ref[0]


def _ada_modulation(c, ada_w, ada_b):
    L, D, N = ada_w.shape
    B = c.shape[0]
    c8 = jnp.zeros((8, D), F32).at[:B].set(c)
    tn = 1024
    return pl.pallas_call(
        _ada_kernel, name="ada_modulation", grid=(L, N // tn),
        in_specs=[pl.BlockSpec((8, D), lambda l, j: (0, 0)),
                  pl.BlockSpec((1, D, tn), lambda l, j: (l, 0, j)),
                  pl.BlockSpec((1, 1, tn), lambda l, j: (l, 0, j))],
        out_specs=pl.BlockSpec((1, 8, tn), lambda l, j: (l, 0, j)),
        out_shape=jax.ShapeDtypeStruct((L, 8, N), F32),
        compiler_params=_params("parallel", "parallel"))(c8, ada_w, ada_b.reshape(L, 1, N))


def _rope_table_kernel(p_ref, o_ref):
    pos = p_ref[0].astype(F32)
    for half, base in ((32, 0), (16, 128)):
        j = lax.broadcasted_iota(jnp.int32, (half, 1), 0).astype(F32)
        inv = jnp.exp(-math.log(ROPE_THETA) * j / half)
        ang = inv * pos
        cs, sn = jnp.cos(ang), jnp.sin(ang)
        o_ref[0, base:base + half] = cs
        o_ref[0, base + half:base + 2 * half] = cs
        o_ref[0, base + 2 * half:base + 3 * half] = -sn
        o_ref[0, base + 3 * half:base + 4 * half] = sn


def _rope_tables(positions):
    B, S = positions.shape
    return pl.pallas_call(
        _rope_table_kernel, name="rope_tables", grid=(B,),
        in_specs=[pl.BlockSpec((1, 1, S), lambda b: (b, 0, 0))],
        out_specs=pl.BlockSpec((1, 192, S), lambda b: (b, 0, 0)),
        out_shape=jax.ShapeDtypeStruct((B, 192, S), F32),
        compiler_params=_params("parallel"))(positions.reshape(B, 1, S))


def _partner(n_heads, dim):
    r = np.arange(n_heads * dim)
    return (r // dim) * dim + (r % dim + dim // 2) % dim


def _layer_weight_layouts(w_in, mla_q_norm, mla_kv_norm, mla_w_uq, mla_w_ukv):
    L, D, _ = w_in.shape
    oB = MLA_COLS
    oC = oB + SWA_COLS
    oD = oC + DIFF_COLS
    Z = w_in.shape[2]
    wz = jnp.concatenate([w_in, jnp.zeros((L, D, 1), F32)], axis=2)

    kpe = MLA_Q_RANK + MLA_KV_RANK + np.arange(MLA_ROPE)
    pad96 = np.full(LANES - MLA_ROPE, Z)
    swa_k = oB + SWA_HEADS * HEAD_DIM + np.arange(SWA_KV_HEADS * HEAD_DIM)
    h, j, r = np.meshgrid(np.arange(DIFF_HEADS), np.arange(2), np.arange(DIFF_QK), indexing="ij")
    diff_src = (h * 2 * DIFF_QK + j * DIFF_QK + r).transpose(1, 0, 2).reshape(-1)
    diff_q = oC + diff_src
    diff_k = oC + DIFF_HEADS * 2 * DIFF_QK + diff_src
    nat_idx = np.concatenate([
        np.arange(MLA_Q_RANK + MLA_KV_RANK),
        kpe, pad96, kpe[_partner(1, MLA_ROPE)], pad96,
        swa_k, swa_k[_partner(SWA_KV_HEADS, HEAD_DIM)],
        diff_k, diff_k[_partner(2 * DIFF_HEADS, DIFF_QK)],
        oD + np.arange(HY_COLS)])
    w_nat = jnp.take(wz, jnp.asarray(nat_idx), axis=2).astype(BF16)

    swa_q = oB + np.arange(SWA_HEADS * HEAD_DIM)
    swa_v = oB + (SWA_HEADS + SWA_KV_HEADS) * HEAD_DIM + np.arange(SWA_KV_HEADS * HEAD_DIM)
    diff_v = oC + DIFF_HEADS * 4 * DIFF_QK + np.arange(DIFF_HEADS * DIFF_V)
    fm_idx = np.concatenate([
        swa_q, swa_q[_partner(SWA_HEADS, HEAD_DIM)], swa_v,
        diff_q, diff_q[_partner(2 * DIFF_HEADS, DIFF_QK)], diff_v])
    s_swa = HEAD_DIM ** -0.5 * LOG2E
    s_diff = DIFF_QK ** -0.5 * LOG2E
    fm_scale = np.concatenate([np.full(512, s_swa), np.ones(128), np.full(512, s_diff), np.ones(256)]).astype(np.float32)
    w_fm = (jnp.take(wz, jnp.asarray(fm_idx), axis=2) * fm_scale).transpose(0, 2, 1).astype(BF16)

    s_mla = (MLA_NOPE + MLA_ROPE) ** -0.5 * LOG2E
    uq = mla_w_uq * mla_q_norm[:, :, None] * s_mla
    hq = np.arange(MLA_HEADS)[:, None] * (MLA_NOPE + MLA_ROPE)
    qn_idx = (hq + np.arange(MLA_NOPE)[None]).reshape(-1)
    qp_idx = (hq + MLA_NOPE + np.arange(MLA_ROPE)[None]).reshape(-1)
    w_qn = jnp.take(uq, jnp.asarray(qn_idx), axis=2).transpose(0, 2, 1).astype(BF16)
    w_qpa = jnp.take(uq, jnp.asarray(qp_idx), axis=2).transpose(0, 2, 1).astype(BF16)
    w_qpb = jnp.take(uq, jnp.asarray(qp_idx[_partner(MLA_HEADS, MLA_ROPE)]), axis=2).transpose(0, 2, 1).astype(BF16)
    ukv = mla_w_ukv * mla_kv_norm[:, :, None]
    hk = np.arange(MLA_HEADS)[:, None] * (MLA_NOPE + MLA_V)
    kn_idx = (hk + np.arange(MLA_NOPE)[None]).reshape(-1)
    v_idx = (hk + MLA_NOPE + np.arange(MLA_V)[None]).reshape(-1)
    w_kn = jnp.take(ukv, jnp.asarray(kn_idx), axis=2).astype(BF16)
    w_v = jnp.take(ukv, jnp.asarray(v_idx), axis=2).transpose(0, 2, 1).astype(BF16)
    return w_nat, w_fm, w_qn, w_qpa, w_qpb, w_kn, w_v


def _natural_rope_table(tab_fm):
    t = tab_fm.transpose(0, 2, 1)
    cos64, sin64, cos32, sin32 = t[..., 0:64], t[..., 64:128], t[..., 128:160], t[..., 160:192]
    z = jnp.zeros(t.shape[:2] + (LANES - MLA_ROPE,), F32)
    return jnp.concatenate([cos32, z, sin32, z,
                            jnp.tile(cos64, (1, 1, 2)), jnp.tile(sin64, (1, 1, 2)),
                            jnp.tile(cos32, (1, 1, 8)), jnp.tile(sin32, (1, 1, 8))], axis=-1)


def _rms(x):
    return x * lax.rsqrt(jnp.mean(x * x, axis=-1, keepdims=True) + EPS)


def _proj_kernel(x_ref, sc_ref, sh_ref, g_ref, wn_ref, wf_ref, wqn_ref, wqpa_ref, wqpb_ref, wkn_ref, wv_ref,
                 tn_ref, tf_ref,
                 kmla_ref, qn_ref, qp_ref, vmla_ref, kswa_ref, qswa_ref, vswa_ref, kdiff_ref, qdiff_ref, vdiff_ref, hy_ref):
    x = x_ref[0]
    t = x.shape[0]
    h = _rms(x) * g_ref[...] * (1.0 + sc_ref[0]) + sh_ref[0]
    hb = h.astype(BF16)
    cn = jnp.dot(hb, wn_ref[0], preferred_element_type=F32)
    tab = tn_ref[0]
    cqn = _rms(cn[:, 0:256]).astype(BF16)
    ckvn = _rms(cn[:, 256:384]).astype(BF16)
    kpe = (cn[:, 384:512] * tab[:, 0:128] + cn[:, 512:640] * tab[:, 128:256]).astype(BF16)
    kswa_ref[0] = (cn[:, 640:768] * tab[:, 256:384] + cn[:, 768:896] * tab[:, 384:512]).astype(BF16)
    kdiff_ref[0] = (cn[:, 896:1152] * tab[:, 512:768] + cn[:, 1152:1408] * tab[:, 768:1024]).astype(BF16)
    hy_ref[0] = cn[:, 1408:2176]
    kn = jnp.dot(ckvn, wkn_ref[0], preferred_element_type=F32).astype(BF16)
    kmla_ref[0, 0] = jnp.concatenate([kn[:, 0:128], kpe], axis=1)
    kmla_ref[0, 1] = jnp.concatenate([kn[:, 128:256], kpe], axis=1)

    ft = _nt(wf_ref[0], hb)
    tf = tf_ref[0]
    cos64, sin64, cos32, sin32 = tf[0:64], tf[64:128], tf[128:160], tf[160:192]
    qswa = ft[0:256].reshape(4, 64, t) * cos64[None] + ft[256:512].reshape(4, 64, t) * sin64[None]
    qswa_ref[0] = qswa.reshape(256, t).astype(BF16)
    vswa_ref[0] = ft[512:640].astype(BF16)
    qdiff = ft[640:896].reshape(8, 32, t) * cos32[None] + ft[896:1152].reshape(8, 32, t) * sin32[None]
    qdiff_ref[0] = qdiff.reshape(256, t).astype(BF16)
    vdiff_ref[0] = _with_ones_rows(ft[1152:1408], t)
    qn_ref[0] = _nt(wqn_ref[0], cqn).astype(BF16)
    qp = _nt(wqpa_ref[0], cqn).reshape(4, 32, t) * cos32[None] + _nt(wqpb_ref[0], cqn).reshape(4, 32, t) * sin32[None]
    qp_ref[0] = qp.reshape(128, t).astype(BF16)
    vmla_ref[0] = _with_ones_rows(_nt(wv_ref[0], ckvn), t)


def _project(x, sc1, sh1, ln_g, layer, weights, tab_nat, tab_fm, t=256):
    B, S, D = x.shape
    w_nat, w_fm, w_qn, w_qpa, w_qpb, w_kn, w_v = weights
    wspec = lambda a: pl.BlockSpec((1,) + a.shape[1:], lambda b, i: (layer,) + (0,) * (a.ndim - 1))
    nat = lambda n: pl.BlockSpec((1, t, n), lambda b, i: (b, i, 0))
    fm = lambda n: pl.BlockSpec((1, n, t), lambda b, i: (b, 0, i))
    outs = [
        (jax.ShapeDtypeStruct((B, 2, S, 256), BF16), pl.BlockSpec((1, 2, t, 256), lambda b, i: (b, 0, i, 0))),
        (jax.ShapeDtypeStruct((B, 256, S), BF16), fm(256)),
        (jax.ShapeDtypeStruct((B, 128, S), BF16), fm(128)),
        (jax.ShapeDtypeStruct((B, 4 * V_ROWS, S), BF16), fm(4 * V_ROWS)),
        (jax.ShapeDtypeStruct((B, S, 128), BF16), nat(128)),
        (jax.ShapeDtypeStruct((B, 256, S), BF16), fm(256)),
        (jax.ShapeDtypeStruct((B, 128, S), BF16), fm(128)),
        (jax.ShapeDtypeStruct((B, S, 256), BF16), nat(256)),
        (jax.ShapeDtypeStruct((B, 256, S), BF16), fm(256)),
        (jax.ShapeDtypeStruct((B, 4 * V_ROWS, S), BF16), fm(4 * V_ROWS)),
        (jax.ShapeDtypeStruct((B, S, HY_COLS), F32), nat(HY_COLS)),
    ]
    return pl.pallas_call(
        _proj_kernel, name="mixer_projection", grid=(B, S // t),
        in_specs=[nat(D),
                  pl.BlockSpec((1, 1, D), lambda b, i: (b, 0, 0)),
                  pl.BlockSpec((1, 1, D), lambda b, i: (b, 0, 0)),
                  pl.BlockSpec((1, D), lambda b, i: (0, 0)),
                  wspec(w_nat), wspec(w_fm), wspec(w_qn), wspec(w_qpa), wspec(w_qpb), wspec(w_kn), wspec(w_v),
                  nat(1024), fm(192)],
        out_specs=[o[1] for o in outs], out_shape=[o[0] for o in outs],
        compiler_params=_params("parallel", "parallel"),
    )(x, sc1, sh1, ln_g[layer].reshape(1, D), w_nat, w_fm, w_qn, w_qpa, w_qpb, w_kn, w_v, tab_nat, tab_fm)


V_ROWS = 80


def _with_ones_rows(v, t):
    tail = (lax.broadcasted_iota(jnp.int32, (V_ROWS - 64, t), 0) == 0).astype(F32)
    parts = []
    for h in range(v.shape[0] // 64):
        parts += [v[h * 64:(h + 1) * 64], tail]
    return jnp.concatenate(parts, axis=0).astype(BF16)


def _flash_scratch(tk, tq):
    return ([pltpu.VMEM((256, tq), BF16)] + [pltpu.VMEM((tk, tq), F32)] * 4 + [pltpu.VMEM((tk, tq), BF16)] * 4
            + [pltpu.VMEM((1, tq), F32)] * 4 + [pltpu.VMEM((V_ROWS, tq), F32), pltpu.VMEM((1, tq), F32)])


def _flash_sweep(k_at, v_at, q_pad, n_blocks, scratch):
    s, p, al = scratch[0:4], scratch[4:8], scratch[8:12]
    acc_ref, m_ref = scratch[12], scratch[13]

    def scores(i, b):
        s[b][...] = jnp.dot(k_at(i), q_pad, preferred_element_type=F32)

    def softmax(src, dst):
        x = s[src][...]
        m_old = m_ref[...]
        m_new = jnp.maximum(m_old, jnp.max(x, axis=0, keepdims=True))
        al[dst][...] = jnp.exp2(m_old - m_new)
        p[dst][...] = jnp.exp2(x - m_new).astype(BF16)
        m_ref[...] = m_new

    def values(i, b):
        acc_ref[...] = al[b][...] * acc_ref[...] + jnp.dot(v_at(i), p[b][...], preferred_element_type=F32)

    m_ref[...] = jnp.full(m_ref.shape, NEG_INF, F32)
    acc_ref[...] = jnp.zeros(acc_ref.shape, F32)
    scores(0, 2)
    scores(1, 3)
    softmax(2, 0)
    softmax(3, 1)
    scores(2, 0)
    scores(3, 1)

    def body(j, carry):
        i = 4 * j
        scores(i + 4, 2), scores(i + 5, 3)
        softmax(0, 2), softmax(1, 3)
        values(i, 0), values(i + 1, 1)
        scores(i + 6, 0), scores(i + 7, 1)
        softmax(2, 0), softmax(3, 1)
        values(i + 2, 2), values(i + 3, 3)
        return carry

    lax.fori_loop(0, n_blocks // 4 - 1, body, 0)
    n = n_blocks
    softmax(0, 2), softmax(1, 3)
    values(n - 4, 0), values(n - 3, 1), values(n - 2, 2), values(n - 1, 3)
    acc = acc_ref[...]
    return acc[0:64] / acc[64:65]


def _mla_kernel(k_ref, qn_ref, qp_ref, v_ref, o_ref, qpad_ref, *scratch, tk):
    slot = pl.program_id(1) % 2
    S = k_ref.shape[2]
    tq = qn_ref.shape[2]
    qn = qn_ref[0]
    zero = jnp.zeros_like(qn)
    qpad_ref[0:64, :] = jnp.where(slot == 0, qn, zero)
    qpad_ref[64:128, :] = jnp.where(slot == 1, qn, zero)
    qpad_ref[128:160, :] = qp_ref[0]
    qpad_ref[160:256, :] = jnp.zeros((96, tq), BF16)
    o_ref[0] = _flash_sweep(lambda i: k_ref[0, 0, pl.ds(pl.multiple_of(i * tk, tk), tk), :],
                            lambda i: v_ref[0, :, pl.ds(pl.multiple_of(i * tk, tk), tk)],
                            qpad_ref[...], S // tk, scratch)


def _mla_attention(k, qn, qp, v, tq=1024, tk=512):
    B, _, S, _ = k.shape
    tq, tk = min(tq, S), min(tk, S // 4)
    return pl.pallas_call(
        functools.partial(_mla_kernel, tk=tk), name="mla_attention", grid=(B, MLA_HEADS, S // tq),
        in_specs=[pl.BlockSpec((1, 1, S, 256), lambda b, h, i: (b, h // 2, 0, 0)),
                  pl.BlockSpec((1, 64, tq), lambda b, h, i: (b, h, i)),
                  pl.BlockSpec((1, 32, tq), lambda b, h, i: (b, h, i)),
                  pl.BlockSpec((1, V_ROWS, S), lambda b, h, i: (b, h, 0))],
        out_specs=pl.BlockSpec((1, 64, tq), lambda b, h, i: (b, h, i)),
        out_shape=jax.ShapeDtypeStruct((B, 256, S), F32),
        scratch_shapes=_flash_scratch(tk, tq),
        compiler_params=_params("parallel", "parallel", "arbitrary"))(k, qn, qp, v)


def _diff_kernel(lam_ref, g_ref, k_ref, q1_ref, q2_ref, v_ref, o_ref, qpad_ref, *scratch, tk, lambda_init):
    head = pl.program_id(1)
    S = k_ref.shape[1]
    maps = []
    for j, q_ref in enumerate((q1_ref, q2_ref)):
        q = q_ref[0]
        qpad_ref[...] = jnp.zeros(qpad_ref.shape, BF16)
        for s in range(DIFF_HEADS):
            r0 = j * 128 + s * DIFF_QK
            qpad_ref[r0:r0 + DIFF_QK, :] = jnp.where(head == s, q, jnp.zeros_like(q))
        maps.append(_flash_sweep(lambda i: k_ref[0, pl.ds(pl.multiple_of(i * tk, tk), tk), :],
                                 lambda i: v_ref[0, :, pl.ds(pl.multiple_of(i * tk, tk), tk)],
                                 qpad_ref[...], S // tk, scratch))
    lv = lam_ref[...]
    lam = (jnp.exp(jnp.sum(lv[0:1] * lv[1:2], axis=1, keepdims=True))
           - jnp.exp(jnp.sum(lv[2:3] * lv[3:4], axis=1, keepdims=True)) + lambda_init)
    o = maps[0] - lam * maps[1]
    o = o * lax.rsqrt(jnp.mean(o * o, axis=0, keepdims=True) + EPS)
    o_ref[0] = o * g_ref[...] * (1.0 - lambda_init)


def _diff_attention(k, q, v, lam_vecs, norm_g, lambda_init, tq=1024, tk=512):
    B, S, _ = k.shape
    tq, tk = min(tq, S), min(tk, S // 4)
    return pl.pallas_call(
        functools.partial(_diff_kernel, tk=tk, lambda_init=lambda_init), name="diff_attention",
        grid=(B, DIFF_HEADS, S // tq),
        in_specs=[pl.BlockSpec((4, DIFF_QK), lambda b, h, i: (0, 0)),
                  pl.BlockSpec((DIFF_V, 1), lambda b, h, i: (0, 0)),
                  pl.BlockSpec((1, S, 256), lambda b, h, i: (b, 0, 0)),
                  pl.BlockSpec((1, 32, tq), lambda b, h, i: (b, h, i)),
                  pl.BlockSpec((1, 32, tq), lambda b, h, i: (b, DIFF_HEADS + h, i)),
                  pl.BlockSpec((1, V_ROWS, S), lambda b, h, i: (b, h, 0))],
        out_specs=pl.BlockSpec((1, 64, tq), lambda b, h, i: (b, h, i)),
        out_shape=jax.ShapeDtypeStruct((B, 256, S), F32),
        scratch_shapes=_flash_scratch(tk, tq),
        compiler_params=_params("parallel", "parallel", "arbitrary"))(lam_vecs, norm_g.reshape(DIFF_V, 1), k, q, q, v)


def _swa_kernel(sink_ref, q_ref, k0, k1, k2, k3, v0, v1, v2, v3, o_ref, *, S, tq):
    head = pl.program_id(1)
    qi = pl.program_id(2)
    q = q_ref[0]
    zero = jnp.zeros_like(q)
    q_pad = jnp.concatenate([jnp.where(head // 2 == 0, q, zero), jnp.where(head // 2 == 1, q, zero)], axis=0)
    k = jnp.concatenate([k0[0], k1[0], k2[0], k3[0]], axis=0)
    v = jnp.concatenate([v0[0], v1[0], v2[0], v3[0]], axis=1)
    s = jnp.dot(k, q_pad, preferred_element_type=F32)
    kpos = qi * tq - SWA_WINDOW + lax.broadcasted_iota(jnp.int32, s.shape, 0)
    qpos = qi * tq + lax.broadcasted_iota(jnp.int32, s.shape, 1)
    valid = (jnp.abs(kpos - qpos) <= SWA_WINDOW) & (kpos >= 0) & (kpos < S)
    s = jnp.where(valid, s, NEG_INF)
    sink = sink_ref[head] * LOG2E
    m = jnp.maximum(jnp.max(s, axis=0, keepdims=True), sink)
    e = jnp.exp2(s - m)
    denom = jnp.sum(e, axis=0, keepdims=True) + jnp.exp2(sink - m)
    o_ref[0] = jnp.dot(v, e.astype(BF16), preferred_element_type=F32) / denom


def _swa_attention(k, q, v, sink):
    B, S, _ = k.shape
    W = SWA_WINDOW
    tq = 2 * W
    nb = S // W

    def kidx(d):
        return lambda b, h, i: (b, jnp.clip(i * 2 - 1 + d, 0, nb - 1), 0)

    def vidx(d):
        return lambda b, h, i: (b, h // 2, jnp.clip(i * 2 - 1 + d, 0, nb - 1))

    return pl.pallas_call(
        functools.partial(_swa_kernel, S=S, tq=tq), name="swa_attention", grid=(B, SWA_HEADS, S // tq),
        in_specs=[pl.BlockSpec(memory_space=pltpu.SMEM),
                  pl.BlockSpec((1, 64, tq), lambda b, h, i: (b, h, i))]
                 + [pl.BlockSpec((1, W, 128), kidx(d)) for d in range(4)]
                 + [pl.BlockSpec((1, 64, W), vidx(d)) for d in range(4)],
        out_specs=pl.BlockSpec((1, 64, tq), lambda b, h, i: (b, h, i)),
        out_shape=jax.ShapeDtypeStruct((B, 256, S), F32),
        compiler_params=_params("parallel", "parallel", "parallel"))(sink, q, k, k, k, k, v, v, v, v)


def _hyena_constants(S):
    N = 2 * S
    P2 = DFT_P2
    P1 = N // P2
    t = np.linspace(0.0, 1.0, S)
    w = 2.0 * math.pi * np.arange(S) / S
    f = np.linspace(1e-4, HY_BANDS - 1, HY_BANDS)
    z = np.concatenate([t[:, None], np.cos(w[:, None] * f), -np.sin(w[:, None] * f), np.zeros((S, 40 - HY_EMB))], axis=1)
    decay = np.linspace(HY_MIN_DECAY, HY_MAX_DECAY, HY_CH)
    window = np.exp(-t[:, None] * np.abs(decay))
    k1 = np.arange(P1)
    n1 = np.arange(P1 // 2)
    a1 = 2.0 * math.pi * np.outer(k1, n1) / P1
    m_fwd = np.concatenate([np.cos(a1), -np.sin(a1)], axis=0)
    m_inv = np.concatenate([np.cos(a1.T), -np.sin(a1.T)], axis=1)
    a2 = 2.0 * math.pi * np.outer(np.arange(P2), np.arange(P2)) / P2
    c2, s2 = np.cos(a2), np.sin(a2)
    g = np.block([[c2, s2], [-s2, c2]])
    at = 2.0 * math.pi * np.outer(k1, np.arange(P2)) / N
    return dict(z=jnp.asarray(z, F32), window=jnp.asarray(window, F32),
                m_fwd=jnp.asarray(m_fwd, BF16), m_inv=jnp.asarray(m_inv, BF16),
                g=jnp.asarray(g, BF16), gt=jnp.asarray(g.T, BF16),
                tw_r=jnp.asarray(np.cos(at)[:, :, None], F32), tw_i=jnp.asarray(-np.sin(at)[:, :, None], F32))


def _filter_kernel(z_ref, win_ref, w1_ref, b1_ref, w2_ref, b2_ref, w3_ref, b3_ref, w4_ref, f_ref, o_ref, st_ref):
    dot = functools.partial(jnp.dot, preferred_element_type=F32, precision=HIGHEST)
    fr = f_ref[0]
    h = jnp.sin(fr[0:1] * (dot(z_ref[...], w1_ref[0]) + b1_ref[0]))
    h = jnp.sin(fr[1:2] * (dot(h, w2_ref[0]) + b2_ref[0]))
    h = jnp.sin(fr[2:3] * (dot(h, w3_ref[0]) + b3_ref[0]))
    out = dot(h, w4_ref[0])
    win = win_ref[...]
    parts = [out[:, q * HY_CH:(q + 1) * HY_CH] * win for q in range(4)]
    for q in range(4):
        o_ref[0, q] = parts[q]
    filt = jnp.concatenate(parts, axis=1)
    ssq = jnp.sum(filt * filt, axis=0, keepdims=True)

    @pl.when(pl.program_id(1) == 0)
    def _():
        cross = filt[0:1, 0:512] * filt[0:1, 512:1024]
        st_ref[0] = jnp.concatenate([ssq, jnp.concatenate([cross, jnp.zeros_like(cross)], axis=1),
                                     jnp.zeros((6, 1024), F32)], axis=0)

    @pl.when(pl.program_id(1) != 0)
    def _():
        st_ref[0, 0:1] = st_ref[0, 0:1] + ssq


def _hyena_filters(consts, hy_w1, hy_b1, hy_w2, hy_b2, hy_w3, hy_b3, hy_w4, hy_sin_freq, S):
    L = hy_w1.shape[0]
    ts = min(512, S)
    w1 = jnp.concatenate([hy_w1, jnp.zeros((L, 40 - HY_EMB, HY_FFN), F32)], axis=1)
    lw = lambda a: pl.BlockSpec((1,) + a.shape[1:], lambda l, i: (l,) + (0,) * (a.ndim - 1))
    b1, b2, b3 = (b.reshape(L, 1, HY_FFN) for b in (hy_b1, hy_b2, hy_b3))
    return pl.pallas_call(
        _filter_kernel, name="hyena_filters", grid=(L, S // ts),
        in_specs=[pl.BlockSpec((ts, 40), lambda l, i: (i, 0)), pl.BlockSpec((ts, HY_CH), lambda l, i: (i, 0)),
                  lw(w1), lw(b1), lw(hy_w2), lw(b2), lw(hy_w3), lw(b3), lw(hy_w4), lw(hy_sin_freq)],
        out_specs=[pl.BlockSpec((1, 4, ts, HY_CH), lambda l, i: (l, 0, i, 0)),
                   pl.BlockSpec((1, 8, 1024), lambda l, i: (l, 0, 0))],
        out_shape=[jax.ShapeDtypeStruct((L, 4, S, HY_CH), F32), jax.ShapeDtypeStruct((L, 8, 1024), F32)],
        compiler_params=_params("parallel", "arbitrary"),
    )(consts["z"], consts["window"], w1, b1, hy_w2, b2, hy_w3, b3, hy_w4, hy_sin_freq)


def _dft_rows_kernel(m_ref, x_ref, o_ref):
    a = jnp.dot(m_ref[...], x_ref[0].astype(BF16), preferred_element_type=F32)
    o_ref[0] = a.reshape(o_ref.shape[1:])


def _dft_rows(consts, x, lanes_per_step=4096):
    nb, p1h, width = x.shape
    tl = min(lanes_per_step, width)
    return pl.pallas_call(
        _dft_rows_kernel, name="hyena_dft_rows", grid=(nb, width // tl),
        in_specs=[pl.BlockSpec((4 * p1h, p1h), lambda b, j: (0, 0)),
                  pl.BlockSpec((1, p1h, tl), lambda b, j: (b, 0, j))],
        out_specs=pl.BlockSpec((1, 2, 2 * p1h, tl), lambda b, j: (b, 0, 0, j)),
        out_shape=jax.ShapeDtypeStruct((nb, 2, 2 * p1h, width), F32),
        compiler_params=_params("parallel", "parallel"))(consts["m_fwd"], x)


def _twiddled_spectrum(a_ref, kk, tr, ti, g):
    ar, ai = a_ref[0, 0, kk], a_ref[0, 1, kk]
    pre = jnp.concatenate([ar * tr - ai * ti, ar * ti + ai * tr], axis=0).astype(BF16)
    return jnp.dot(g, pre, preferred_element_type=F32)


def _filter_spectrum_kernel(g_ref, twr_ref, twi_ref, st_ref, af_ref, ab_ref, h_ref, *, order, n_total):
    P2 = DFT_P2
    st = st_ref[0]
    c0 = order * HY_CH
    ssq = st[0:1, c0:c0 + HY_CH] + st[0:1, 512 + c0:512 + c0 + HY_CH] + 2.0 * st[1:2, c0:c0 + HY_CH]
    scale = lax.rsqrt(ssq + EPS) * (1.0 / n_total)
    g = g_ref[...]
    for kk in range(af_ref.shape[2]):
        tr, ti = twr_ref[kk], twi_ref[kk]
        xf = _twiddled_spectrum(af_ref, kk, tr, ti, g)
        xb = _twiddled_spectrum(ab_ref, kk, tr, ti, g)
        h_ref[0, 0, kk] = (xf[:P2] + xb[:P2]) * scale
        h_ref[0, 1, kk] = (xf[P2:] - xb[P2:]) * scale


def _filter_spectra(consts, filters, stats, S, kb=8):
    L = filters.shape[0]
    P2, C = DFT_P2, HY_CH
    P1 = 2 * S // P2
    kb = min(kb, P1)
    a = _dft_rows(consts, filters.reshape(L * 4, P1 // 2, P2 * C)).reshape(L * 4, 2, P1, P2, C)
    outs = []
    for order in range(HY_ORDER):
        outs.append(pl.pallas_call(
            functools.partial(_filter_spectrum_kernel, order=order, n_total=2 * S),
            name=f"hyena_filter_spectrum_{order}", grid=(L, P1 // kb),
            in_specs=[pl.BlockSpec((2 * P2, 2 * P2), lambda l, i: (0, 0)),
                      pl.BlockSpec((kb, P2, 1), lambda l, i: (i, 0, 0)),
                      pl.BlockSpec((kb, P2, 1), lambda l, i: (i, 0, 0)),
                      pl.BlockSpec((1, 8, 1024), lambda l, i: (l, 0, 0)),
                      pl.BlockSpec((1, 2, kb, P2, C), lambda l, i: (l * 4 + order, 0, i, 0, 0)),
                      pl.BlockSpec((1, 2, kb, P2, C), lambda l, i: (l * 4 + 2 + order, 0, i, 0, 0))],
            out_specs=pl.BlockSpec((1, 2, kb, P2, C), lambda l, i: (l, 0, i, 0, 0)),
            out_shape=jax.ShapeDtypeStruct((L, 2, P1, P2, C), F32),
            compiler_params=_params("parallel", "parallel"),
        )(consts["g"], consts["tw_r"], consts["tw_i"], stats, a, a))
    return outs


def _spectral_filter_kernel(g_ref, gt_ref, twr_ref, twi_ref, h_ref, a_ref, o_ref):
    P2 = DFT_P2
    g, gt = g_ref[...], gt_ref[...]
    for kk in range(a_ref.shape[2]):
        tr, ti = twr_ref[kk], twi_ref[kk]
        x = _twiddled_spectrum(a_ref, kk, tr, ti, g)
        xr, xi = x[:P2], x[P2:]
        hr, hi = h_ref[0, 0, kk], h_ref[0, 1, kk]
        y = jnp.concatenate([xr * hr - xi * hi, xr * hi + xi * hr], axis=0).astype(BF16)
        bm = jnp.dot(gt, y, preferred_element_type=F32)
        br, bi = bm[:P2], bm[P2:]
        o_ref[0, 0, kk] = br * tr + bi * ti
        o_ref[0, 1, kk] = bi * tr - br * ti


def _spectral_filter(consts, a, h, layer, kb=8):
    B, _, P1, P2, C = a.shape
    kb = min(kb, P1)
    blk = lambda f: pl.BlockSpec((1, 2, kb, P2, C), f)
    return pl.pallas_call(
        _spectral_filter_kernel, name="hyena_spectral_filter", grid=(B, P1 // kb),
        in_specs=[pl.BlockSpec((2 * P2, 2 * P2), lambda b, i: (0, 0)),
                  pl.BlockSpec((2 * P2, 2 * P2), lambda b, i: (0, 0)),
                  pl.BlockSpec((kb, P2, 1), lambda b, i: (i, 0, 0)),
                  pl.BlockSpec((kb, P2, 1), lambda b, i: (i, 0, 0)),
                  blk(lambda b, i: (layer, 0, i, 0, 0)), blk(lambda b, i: (b, 0, i, 0, 0))],
        out_specs=blk(lambda b, i: (b, 0, i, 0, 0)),
        out_shape=jax.ShapeDtypeStruct(a.shape, F32),
        compiler_params=_params("parallel", "parallel"),
    )(consts["g"], consts["gt"], consts["tw_r"], consts["tw_i"], h, a)


def _idft_rows_kernel(m_ref, b_ref, u_ref, gate_ref, bias_ref, o_ref):
    bm = b_ref[0]
    y = jnp.dot(m_ref[...], bm.reshape(bm.shape[0] * bm.shape[1], bm.shape[2]).astype(BF16), preferred_element_type=F32)
    o_ref[0] = gate_ref[0] * (y + bias_ref[...] * u_ref[0])


def _idft_rows_gated(consts, bm, u, gate, bias, lanes_per_step=4096):
    B, p1h, width = u.shape
    tl = min(lanes_per_step, width)
    row = lambda: pl.BlockSpec((1, p1h, tl), lambda b, j: (b, 0, j))
    return pl.pallas_call(
        _idft_rows_kernel, name="hyena_idft_rows", grid=(B, width // tl),
        in_specs=[pl.BlockSpec((p1h, 4 * p1h), lambda b, j: (0, 0)),
                  pl.BlockSpec((1, 2, 2 * p1h, tl), lambda b, j: (b, 0, 0, j)),
                  row(), row(), pl.BlockSpec((1, tl), lambda b, j: (0, 0))],
        out_specs=row(), out_shape=jax.ShapeDtypeStruct(u.shape, F32),
        compiler_params=_params("parallel", "parallel"),
    )(consts["m_inv"], bm, u, gate, jnp.tile(bias.reshape(1, HY_CH), (1, tl // HY_CH)))


def _short_conv_kernel(x_ref, p_ref, n_ref, w_ref, b_ref, v_ref, x1_ref, x2_ref):
    i = pl.program_id(1)
    x = x_ref[0]
    t = x.shape[0]
    row = lax.broadcasted_iota(jnp.int32, x.shape, 0)
    prev_row = jnp.where(i == 0, 0.0, p_ref[0, 7:8, :])
    next_row = jnp.where(i == pl.num_programs(1) - 1, 0.0, n_ref[0, 0:1, :])
    before = jnp.where(row == 0, prev_row, pltpu.roll(x, 1, axis=0))
    after = jnp.where(row == t - 1, next_row, pltpu.roll(x, t - 1, axis=0))
    w = w_ref[...]
    u = before * w[0:1] + x * w[1:2] + after * w[2:3] + b_ref[...]
    v_ref[0] = u[:, 0:HY_CH]
    x1_ref[0] = u[:, HY_CH:2 * HY_CH]
    x2_ref[0] = u[:, 2 * HY_CH:3 * HY_CH]


def _short_conv(cols, conv_w, conv_b, t=512):
    B, S, W = cols.shape
    t = min(t, S)
    r = t // 8
    out = pl.BlockSpec((1, t, HY_CH), lambda b, i: (b, i, 0))
    return pl.pallas_call(
        _short_conv_kernel, name="hyena_short_conv", grid=(B, S // t),
        in_specs=[pl.BlockSpec((1, t, W), lambda b, i: (b, i, 0)),
                  pl.BlockSpec((1, 8, W), lambda b, i: (b, jnp.maximum(i * r - 1, 0), 0)),
                  pl.BlockSpec((1, 8, W), lambda b, i: (b, jnp.minimum((i + 1) * r, S // 8 - 1), 0)),
                  pl.BlockSpec((HY_SHORT, W), lambda b, i: (0, 0)), pl.BlockSpec((1, W), lambda b, i: (0, 0))],
        out_specs=[out, out, out], out_shape=[jax.ShapeDtypeStruct((B, S, HY_CH), F32)] * 3,
        compiler_params=_params("parallel", "parallel"))(cols, cols, cols, conv_w, conv_b.reshape(1, W))


def _hyena_mixer(consts, cols, conv_w, conv_b, spectra, bias, layer):
    B, S, _ = cols.shape
    P2, C = DFT_P2, HY_CH
    P1 = 2 * S // P2
    v, x1, x2 = _short_conv(cols, conv_w, conv_b)
    rows = lambda a: a.reshape(B, P1 // 2, P2 * C)
    z = rows(v)
    for order, gate in enumerate((x1, x2)):
        a = _dft_rows(consts, z).reshape(B, 2, P1, P2, C)
        bm = _spectral_filter(consts, a, spectra[order], layer).reshape(B, 2, P1, P2 * C)
        z = _idft_rows_gated(consts, bm, z, rows(gate), bias[order])
    return z.reshape(B, S, C)


def _route(lt):
    t = lt.shape[1]
    row = lax.broadcasted_iota(jnp.int32, (8, t), 0)
    lg = jnp.where(row < N_GROUPS, lt[0:8], NEG_INF)
    mg = jnp.max(lg, axis=0, keepdims=True)
    p_top = 1.0 / jnp.sum(jnp.exp(lg - mg), axis=0, keepdims=True)
    g_sel = jnp.min(jnp.where(lg == mg, row, 8), axis=0, keepdims=True)
    le = jnp.zeros((8, t), F32)
    for g in range(N_GROUPS):
        le = jnp.where(g_sel == g, lt[8 + 8 * g:16 + 8 * g], le)
    v1 = jnp.max(le, axis=0, keepdims=True)
    i1 = jnp.min(jnp.where(le == v1, row, 8), axis=0, keepdims=True)
    rest = jnp.where(row == i1, NEG_INF, le)
    v2 = jnp.max(rest, axis=0, keepdims=True)
    i2 = jnp.min(jnp.where(rest == v2, row, 8), axis=0, keepdims=True)
    e = jnp.exp(v2 - v1)
    w1 = 1.0 / (1.0 + e)
    base = g_sel * EXPERTS_PER_GROUP
    idx = jnp.where(row == 0, base + i1, jnp.where(row == 1, base + i2, 0))
    gate = jnp.where(row == 0, p_top * w1, jnp.where(row == 1, p_top * (e * w1), 0.0))
    return idx, gate


def _post_kernel(x_ref, g1_ref, sc_ref, sh_ref, ln_ref, a_ref, b_ref, c_ref, d_ref, wo_ref, wr_ref, br_ref,
                 xo_ref, h_ref, idx_ref, gate_ref):
    wo = wo_ref[0]
    y = (_tn(a_ref[0].astype(BF16), wo[0:256]) + _tn(b_ref[0].astype(BF16), wo[256:512])
         + _tn(c_ref[0].astype(BF16), wo[512:768])
         + jnp.dot(d_ref[0].astype(BF16), wo[768:1024], preferred_element_type=F32))
    x = x_ref[0] + g1_ref[0] * y
    xo_ref[0] = x
    h = _rms(x) * ln_ref[...] * (1.0 + sc_ref[0]) + sh_ref[0]
    h_ref[0] = h.astype(BF16)
    lt = lax.dot_general(wr_ref[0], h, (((1,), (1,)), ((), ())), preferred_element_type=F32, precision=HIGHEST) + br_ref[0]
    idx, gate = _route(lt)
    idx_ref[...] = idx
    gate_ref[...] = gate


def _post_mixer(x, g1, sc2, sh2, ln_g, layer, mix_a, mix_b, mix_c, mix_d, w_out, w_router, b_router, t=256):
    B, S, D = x.shape
    t = min(t, S)
    nt = S // t
    nat = lambda n: pl.BlockSpec((1, t, n), lambda b, i: (b, i, 0))
    fm = pl.BlockSpec((1, 256, t), lambda b, i: (b, 0, i))
    mod = pl.BlockSpec((1, 1, D), lambda b, i: (b, 0, 0))
    lw = lambda a: pl.BlockSpec((1,) + a.shape[1:], lambda b, i: (layer,) + (0,) * (a.ndim - 1))
    tok = pl.BlockSpec((8, t), lambda b, i: (0, b * nt + i))
    return pl.pallas_call(
        _post_kernel, name="mixer_output_router", grid=(B, nt),
        in_specs=[nat(D), mod, mod, mod, pl.BlockSpec((1, D), lambda b, i: (0, 0)), fm, fm, fm, nat(256),
                  lw(w_out), lw(w_router), lw(b_router)],
        out_specs=[nat(D), nat(D), tok, tok],
        out_shape=[jax.ShapeDtypeStruct((B, S, D), F32), jax.ShapeDtypeStruct((B, S, D), BF16),
                   jax.ShapeDtypeStruct((8, B * S), jnp.int32), jax.ShapeDtypeStruct((8, B * S), F32)],
        compiler_params=_params("parallel", "parallel"),
    )(x, g1, sc2, sh2, ln_g[layer].reshape(1, D), mix_a, mix_b, mix_c, mix_d, w_out, w_router, b_router)


def _expert_kernel(be_ref, x_ref, wg_ref, wu_ref, wd_ref, o_ref):
    x = x_ref[...]
    g = jnp.dot(x, wg_ref[0, 0].astype(BF16), preferred_element_type=F32)
    u = jnp.dot(x, wu_ref[0, 0].astype(BF16), preferred_element_type=F32)
    a = (g * jax.nn.sigmoid(g) * u).astype(BF16)
    o_ref[...] = jnp.dot(a, wd_ref[0, 0].astype(BF16), preferred_element_type=F32)


def _expert_ffn(xg, block_expert, w_gate, w_up, w_down, layer):
    P, D = xg.shape
    n_blocks = P // MOE_ROWS
    wmap = lambda i, be: (layer, be[i], 0, 0)
    return pl.pallas_call(
        _expert_kernel, name="expert_ffn",
        grid_spec=pltpu.PrefetchScalarGridSpec(
            num_scalar_prefetch=1, grid=(n_blocks,),
            in_specs=[pl.BlockSpec((MOE_ROWS, D), lambda i, be: (i, 0)),
                      pl.BlockSpec((1, 1, D, D_EXPERT), wmap), pl.BlockSpec((1, 1, D, D_EXPERT), wmap),
                      pl.BlockSpec((1, 1, D_EXPERT, D), wmap)],
            out_specs=pl.BlockSpec((MOE_ROWS, D), lambda i, be: (i, 0))),
        out_shape=jax.ShapeDtypeStruct((P, D), F32),
        compiler_params=_params("arbitrary"))(block_expert, xg, w_gate, w_up, w_down)


def _dispatch_plan(idx, T):
    e = idx[0:TOP_K]
    onehot = (e.reshape(-1)[:, None] == jnp.arange(N_EXPERTS)[None]).astype(jnp.int32)
    before = jnp.cumsum(onehot, axis=0) - onehot
    rank = jnp.sum(before * onehot, axis=1)
    counts = jnp.sum(onehot, axis=0)
    padded = (counts + MOE_ROWS - 1) // MOE_ROWS * MOE_ROWS
    pad_end = jnp.cumsum(padded)
    dest = ((pad_end - padded)[e.reshape(-1)] + rank).reshape(TOP_K, T)
    n_blocks = (TOP_K * T + N_EXPERTS * (MOE_ROWS - 1) + MOE_ROWS - 1) // MOE_ROWS
    tok = jnp.tile(jnp.arange(T, dtype=jnp.int32), TOP_K)
    slot_tok = jnp.full((n_blocks * MOE_ROWS,), T, jnp.int32).at[dest.reshape(-1)].set(tok)
    starts = jnp.arange(n_blocks, dtype=jnp.int32) * MOE_ROWS
    block_expert = jnp.minimum(jnp.sum((pad_end[None, :] <= starts[:, None]).astype(jnp.int32), axis=1), N_EXPERTS - 1)
    return dest, slot_tok, block_expert.astype(jnp.int32)


def _combine_kernel(x_ref, g2_ref, y0_ref, y1_ref, gate_ref, fg_ref, o_ref, *, final):
    gt = gate_ref[...]
    x = x_ref[0] + g2_ref[0] * (gt[:, 0:1] * y0_ref[0] + gt[:, 1:2] * y1_ref[0])
    if final:
        x = _rms(x) * fg_ref[...]
    o_ref[0] = x


def _combine(x, g2, y0, y1, gate_t, final_g, final, t=512):
    B, S, D = x.shape
    t = min(t, S)
    nt = S // t
    nat = pl.BlockSpec((1, t, D), lambda b, i: (b, i, 0))
    return pl.pallas_call(
        functools.partial(_combine_kernel, final=final), name="moe_combine", grid=(B, nt),
        in_specs=[nat, pl.BlockSpec((1, 1, D), lambda b, i: (b, 0, 0)), nat, nat,
                  pl.BlockSpec((t, 8), lambda b, i: (b * nt + i, 0)), pl.BlockSpec((1, D), lambda b, i: (0, 0))],
        out_specs=nat, out_shape=jax.ShapeDtypeStruct((B, S, D), F32),
        compiler_params=_params("parallel", "parallel"))(x, g2, y0, y1, gate_t, final_g.reshape(1, D))


def _moe(x, h, idx, gate, g2, w_gate, w_up, w_down, layer, final_g, final):
    B, S, D = x.shape
    T = B * S
    dest, slot_tok, block_expert = _dispatch_plan(idx, T)
    h_pad = jnp.concatenate([h.reshape(T, D), jnp.zeros((1, D), h.dtype)], axis=0)
    yb = _expert_ffn(h_pad[slot_tok], block_expert, w_gate, w_up, w_down, layer)
    y0 = yb[dest[0]].reshape(B, S, D)
    y1 = yb[dest[1]].reshape(B, S, D)
    return _combine(x, g2, y0, y1, gate.T, final_g, final)


def kernel(x, c, positions, ln1_g, ln2_g, ada_w, ada_b, w_in, mla_q_norm, mla_kv_norm, mla_w_uq, mla_w_ukv, swa_sink, diff_lambda, diff_norm_g, hy_conv_w, hy_conv_b, hy_w1, hy_b1, hy_w2, hy_b2, hy_w3, hy_b3, hy_w4, hy_sin_freq, hy_bias, w_out, router_g_w, router_g_b, router_e_w, router_e_b, moe_w_gate, moe_w_up, moe_w_down, final_g):
    B, S, D = x.shape
    L = w_in.shape[0]
    mod = _ada_modulation(c, ada_w, ada_b)[:, :B].reshape(L, B, 6, 1, D)
    tab_fm = _rope_tables(positions)
    tab_nat = _natural_rope_table(tab_fm)
    weights = _layer_weight_layouts(w_in, mla_q_norm, mla_kv_norm, mla_w_uq, mla_w_ukv)
    w_out_b = w_out.astype(BF16)
    w_router = jnp.concatenate([router_g_w.transpose(0, 2, 1), jnp.zeros((L, 8 - N_GROUPS, D), F32),
                                router_e_w.transpose(0, 2, 1)], axis=1)
    b_router = jnp.concatenate([router_g_b, jnp.zeros((L, 8 - N_GROUPS), F32), router_e_b], axis=1)[:, :, None]
    consts = _hyena_constants(S)
    filters, stats = _hyena_filters(consts, hy_w1, hy_b1, hy_w2, hy_b2, hy_w3, hy_b3, hy_w4, hy_sin_freq, S)
    spectra = _filter_spectra(consts, filters, stats, S)
    for l in range(L):
        sh1, sc1, g1, sh2, sc2, g2 = (mod[l, :, i] for i in range(6))
        kmla, qn, qp, vmla, kswa, qswa, vswa, kdiff, qdiff, vdiff, hy = _project(
            x, sc1, sh1, ln1_g, l, weights, tab_nat, tab_fm)
        mix_a = _mla_attention(kmla, qn, qp, vmla)
        mix_b = _swa_attention(kswa, qswa, vswa, swa_sink[l])
        lambda_init = 0.8 - 0.6 * math.exp(-0.3 * l)
        mix_c = _diff_attention(kdiff, qdiff, vdiff, diff_lambda[l], diff_norm_g[l], lambda_init)
        mix_d = _hyena_mixer(consts, hy, hy_conv_w[l], hy_conv_b[l], spectra, hy_bias[l], l)
        x, h, idx, gate = _post_mixer(x, g1, sc2, sh2, ln2_g, l, mix_a, mix_b, mix_c, mix_d, w_out_b, w_router, b_router)
        x = _moe(x, h, idx, gate, g2, moe_w_gate, moe_w_up, moe_w_down, l, final_g, final=(l == L - 1))
    return x
```

```python
import functools
import math

import numpy as np
import jax
import jax.numpy as jnp
from jax import lax
from jax.experimental import pallas as pl
from jax.experimental.pallas import tpu as pltpu

F32 = jnp.float32
BF16 = jnp.bfloat16
HIGHEST = lax.Precision.HIGHEST

HEAD_DIM = 64
ROPE_THETA = 10000.0
EPS = 1e-6
NEG_INF = -1e30
LOG2E = 1.4426950408889634
MLA_HEADS, MLA_Q_RANK, MLA_KV_RANK, MLA_NOPE, MLA_ROPE, MLA_V = 4, 256, 128, 64, 32, 64
SWA_HEADS, SWA_KV_HEADS, SWA_WINDOW = 4, 2, 128
DIFF_HEADS, DIFF_QK, DIFF_V = 4, 32, 64
HY_CH, HY_ORDER, HY_DIRS, HY_SHORT, HY_EMB, HY_FFN = 256, 2, 2, 3, 33, 64
HY_BANDS = (HY_EMB - 1) // 2
HY_MIN_DECAY = math.log(1e-2) / 1.5
HY_MAX_DECAY = math.log(1e-2) / 0.3
MLA_COLS = MLA_Q_RANK + MLA_KV_RANK + MLA_ROPE
SWA_COLS = (SWA_HEADS + 2 * SWA_KV_HEADS) * HEAD_DIM
DIFF_COLS = DIFF_HEADS * (4 * DIFF_QK + DIFF_V)
HY_COLS = (HY_ORDER + 1) * HY_CH
N_GROUPS, EXPERTS_PER_GROUP, TOP_K, D_EXPERT = 4, 8, 2, 256
N_EXPERTS = N_GROUPS * EXPERTS_PER_GROUP

V7X_VMEM_LIMIT_BYTES = 56 * 1024 * 1024
LANES = 128
DFT_P2 = 128
MOE_ROWS = 512


def _params(*sem):
    return pltpu.CompilerParams(dimension_semantics=sem, vmem_limit_bytes=V7X_VMEM_LIMIT_BYTES)


def _nt(a, b):
    return lax.dot_general(a, b, (((1,), (1,)), ((), ())), preferred_element_type=F32)


def _tn(a, b):
    return lax.dot_general(a, b, (((0,), (0,)), ((), ())), preferred_element_type=F32)


def _ada_kernel(c_ref, w_ref, b_ref, o_ref):
    c = c_ref[...]
    act = c * jax.nn.sigmoid(c)
    o_ref[0] = jnp.dot(act, w_ref[0], preferred_element_type=F32, precision=HIGHEST) + b_ref[0]


def _ada_modulation(c, ada_w, ada_b):
    L, D, N = ada_w.shape
    B = c.shape[0]
    c8 = jnp.zeros((8, D), F32).at[:B].set(c)
    tn = 1024
    return pl.pallas_call(
        _ada_kernel, name="ada_modulation", grid=(L, N // tn),
        in_specs=[pl.BlockSpec((8, D), lambda l, j: (0, 0)),
                  pl.BlockSpec((1, D, tn), lambda l, j: (l, 0, j)),
                  pl.BlockSpec((1, 1, tn), lambda l, j: (l, 0, j))],
        out_specs=pl.BlockSpec((1, 8, tn), lambda l, j: (l, 0, j)),
        out_shape=jax.ShapeDtypeStruct((L, 8, N), F32),
        compiler_params=_params("parallel", "parallel"))(c8, ada_w, ada_b.reshape(L, 1, N))


def _rope_table_kernel(p_ref, o_ref):
    pos = p_ref[0].astype(F32)
    for half, base in ((32, 0), (16, 128)):
        j = lax.broadcasted_iota(jnp.int32, (half, 1), 0).astype(F32)
        inv = jnp.exp(-math.log(ROPE_THETA) * j / half)
        ang = inv * pos
        cs, sn = jnp.cos(ang), jnp.sin(ang)
        o_ref[0, base:base + half] = cs
        o_ref[0, base + half:base + 2 * half] = cs
        o_ref[0, base + 2 * half:base + 3 * half] = -sn
        o_ref[0, base + 3 * half:base + 4 * half] = sn


def _rope_tables(positions):
    B, S = positions.shape
    return pl.pallas_call(
        _rope_table_kernel, name="rope_tables", grid=(B,),
        in_specs=[pl.BlockSpec((1, 1, S), lambda b: (b, 0, 0))],
        out_specs=pl.BlockSpec((1, 192, S), lambda b: (b, 0, 0)),
        out_shape=jax.ShapeDtypeStruct((B, 192, S), F32),
        compiler_params=_params("parallel"))(positions.reshape(B, 1, S))


def _partner(n_heads, dim):
    r = np.arange(n_heads * dim)
    return (r // dim) * dim + (r % dim + dim // 2) % dim


def _layer_weight_layouts(w_in, mla_q_norm, mla_kv_norm, mla_w_uq, mla_w_ukv):
    L, D, _ = w_in.shape
    oB = MLA_COLS
    oC = oB + SWA_COLS
    oD = oC + DIFF_COLS
    Z = w_in.shape[2]
    wz = jnp.concatenate([w_in, jnp.zeros((L, D, 1), F32)], axis=2)

    kpe = MLA_Q_RANK + MLA_KV_RANK + np.arange(MLA_ROPE)
    pad96 = np.full(LANES - MLA_ROPE, Z)
    swa_k = oB + SWA_HEADS * HEAD_DIM + np.arange(SWA_KV_HEADS * HEAD_DIM)
    h, j, r = np.meshgrid(np.arange(DIFF_HEADS), np.arange(2), np.arange(DIFF_QK), indexing="ij")
    diff_src = (h * 2 * DIFF_QK + j * DIFF_QK + r).transpose(1, 0, 2).reshape(-1)
    diff_q = oC + diff_src
    diff_k = oC + DIFF_HEADS * 2 * DIFF_QK + diff_src
    nat_idx = np.concatenate([
        np.arange(MLA_Q_RANK + MLA_KV_RANK),
        kpe, pad96, kpe[_partner(1, MLA_ROPE)], pad96,
        swa_k, swa_k[_partner(SWA_KV_HEADS, HEAD_DIM)],
        diff_k, diff_k[_partner(2 * DIFF_HEADS, DIFF_QK)],
        oD + np.arange(HY_COLS)])
    w_nat = jnp.take(wz, jnp.asarray(nat_idx), axis=2).astype(BF16)

    swa_q = oB + np.arange(SWA_HEADS * HEAD_DIM)
    swa_v = oB + (SWA_HEADS + SWA_KV_HEADS) * HEAD_DIM + np.arange(SWA_KV_HEADS * HEAD_DIM)
    diff_v = oC + DIFF_HEADS * 4 * DIFF_QK + np.arange(DIFF_HEADS * DIFF_V)
    fm_idx = np.concatenate([
        swa_q, swa_q[_partner(SWA_HEADS, HEAD_DIM)], swa_v,
        diff_q, diff_q[_partner(2 * DIFF_HEADS, DIFF_QK)], diff_v])
    s_swa = HEAD_DIM ** -0.5 * LOG2E
    s_diff = DIFF_QK ** -0.5 * LOG2E
    fm_scale = np.concatenate([np.full(512, s_swa), np.ones(128), np.full(512, s_diff), np.ones(256)]).astype(np.float32)
    w_fm = (jnp.take(wz, jnp.asarray(fm_idx), axis=2) * fm_scale).transpose(0, 2, 1).astype(BF16)

    s_mla = (MLA_NOPE + MLA_ROPE) ** -0.5 * LOG2E
    uq = mla_w_uq * mla_q_norm[:, :, None] * s_mla
    hq = np.arange(MLA_HEADS)[:, None] * (MLA_NOPE + MLA_ROPE)
    qn_idx = (hq + np.arange(MLA_NOPE)[None]).reshape(-1)
    qp_idx = (hq + MLA_NOPE + np.arange(MLA_ROPE)[None]).reshape(-1)
    w_qn = jnp.take(uq, jnp.asarray(qn_idx), axis=2).transpose(0, 2, 1).astype(BF16)
    w_qpa = jnp.take(uq, jnp.asarray(qp_idx), axis=2).transpose(0, 2, 1).astype(BF16)
    w_qpb = jnp.take(uq, jnp.asarray(qp_idx[_partner(MLA_HEADS, MLA_ROPE)]), axis=2).transpose(0, 2, 1).astype(BF16)
    ukv = mla_w_ukv * mla_kv_norm[:, :, None]
    hk = np.arange(MLA_HEADS)[:, None] * (MLA_NOPE + MLA_V)
    kn_idx = (hk + np.arange(MLA_NOPE)[None]).reshape(-1)
    v_idx = (hk + MLA_NOPE + np.arange(MLA_V)[None]).reshape(-1)
    w_kn = jnp.take(ukv, jnp.asarray(kn_idx), axis=2).astype(BF16)
    w_v = jnp.take(ukv, jnp.asarray(v_idx), axis=2).transpose(0, 2, 1).astype(BF16)
    return w_nat, w_fm, w_qn, w_qpa, w_qpb, w_kn, w_v


def _natural_rope_table(tab_fm):
    t = tab_fm.transpose(0, 2, 1)
    cos64, sin64, cos32, sin32 = t[..., 0:64], t[..., 64:128], t[..., 128:160], t[..., 160:192]
    z = jnp.zeros(t.shape[:2] + (LANES - MLA_ROPE,), F32)
    return jnp.concatenate([cos32, z, sin32, z,
                            jnp.tile(cos64, (1, 1, 2)), jnp.tile(sin64, (1, 1, 2)),
                            jnp.tile(cos32, (1, 1, 8)), jnp.tile(sin32, (1, 1, 8))], axis=-1)


def _rms(x):
    return x * lax.rsqrt(jnp.mean(x * x, axis=-1, keepdims=True) + EPS)


def _proj_kernel(x_ref, sc_ref, sh_ref, g_ref, wn_ref, wf_ref, wqn_ref, wqpa_ref, wqpb_ref, wkn_ref, wv_ref,
                 tn_ref, tf_ref,
                 kmla_ref, qn_ref, qp_ref, vmla_ref, kswa_ref, qswa_ref, vswa_ref, kdiff_ref, qdiff_ref, vdiff_ref, hy_ref):
    x = x_ref[0]
    t = x.shape[0]
    h = _rms(x) * g_ref[...] * (1.0 + sc_ref[0]) + sh_ref[0]
    hb = h.astype(BF16)
    cn = jnp.dot(hb, wn_ref[0], preferred_element_type=F32)
    tab = tn_ref[0]
    cqn = _rms(cn[:, 0:256]).astype(BF16)
    ckvn = _rms(cn[:, 256:384]).astype(BF16)
    kpe = (cn[:, 384:512] * tab[:, 0:128] + cn[:, 512:640] * tab[:, 128:256]).astype(BF16)
    kswa_ref[0] = (cn[:, 640:768] * tab[:, 256:384] + cn[:, 768:896] * tab[:, 384:512]).astype(BF16)
    kdiff_ref[0] = (cn[:, 896:1152] * tab[:, 512:768] + cn[:, 1152:1408] * tab[:, 768:1024]).astype(BF16)
    hy_ref[0] = cn[:, 1408:2176]
    kn = jnp.dot(ckvn, wkn_ref[0], preferred_element_type=F32).astype(BF16)
    kmla_ref[0, 0] = jnp.concatenate([kn[:, 0:128], kpe], axis=1)
    kmla_ref[0, 1] = jnp.concatenate([kn[:, 128:256], kpe], axis=1)

    ft = _nt(wf_ref[0], hb)
    tf = tf_ref[0]
    cos64, sin64, cos32, sin32 = tf[0:64], tf[64:128], tf[128:160], tf[160:192]
    qswa = ft[0:256].reshape(4, 64, t) * cos64[None] + ft[256:512].reshape(4, 64, t) * sin64[None]
    qswa_ref[0] = qswa.reshape(256, t).astype(BF16)
    vswa_ref[0] = _with_ones_rows(ft[512:640], t)
    qdiff = ft[640:896].reshape(8, 32, t) * cos32[None] + ft[896:1152].reshape(8, 32, t) * sin32[None]
    qdiff_ref[0] = qdiff.reshape(256, t).astype(BF16)
    vdiff_ref[0] = _with_ones_rows(ft[1152:1408], t)
    qn_ref[0] = _nt(wqn_ref[0], cqn).astype(BF16)
    qp = _nt(wqpa_ref[0], cqn).reshape(4, 32, t) * cos32[None] + _nt(wqpb_ref[0], cqn).reshape(4, 32, t) * sin32[None]
    qp_ref[0] = qp.reshape(128, t).astype(BF16)
    vmla_ref[0] = _with_ones_rows(_nt(wv_ref[0], ckvn), t)


def _project(x, sc1, sh1, ln_g, layer, weights, tab_nat, tab_fm, t=256):
    B, S, D = x.shape
    w_nat, w_fm, w_qn, w_qpa, w_qpb, w_kn, w_v = weights
    wspec = lambda a: pl.BlockSpec((1,) + a.shape[1:], lambda b, i: (layer,) + (0,) * (a.ndim - 1))
    nat = lambda n: pl.BlockSpec((1, t, n), lambda b, i: (b, i, 0))
    fm = lambda n: pl.BlockSpec((1, n, t), lambda b, i: (b, 0, i))
    outs = [
        (jax.ShapeDtypeStruct((B, 2, S, 256), BF16), pl.BlockSpec((1, 2, t, 256), lambda b, i: (b, 0, i, 0))),
        (jax.ShapeDtypeStruct((B, 256, S), BF16), fm(256)),
        (jax.ShapeDtypeStruct((B, 128, S), BF16), fm(128)),
        (jax.ShapeDtypeStruct((B, 4 * V_ROWS, S), BF16), fm(4 * V_ROWS)),
        (jax.ShapeDtypeStruct((B, S, 128), BF16), nat(128)),
        (jax.ShapeDtypeStruct((B, 256, S), BF16), fm(256)),
        (jax.ShapeDtypeStruct((B, 2 * V_ROWS, S), BF16), fm(2 * V_ROWS)),
        (jax.ShapeDtypeStruct((B, S, 256), BF16), nat(256)),
        (jax.ShapeDtypeStruct((B, 256, S), BF16), fm(256)),
        (jax.ShapeDtypeStruct((B, 4 * V_ROWS, S), BF16), fm(4 * V_ROWS)),
        (jax.ShapeDtypeStruct((B, S, HY_COLS), F32), nat(HY_COLS)),
    ]
    return pl.pallas_call(
        _proj_kernel, name="mixer_projection", grid=(B, S // t),
        in_specs=[nat(D),
                  pl.BlockSpec((1, 1, D), lambda b, i: (b, 0, 0)),
                  pl.BlockSpec((1, 1, D), lambda b, i: (b, 0, 0)),
                  pl.BlockSpec((1, D), lambda b, i: (0, 0)),
                  wspec(w_nat), wspec(w_fm), wspec(w_qn), wspec(w_qpa), wspec(w_qpb), wspec(w_kn), wspec(w_v),
                  nat(1024), fm(192)],
        out_specs=[o[1] for o in outs], out_shape=[o[0] for o in outs],
        compiler_params=_params("parallel", "parallel"),
    )(x, sc1, sh1, ln_g[layer].reshape(1, D), w_nat, w_fm, w_qn, w_qpa, w_qpb, w_kn, w_v, tab_nat, tab_fm)


V_ROWS = 80


def _with_ones_rows(v, t):
    tail = (lax.broadcasted_iota(jnp.int32, (V_ROWS - 64, t), 0) == 0).astype(F32)
    parts = []
    for h in range(v.shape[0] // 64):
        parts += [v[h * 64:(h + 1) * 64], tail]
    return jnp.concatenate(parts, axis=0).astype(BF16)


def _flash_scratch(tk, tq):
    return ([pltpu.VMEM((256, tq), BF16)] + [pltpu.VMEM((tk, tq), F32)] * 4 + [pltpu.VMEM((tk, tq), BF16)] * 4
            + [pltpu.VMEM((1, tq), F32)] * 4 + [pltpu.VMEM((V_ROWS, tq), F32), pltpu.VMEM((1, tq), F32)])


def _flash_sweep(k_at, v_at, q_pad, n_blocks, scratch):
    s, p, al = scratch[0:4], scratch[4:8], scratch[8:12]
    acc_ref, m_ref = scratch[12], scratch[13]

    def scores(i, b):
        s[b][...] = jnp.dot(k_at(i), q_pad, preferred_element_type=F32)

    def softmax(src, dst):
        x = s[src][...]
        m_old = m_ref[...]
        m_new = jnp.maximum(m_old, jnp.max(x, axis=0, keepdims=True))
        al[dst][...] = jnp.exp2(m_old - m_new)
        p[dst][...] = jnp.exp2(x - m_new).astype(BF16)
        m_ref[...] = m_new

    def values(i, b):
        acc_ref[...] = al[b][...] * acc_ref[...] + jnp.dot(v_at(i), p[b][...], preferred_element_type=F32)

    m_ref[...] = jnp.full(m_ref.shape, NEG_INF, F32)
    acc_ref[...] = jnp.zeros(acc_ref.shape, F32)
    scores(0, 2)
    scores(1, 3)
    softmax(2, 0)
    softmax(3, 1)
    scores(2, 0)
    scores(3, 1)

    def body(j, carry):
        i = 4 * j
        scores(i + 4, 2), scores(i + 5, 3)
        softmax(0, 2), softmax(1, 3)
        values(i, 0), values(i + 1, 1)
        scores(i + 6, 0), scores(i + 7, 1)
        softmax(2, 0), softmax(3, 1)
        values(i + 2, 2), values(i + 3, 3)
        return carry

    lax.fori_loop(0, n_blocks // 4 - 1, body, 0)
    n = n_blocks
    softmax(0, 2), softmax(1, 3)
    values(n - 4, 0), values(n - 3, 1), values(n - 2, 2), values(n - 1, 3)
    acc = acc_ref[...]
    return acc[0:64] / acc[64:65]


def _mla_kernel(k_ref, qn_ref, qp_ref, v_ref, o_ref, qpad_ref, *scratch, tk):
    slot = pl.program_id(1) % 2
    S = k_ref.shape[2]
    tq = qn_ref.shape[2]
    qn = qn_ref[0]
    zero = jnp.zeros_like(qn)
    qpad_ref[0:64, :] = jnp.where(slot == 0, qn, zero)
    qpad_ref[64:128, :] = jnp.where(slot == 1, qn, zero)
    qpad_ref[128:160, :] = qp_ref[0]
    qpad_ref[160:256, :] = jnp.zeros((96, tq), BF16)
    o_ref[0] = _flash_sweep(lambda i: k_ref[0, 0, pl.ds(pl.multiple_of(i * tk, tk), tk), :],
                            lambda i: v_ref[0, :, pl.ds(pl.multiple_of(i * tk, tk), tk)],
                            qpad_ref[...], S // tk, scratch)


def _mla_attention(k, qn, qp, v, tq=1024, tk=512):
    B, _, S, _ = k.shape
    tq, tk = min(tq, S), min(tk, S // 4)
    return pl.pallas_call(
        functools.partial(_mla_kernel, tk=tk), name="mla_attention", grid=(B, MLA_HEADS, S // tq),
        in_specs=[pl.BlockSpec((1, 1, S, 256), lambda b, h, i: (b, h // 2, 0, 0)),
                  pl.BlockSpec((1, 64, tq), lambda b, h, i: (b, h, i)),
                  pl.BlockSpec((1, 32, tq), lambda b, h, i: (b, h, i)),
                  pl.BlockSpec((1, V_ROWS, S), lambda b, h, i: (b, h, 0))],
        out_specs=pl.BlockSpec((1, 64, tq), lambda b, h, i: (b, h, i)),
        out_shape=jax.ShapeDtypeStruct((B, 256, S), F32),
        scratch_shapes=_flash_scratch(tk, tq),
        compiler_params=_params("parallel", "parallel", "arbitrary"))(k, qn, qp, v)


def _diff_kernel(lam_ref, g_ref, k_ref, q1_ref, q2_ref, v_ref, o_ref, qpad_ref, *scratch, tk, lambda_init):
    head = pl.program_id(1)
    S = k_ref.shape[1]
    maps = []
    for j, q_ref in enumerate((q1_ref, q2_ref)):
        q = q_ref[0]
        qpad_ref[...] = jnp.zeros(qpad_ref.shape, BF16)
        for s in range(DIFF_HEADS):
            r0 = j * 128 + s * DIFF_QK
            qpad_ref[r0:r0 + DIFF_QK, :] = jnp.where(head == s, q, jnp.zeros_like(q))
        maps.append(_flash_sweep(lambda i: k_ref[0, pl.ds(pl.multiple_of(i * tk, tk), tk), :],
                                 lambda i: v_ref[0, :, pl.ds(pl.multiple_of(i * tk, tk), tk)],
                                 qpad_ref[...], S // tk, scratch))
    lv = lam_ref[...]
    lam = (jnp.exp(jnp.sum(lv[0:1] * lv[1:2], axis=1, keepdims=True))
           - jnp.exp(jnp.sum(lv[2:3] * lv[3:4], axis=1, keepdims=True)) + lambda_init)
    o = maps[0] - lam * maps[1]
    o = o * lax.rsqrt(jnp.mean(o * o, axis=0, keepdims=True) + EPS)
    o_ref[0] = o * g_ref[...] * (1.0 - lambda_init)


def _diff_attention(k, q, v, lam_vecs, norm_g, lambda_init, tq=1024, tk=512):
    B, S, _ = k.shape
    tq, tk = min(tq, S), min(tk, S // 4)
    return pl.pallas_call(
        functools.partial(_diff_kernel, tk=tk, lambda_init=lambda_init), name="diff_attention",
        grid=(B, DIFF_HEADS, S // tq),
        in_specs=[pl.BlockSpec((4, DIFF_QK), lambda b, h, i: (0, 0)),
                  pl.BlockSpec((DIFF_V, 1), lambda b, h, i: (0, 0)),
                  pl.BlockSpec((1, S, 256), lambda b, h, i: (b, 0, 0)),
                  pl.BlockSpec((1, 32, tq), lambda b, h, i: (b, h, i)),
                  pl.BlockSpec((1, 32, tq), lambda b, h, i: (b, DIFF_HEADS + h, i)),
                  pl.BlockSpec((1, V_ROWS, S), lambda b, h, i: (b, h, 0))],
        out_specs=pl.BlockSpec((1, 64, tq), lambda b, h, i: (b, h, i)),
        out_shape=jax.ShapeDtypeStruct((B, 256, S), F32),
        scratch_shapes=_flash_scratch(tk, tq),
        compiler_params=_params("parallel", "parallel", "arbitrary"))(lam_vecs, norm_g.reshape(DIFF_V, 1), k, q, q, v)


def _swa_kernel(sink_ref, q_ref, k_prev, k_main, k_next, v_prev, v_main, v_next, o_ref, *, S):
    qi = pl.program_id(1)
    tq = q_ref.shape[2]
    k = jnp.concatenate([k_prev[0], k_main[0], k_next[0]], axis=0)
    v = jnp.concatenate([v_prev[0], v_main[0], v_next[0]], axis=1)
    shape = (tq + 2 * SWA_WINDOW, tq)
    kpos = qi * tq - SWA_WINDOW + lax.broadcasted_iota(jnp.int32, shape, 0)
    qpos = qi * tq + lax.broadcasted_iota(jnp.int32, shape, 1)
    valid = (jnp.abs(kpos - qpos) <= SWA_WINDOW) & (kpos >= 0) & (kpos < S)
    for head in range(SWA_HEADS):
        kv = head // (SWA_HEADS // SWA_KV_HEADS)
        q = q_ref[0, head * HEAD_DIM:(head + 1) * HEAD_DIM, :]
        zero = jnp.zeros_like(q)
        q_pad = jnp.concatenate([q, zero] if kv == 0 else [zero, q], axis=0)
        s = jnp.where(valid, jnp.dot(k, q_pad, preferred_element_type=F32), NEG_INF)
        sink = sink_ref[head] * LOG2E
        m = jnp.maximum(jnp.max(s, axis=0, keepdims=True), sink)
        e = jnp.exp2(s - m).astype(BF16)
        acc = jnp.dot(v[kv * V_ROWS:(kv + 1) * V_ROWS], e, preferred_element_type=F32)
        o_ref[0, head * HEAD_DIM:(head + 1) * HEAD_DIM, :] = acc[0:HEAD_DIM] / (acc[HEAD_DIM:HEAD_DIM + 1] + jnp.exp2(sink - m))


def _swa_attention(k, q, v, sink, tq=512):
    B, S, _ = k.shape
    W = SWA_WINDOW
    tq = min(tq, S)
    r = tq // W
    nb = S // W
    prev = lambda i: jnp.maximum(i * r - 1, 0)
    nxt = lambda i: jnp.minimum((i + 1) * r, nb - 1)
    return pl.pallas_call(
        functools.partial(_swa_kernel, S=S), name="swa_attention", grid=(B, S // tq),
        in_specs=[pl.BlockSpec(memory_space=pltpu.SMEM),
                  pl.BlockSpec((1, 256, tq), lambda b, i: (b, 0, i)),
                  pl.BlockSpec((1, W, 128), lambda b, i: (b, prev(i), 0)),
                  pl.BlockSpec((1, tq, 128), lambda b, i: (b, i, 0)),
                  pl.BlockSpec((1, W, 128), lambda b, i: (b, nxt(i), 0)),
                  pl.BlockSpec((1, 2 * V_ROWS, W), lambda b, i: (b, 0, prev(i))),
                  pl.BlockSpec((1, 2 * V_ROWS, tq), lambda b, i: (b, 0, i)),
                  pl.BlockSpec((1, 2 * V_ROWS, W), lambda b, i: (b, 0, nxt(i)))],
        out_specs=pl.BlockSpec((1, 256, tq), lambda b, i: (b, 0, i)),
        out_shape=jax.ShapeDtypeStruct((B, 256, S), F32),
        compiler_params=_params("parallel", "parallel"))(sink, q, k, k, k, v, v, v)


def _hyena_constants(S):
    N = 2 * S
    P2 = DFT_P2
    P1 = N // P2
    t = np.linspace(0.0, 1.0, S)
    w = 2.0 * math.pi * np.arange(S) / S
    f = np.linspace(1e-4, HY_BANDS - 1, HY_BANDS)
    z = np.concatenate([t[:, None], np.cos(w[:, None] * f), -np.sin(w[:, None] * f), np.zeros((S, 40 - HY_EMB))], axis=1)
    decay = np.linspace(HY_MIN_DECAY, HY_MAX_DECAY, HY_CH)
    window = np.exp(-t[:, None] * np.abs(decay))
    k1 = np.arange(P1)
    n1 = np.arange(P1 // 2)
    a1 = 2.0 * math.pi * np.outer(k1, n1) / P1
    m_fwd = np.concatenate([np.cos(a1), -np.sin(a1)], axis=0)
    m_inv = np.concatenate([np.cos(a1.T), -np.sin(a1.T)], axis=1)
    a2 = 2.0 * math.pi * np.outer(np.arange(P2), np.arange(P2)) / P2
    c2, s2 = np.cos(a2), np.sin(a2)
    g = np.block([[c2, s2], [-s2, c2]])
    at = 2.0 * math.pi * np.outer(k1, np.arange(P2)) / N
    return dict(z=jnp.asarray(z, F32), window=jnp.asarray(window, F32),
                m_fwd=jnp.asarray(m_fwd, BF16), m_inv=jnp.asarray(m_inv, BF16),
                g=jnp.asarray(g, BF16), gt=jnp.asarray(g.T, BF16),
                tw_r=jnp.asarray(np.cos(at)[:, :, None], F32), tw_i=jnp.asarray(-np.sin(at)[:, :, None], F32))


def _filter_kernel(z_ref, win_ref, w1_ref, b1_ref, w2_ref, b2_ref, w3_ref, b3_ref, w4_ref, f_ref, o_ref, st_ref):
    dot = functools.partial(jnp.dot, preferred_element_type=F32, precision=HIGHEST)
    fr = f_ref[0]
    h = jnp.sin(fr[0:1] * (dot(z_ref[...], w1_ref[0]) + b1_ref[0]))
    h = jnp.sin(fr[1:2] * (dot(h, w2_ref[0]) + b2_ref[0]))
    h = jnp.sin(fr[2:3] * (dot(h, w3_ref[0]) + b3_ref[0]))
    out = dot(h, w4_ref[0])
    win = win_ref[...]
    parts = [out[:, q * HY_CH:(q + 1) * HY_CH] * win for q in range(4)]
    for q in range(4):
        o_ref[0, q] = parts[q]
    filt = jnp.concatenate(parts, axis=1)
    ssq = jnp.sum(filt * filt, axis=0, keepdims=True)

    @pl.when(pl.program_id(1) == 0)
    def _():
        cross = filt[0:1, 0:512] * filt[0:1, 512:1024]
        st_ref[0] = jnp.concatenate([ssq, jnp.concatenate([cross, jnp.zeros_like(cross)], axis=1),
                                     jnp.zeros((6, 1024), F32)], axis=0)

    @pl.when(pl.program_id(1) != 0)
    def _():
        st_ref[0, 0:1] = st_ref[0, 0:1] + ssq


def _hyena_filters(consts, hy_w1, hy_b1, hy_w2, hy_b2, hy_w3, hy_b3, hy_w4, hy_sin_freq, S):
    L = hy_w1.shape[0]
    ts = min(512, S)
    w1 = jnp.concatenate([hy_w1, jnp.zeros((L, 40 - HY_EMB, HY_FFN), F32)], axis=1)
    lw = lambda a: pl.BlockSpec((1,) + a.shape[1:], lambda l, i: (l,) + (0,) * (a.ndim - 1))
    b1, b2, b3 = (b.reshape(L, 1, HY_FFN) for b in (hy_b1, hy_b2, hy_b3))
    return pl.pallas_call(
        _filter_kernel, name="hyena_filters", grid=(L, S // ts),
        in_specs=[pl.BlockSpec((ts, 40), lambda l, i: (i, 0)), pl.BlockSpec((ts, HY_CH), lambda l, i: (i, 0)),
                  lw(w1), lw(b1), lw(hy_w2), lw(b2), lw(hy_w3), lw(b3), lw(hy_w4), lw(hy_sin_freq)],
        out_specs=[pl.BlockSpec((1, 4, ts, HY_CH), lambda l, i: (l, 0, i, 0)),
                   pl.BlockSpec((1, 8, 1024), lambda l, i: (l, 0, 0))],
        out_shape=[jax.ShapeDtypeStruct((L, 4, S, HY_CH), F32), jax.ShapeDtypeStruct((L, 8, 1024), F32)],
        compiler_params=_params("parallel", "arbitrary"),
    )(consts["z"], consts["window"], w1, b1, hy_w2, b2, hy_w3, b3, hy_w4, hy_sin_freq)


def _dft_rows_kernel(m_ref, x_ref, o_ref):
    a = jnp.dot(m_ref[...], x_ref[0].astype(BF16), preferred_element_type=F32)
    o_ref[0] = a.reshape(o_ref.shape[1:])


def _dft_rows(consts, x, lanes_per_step=4096):
    nb, p1h, width = x.shape
    tl = min(lanes_per_step, width)
    return pl.pallas_call(
        _dft_rows_kernel, name="hyena_dft_rows", grid=(nb, width // tl),
        in_specs=[pl.BlockSpec((4 * p1h, p1h), lambda b, j: (0, 0)),
                  pl.BlockSpec((1, p1h, tl), lambda b, j: (b, 0, j))],
        out_specs=pl.BlockSpec((1, 2, 2 * p1h, tl), lambda b, j: (b, 0, 0, j)),
        out_shape=jax.ShapeDtypeStruct((nb, 2, 2 * p1h, width), F32),
        compiler_params=_params("parallel", "parallel"))(consts["m_fwd"], x)


def _twiddled_spectrum(a_ref, kk, tr, ti, g):
    ar, ai = a_ref[0, 0, kk], a_ref[0, 1, kk]
    pre = jnp.concatenate([ar * tr - ai * ti, ar * ti + ai * tr], axis=0).astype(BF16)
    return jnp.dot(g, pre, preferred_element_type=F32)


def _filter_spectrum_kernel(g_ref, twr_ref, twi_ref, st_ref, af_ref, ab_ref, h_ref, *, order, n_total):
    P2 = DFT_P2
    st = st_ref[0]
    c0 = order * HY_CH
    ssq = st[0:1, c0:c0 + HY_CH] + st[0:1, 512 + c0:512 + c0 + HY_CH] + 2.0 * st[1:2, c0:c0 + HY_CH]
    scale = lax.rsqrt(ssq + EPS) * (1.0 / n_total)
    g = g_ref[...]
    for kk in range(af_ref.shape[2]):
        tr, ti = twr_ref[kk], twi_ref[kk]
        xf = _twiddled_spectrum(af_ref, kk, tr, ti, g)
        xb = _twiddled_spectrum(ab_ref, kk, tr, ti, g)
        h_ref[0, 0, kk] = (xf[:P2] + xb[:P2]) * scale
        h_ref[0, 1, kk] = (xf[P2:] - xb[P2:]) * scale


def _filter_spectra(consts, filters, stats, S, kb=8):
    L = filters.shape[0]
    P2, C = DFT_P2, HY_CH
    P1 = 2 * S // P2
    kb = min(kb, P1)
    a = _dft_rows(consts, filters.reshape(L * 4, P1 // 2, P2 * C)).reshape(L * 4, 2, P1, P2, C)
    outs = []
    for order in range(HY_ORDER):
        outs.append(pl.pallas_call(
            functools.partial(_filter_spectrum_kernel, order=order, n_total=2 * S),
            name=f"hyena_filter_spectrum_{order}", grid=(L, P1 // kb),
            in_specs=[pl.BlockSpec((2 * P2, 2 * P2), lambda l, i: (0, 0)),
                      pl.BlockSpec((kb, P2, 1), lambda l, i: (i, 0, 0)),
                      pl.BlockSpec((kb, P2, 1), lambda l, i: (i, 0, 0)),
                      pl.BlockSpec((1, 8, 1024), lambda l, i: (l, 0, 0)),
                      pl.BlockSpec((1, 2, kb, P2, C), lambda l, i: (l * 4 + order, 0, i, 0, 0)),
                      pl.BlockSpec((1, 2, kb, P2, C), lambda l, i: (l * 4 + 2 + order, 0, i, 0, 0))],
            out_specs=pl.BlockSpec((1, 2, kb, P2, C), lambda l, i: (l, 0, i, 0, 0)),
            out_shape=jax.ShapeDtypeStruct((L, 2, P1, P2, C), F32),
            compiler_params=_params("parallel", "parallel"),
        )(consts["g"], consts["tw_r"], consts["tw_i"], stats, a, a))
    return outs


def _spectral_filter_kernel(g_ref, gt_ref, twr_ref, twi_ref, h_ref, a_ref, o_ref):
    P2 = DFT_P2
    g, gt = g_ref[...], gt_ref[...]
    for kk in range(a_ref.shape[2]):
        tr, ti = twr_ref[kk], twi_ref[kk]
        x = _twiddled_spectrum(a_ref, kk, tr, ti, g)
        xr, xi = x[:P2], x[P2:]
        hr, hi = h_ref[0, 0, kk], h_ref[0, 1, kk]
        y = jnp.concatenate([xr * hr - xi * hi, xr * hi + xi * hr], axis=0).astype(BF16)
        bm = jnp.dot(gt, y, preferred_element_type=F32)
        br, bi = bm[:P2], bm[P2:]
        o_ref[0, 0, kk] = br * tr + bi * ti
        o_ref[0, 1, kk] = bi * tr - br * ti


def _spectral_filter(consts, a, h, layer, kb=8):
    B, _, P1, P2, C = a.shape
    kb = min(kb, P1)
    blk = lambda f: pl.BlockSpec((1, 2, kb, P2, C), f)
    return pl.pallas_call(
        _spectral_filter_kernel, name="hyena_spectral_filter", grid=(B, P1 // kb),
        in_specs=[pl.BlockSpec((2 * P2, 2 * P2), lambda b, i: (0, 0)),
                  pl.BlockSpec((2 * P2, 2 * P2), lambda b, i: (0, 0)),
                  pl.BlockSpec((kb, P2, 1), lambda b, i: (i, 0, 0)),
                  pl.BlockSpec((kb, P2, 1), lambda b, i: (i, 0, 0)),
                  blk(lambda b, i: (layer, 0, i, 0, 0)), blk(lambda b, i: (b, 0, i, 0, 0))],
        out_specs=blk(lambda b, i: (b, 0, i, 0, 0)),
        out_shape=jax.ShapeDtypeStruct(a.shape, F32),
        compiler_params=_params("parallel", "parallel"),
    )(consts["g"], consts["gt"], consts["tw_r"], consts["tw_i"], h, a)


def _idft_rows_kernel(m_ref, b_ref, u_ref, gate_ref, bias_ref, o_ref):
    bm = b_ref[0]
    y = jnp.dot(m_ref[...], bm.reshape(bm.shape[0] * bm.shape[1], bm.shape[2]).astype(BF16), preferred_element_type=F32)
    o_ref[0] = gate_ref[0] * (y + bias_ref[...] * u_ref[0])


def _idft_rows_gated(consts, bm, u, gate, bias, lanes_per_step=4096):
    B, p1h, width = u.shape
    tl = min(lanes_per_step, width)
    row = lambda: pl.BlockSpec((1, p1h, tl), lambda b, j: (b, 0, j))
    return pl.pallas_call(
        _idft_rows_kernel, name="hyena_idft_rows", grid=(B, width // tl),
        in_specs=[pl.BlockSpec((p1h, 4 * p1h), lambda b, j: (0, 0)),
                  pl.BlockSpec((1, 2, 2 * p1h, tl), lambda b, j: (b, 0, 0, j)),
                  row(), row(), pl.BlockSpec((1, tl), lambda b, j: (0, 0))],
        out_specs=row(), out_shape=jax.ShapeDtypeStruct(u.shape, F32),
        compiler_params=_params("parallel", "parallel"),
    )(consts["m_inv"], bm, u, gate, jnp.tile(bias.reshape(1, HY_CH), (1, tl // HY_CH)))


def _short_conv_kernel(x_ref, p_ref, n_ref, w_ref, b_ref, v_ref, x1_ref, x2_ref):
    i = pl.program_id(1)
    x = x_ref[0]
    t = x.shape[0]
    row = lax.broadcasted_iota(jnp.int32, x.shape, 0)
    prev_row = jnp.where(i == 0, 0.0, p_ref[0, 7:8, :])
    next_row = jnp.where(i == pl.num_programs(1) - 1, 0.0, n_ref[0, 0:1, :])
    before = jnp.where(row == 0, prev_row, pltpu.roll(x, 1, axis=0))
    after = jnp.where(row == t - 1, next_row, pltpu.roll(x, t - 1, axis=0))
    w = w_ref[...]
    u = before * w[0:1] + x * w[1:2] + after * w[2:3] + b_ref[...]
    v_ref[0] = u[:, 0:HY_CH]
    x1_ref[0] = u[:, HY_CH:2 * HY_CH]
    x2_ref[0] = u[:, 2 * HY_CH:3 * HY_CH]


def _short_conv(cols, conv_w, conv_b, t=512):
    B, S, W = cols.shape
    t = min(t, S)
    r = t // 8
    out = pl.BlockSpec((1, t, HY_CH), lambda b, i: (b, i, 0))
    return pl.pallas_call(
        _short_conv_kernel, name="hyena_short_conv", grid=(B, S // t),
        in_specs=[pl.BlockSpec((1, t, W), lambda b, i: (b, i, 0)),
                  pl.BlockSpec((1, 8, W), lambda b, i: (b, jnp.maximum(i * r - 1, 0), 0)),
                  pl.BlockSpec((1, 8, W), lambda b, i: (b, jnp.minimum((i + 1) * r, S // 8 - 1), 0)),
                  pl.BlockSpec((HY_SHORT, W), lambda b, i: (0, 0)), pl.BlockSpec((1, W), lambda b, i: (0, 0))],
        out_specs=[out, out, out], out_shape=[jax.ShapeDtypeStruct((B, S, HY_CH), F32)] * 3,
        compiler_params=_params("parallel", "parallel"))(cols, cols, cols, conv_w, conv_b.reshape(1, W))


def _hyena_mixer(consts, cols, conv_w, conv_b, spectra, bias, layer):
    B, S, _ = cols.shape
    P2, C = DFT_P2, HY_CH
    P1 = 2 * S // P2
    v, x1, x2 = _short_conv(cols, conv_w, conv_b)
    rows = lambda a: a.reshape(B, P1 // 2, P2 * C)
    z = rows(v)
    for order, gate in enumerate((x1, x2)):
        a = _dft_rows(consts, z).reshape(B, 2, P1, P2, C)
        bm = _spectral_filter(consts, a, spectra[order], layer).reshape(B, 2, P1, P2 * C)
        z = _idft_rows_gated(consts, bm, z, rows(gate), bias[order])
    return z.reshape(B, S, C)


def _route(lt):
    t = lt.shape[1]
    row = lax.broadcasted_iota(jnp.int32, (8, t), 0)
    lg = jnp.where(row < N_GROUPS, lt[0:8], NEG_INF)
    mg = jnp.max(lg, axis=0, keepdims=True)
    p_top = 1.0 / jnp.sum(jnp.exp(lg - mg), axis=0, keepdims=True)
    g_sel = jnp.min(jnp.where(lg == mg, row, 8), axis=0, keepdims=True)
    le = jnp.zeros((8, t), F32)
    for g in range(N_GROUPS):
        le = jnp.where(g_sel == g, lt[8 + 8 * g:16 + 8 * g], le)
    v1 = jnp.max(le, axis=0, keepdims=True)
    i1 = jnp.min(jnp.where(le == v1, row, 8), axis=0, keepdims=True)
    rest = jnp.where(row == i1, NEG_INF, le)
    v2 = jnp.max(rest, axis=0, keepdims=True)
    i2 = jnp.min(jnp.where(rest == v2, row, 8), axis=0, keepdims=True)
    e = jnp.exp(v2 - v1)
    w1 = 1.0 / (1.0 + e)
    base = g_sel * EXPERTS_PER_GROUP
    idx = jnp.where(row == 0, base + i1, jnp.where(row == 1, base + i2, 0))
    gate = jnp.where(row == 0, p_top * w1, jnp.where(row == 1, p_top * (e * w1), 0.0))
    return idx, gate


def _post_kernel(x_ref, g1_ref, sc_ref, sh_ref, ln_ref, a_ref, b_ref, c_ref, d_ref, wo_ref, wr_ref, br_ref,
                 xo_ref, h_ref, idx_ref, gate_ref):
    wo = wo_ref[0]
    y = (_tn(a_ref[0].astype(BF16), wo[0:256]) + _tn(b_ref[0].astype(BF16), wo[256:512])
         + _tn(c_ref[0].astype(BF16), wo[512:768])
         + jnp.dot(d_ref[0].astype(BF16), wo[768:1024], preferred_element_type=F32))
    x = x_ref[0] + g1_ref[0] * y
    xo_ref[0] = x
    h = _rms(x) * ln_ref[...] * (1.0 + sc_ref[0]) + sh_ref[0]
    h_ref[0] = h
    lt = lax.dot_general(wr_ref[0], h, (((1,), (1,)), ((), ())), preferred_element_type=F32, precision=HIGHEST) + br_ref[0]
    idx, gate = _route(lt)
    idx_ref[...] = idx
    gate_ref[...] = gate


def _post_mixer(x, g1, sc2, sh2, ln_g, layer, mix_a, mix_b, mix_c, mix_d, w_out, w_router, b_router, t=256):
    B, S, D = x.shape
    t = min(t, S)
    nt = S // t
    nat = lambda n: pl.BlockSpec((1, t, n), lambda b, i: (b, i, 0))
    fm = pl.BlockSpec((1, 256, t), lambda b, i: (b, 0, i))
    mod = pl.BlockSpec((1, 1, D), lambda b, i: (b, 0, 0))
    lw = lambda a: pl.BlockSpec((1,) + a.shape[1:], lambda b, i: (layer,) + (0,) * (a.ndim - 1))
    tok = pl.BlockSpec((8, t), lambda b, i: (0, b * nt + i))
    return pl.pallas_call(
        _post_kernel, name="mixer_output_router", grid=(B, nt),
        in_specs=[nat(D), mod, mod, mod, pl.BlockSpec((1, D), lambda b, i: (0, 0)), fm, fm, fm, nat(256),
                  lw(w_out), lw(w_router), lw(b_router)],
        out_specs=[nat(D), nat(D), tok, tok],
        out_shape=[jax.ShapeDtypeStruct((B, S, D), F32), jax.ShapeDtypeStruct((B, S, D), F32),
                   jax.ShapeDtypeStruct((8, B * S), jnp.int32), jax.ShapeDtypeStruct((8, B * S), F32)],
        compiler_params=_params("parallel", "parallel"),
    )(x, g1, sc2, sh2, ln_g[layer].reshape(1, D), mix_a, mix_b, mix_c, mix_d, w_out, w_router, b_router)


GATHER_UNROLL = 8


def _start_row_gather(src_hbm, dst_ref, sem, row_of):
    def issue(r, carry):
        pltpu.make_async_copy(src_hbm.at[pl.ds(row_of(r), 1)], dst_ref.at[pl.ds(r, 1)], sem).start()
        return carry

    lax.fori_loop(0, dst_ref.shape[0], issue, 0, unroll=GATHER_UNROLL)


def _wait_row_gather(src_hbm, dst_ref, sem):
    pltpu.make_async_copy(src_hbm.at[pl.ds(0, dst_ref.shape[0])], dst_ref, sem).wait()


def _expert_kernel(be_ref, tok_ref, h_hbm, wg_ref, wu_ref, wd_ref, o_ref, x_buf, sem):
    base = pl.program_id(0) * MOE_ROWS
    _start_row_gather(h_hbm, x_buf, sem, lambda r: tok_ref[base + r])
    _wait_row_gather(h_hbm, x_buf, sem)
    x = x_buf[...].astype(BF16)
    g = jnp.dot(x, wg_ref[0, 0].astype(BF16), preferred_element_type=F32)
    u = jnp.dot(x, wu_ref[0, 0].astype(BF16), preferred_element_type=F32)
    a = (g * jax.nn.sigmoid(g) * u).astype(BF16)
    o_ref[...] = jnp.dot(a, wd_ref[0, 0].astype(BF16), preferred_element_type=F32)


def _expert_ffn(h, slot_tok, block_expert, w_gate, w_up, w_down, layer):
    D = h.shape[1]
    P = slot_tok.shape[0]
    wmap = lambda i, be, tok: (layer, be[i], 0, 0)
    return pl.pallas_call(
        _expert_kernel, name="expert_ffn",
        grid_spec=pltpu.PrefetchScalarGridSpec(
            num_scalar_prefetch=2, grid=(P // MOE_ROWS,),
            in_specs=[pl.BlockSpec(memory_space=pl.ANY),
                      pl.BlockSpec((1, 1, D, D_EXPERT), wmap), pl.BlockSpec((1, 1, D, D_EXPERT), wmap),
                      pl.BlockSpec((1, 1, D_EXPERT, D), wmap)],
            out_specs=pl.BlockSpec((MOE_ROWS, D), lambda i, be, tok: (i, 0)),
            scratch_shapes=[pltpu.VMEM((MOE_ROWS, D), F32), pltpu.SemaphoreType.DMA(())]),
        out_shape=jax.ShapeDtypeStruct((P, D), F32),
        compiler_params=_params("arbitrary"))(block_expert, slot_tok, h, w_gate, w_up, w_down)


def _dispatch_plan(idx, T):
    e = idx[0:TOP_K]
    onehot = (e.reshape(-1)[:, None] == jnp.arange(N_EXPERTS)[None]).astype(jnp.int32)
    before = jnp.cumsum(onehot, axis=0) - onehot
    rank = jnp.sum(before * onehot, axis=1)
    counts = jnp.sum(onehot, axis=0)
    padded = (counts + MOE_ROWS - 1) // MOE_ROWS * MOE_ROWS
    pad_end = jnp.cumsum(padded)
    dest = ((pad_end - padded)[e.reshape(-1)] + rank).reshape(TOP_K, T)
    n_blocks = (TOP_K * T + N_EXPERTS * (MOE_ROWS - 1) + MOE_ROWS - 1) // MOE_ROWS
    tok = jnp.tile(jnp.arange(T, dtype=jnp.int32), TOP_K)
    slot_tok = jnp.zeros((n_blocks * MOE_ROWS,), jnp.int32).at[dest.reshape(-1)].set(tok)
    starts = jnp.arange(n_blocks, dtype=jnp.int32) * MOE_ROWS
    block_expert = jnp.minimum(jnp.sum((pad_end[None, :] <= starts[:, None]).astype(jnp.int32), axis=1), N_EXPERTS - 1)
    return dest, slot_tok, block_expert.astype(jnp.int32)


def _combine_kernel(dest_ref, x_ref, g2_ref, gate_ref, fg_ref, y_hbm, o_ref, y_buf, sem, *, final, n_tokens):
    t = x_ref.shape[1]
    base = (pl.program_id(0) * pl.num_programs(1) + pl.program_id(1)) * t
    for k in range(TOP_K):
        _start_row_gather(y_hbm, y_buf.at[k], sem.at[k], lambda r, k=k: dest_ref[k * n_tokens + base + r])
    for k in range(TOP_K):
        _wait_row_gather(y_hbm, y_buf.at[k], sem.at[k])
    gt = gate_ref[...]
    x = x_ref[0] + g2_ref[0] * (gt[:, 0:1] * y_buf[0] + gt[:, 1:2] * y_buf[1])
    if final:
        x = _rms(x) * fg_ref[...]
    o_ref[0] = x


def _combine(x, g2, yb, dest, gate_t, final_g, final, t=256):
    B, S, D = x.shape
    t = min(t, S)
    nt = S // t
    nat = pl.BlockSpec((1, t, D), lambda b, i, d: (b, i, 0))
    return pl.pallas_call(
        functools.partial(_combine_kernel, final=final, n_tokens=B * S), name="moe_combine",
        grid_spec=pltpu.PrefetchScalarGridSpec(
            num_scalar_prefetch=1, grid=(B, nt),
            in_specs=[nat, pl.BlockSpec((1, 1, D), lambda b, i, d: (b, 0, 0)),
                      pl.BlockSpec((t, 8), lambda b, i, d: (b * nt + i, 0)),
                      pl.BlockSpec((1, D), lambda b, i, d: (0, 0)),
                      pl.BlockSpec(memory_space=pl.ANY)],
            out_specs=nat,
            scratch_shapes=[pltpu.VMEM((TOP_K, t, D), F32), pltpu.SemaphoreType.DMA((TOP_K,))]),
        out_shape=jax.ShapeDtypeStruct((B, S, D), F32),
        compiler_params=_params("arbitrary", "arbitrary"))(dest.reshape(-1), x, g2, gate_t, final_g.reshape(1, D), yb)


def _moe(x, h, idx, gate, g2, w_gate, w_up, w_down, layer, final_g, final):
    B, S, D = x.shape
    T = B * S
    dest, slot_tok, block_expert = _dispatch_plan(idx, T)
    yb = _expert_ffn(h.reshape(T, D), slot_tok, block_expert, w_gate, w_up, w_down, layer)
    return _combine(x, g2, yb, dest, gate.T, final_g, final)


def kernel(x, c, positions, ln1_g, ln2_g, ada_w, ada_b, w_in, mla_q_norm, mla_kv_norm, mla_w_uq, mla_w_ukv, swa_sink, diff_lambda, diff_norm_g, hy_conv_w, hy_conv_b, hy_w1, hy_b1, hy_w2, hy_b2, hy_w3, hy_b3, hy_w4, hy_sin_freq, hy_bias, w_out, router_g_w, router_g_b, router_e_w, router_e_b, moe_w_gate, moe_w_up, moe_w_down, final_g):
    B, S, D = x.shape
    L = w_in.shape[0]
    mod = _ada_modulation(c, ada_w, ada_b)[:, :B].reshape(L, B, 6, 1, D)
    tab_fm = _rope_tables(positions)
    tab_nat = _natural_rope_table(tab_fm)
    weights = _layer_weight_layouts(w_in, mla_q_norm, mla_kv_norm, mla_w_uq, mla_w_ukv)
    w_out_b = w_out.astype(BF16)
    w_router = jnp.concatenate([router_g_w.transpose(0, 2, 1), jnp.zeros((L, 8 - N_GROUPS, D), F32),
                                router_e_w.transpose(0, 2, 1)], axis=1)
    b_router = jnp.concatenate([router_g_b, jnp.zeros((L, 8 - N_GROUPS), F32), router_e_b], axis=1)[:, :, None]
    consts = _hyena_constants(S)
    filters, stats = _hyena_filters(consts, hy_w1, hy_b1, hy_w2, hy_b2, hy_w3, hy_b3, hy_w4, hy_sin_freq, S)
    spectra = _filter_spectra(consts, filters, stats, S)
    for l in range(L):
        sh1, sc1, g1, sh2, sc2, g2 = (mod[l, :, i] for i in range(6))
        kmla, qn, qp, vmla, kswa, qswa, vswa, kdiff, qdiff, vdiff, hy = _project(
            x, sc1, sh1, ln1_g, l, weights, tab_nat, tab_fm)
        mix_a = _mla_attention(kmla, qn, qp, vmla)
        mix_b = _swa_attention(kswa, qswa, vswa, swa_sink[l])
        lambda_init = 0.8 - 0.6 * math.exp(-0.3 * l)
        mix_c = _diff_attention(kdiff, qdiff, vdiff, diff_lambda[l], diff_norm_g[l], lambda_init)
        mix_d = _hyena_mixer(consts, hy, hy_conv_w[l], hy_conv_b[l], spectra, hy_bias[l], l)
        x, h, idx, gate = _post_mixer(x, g1, sc2, sh2, ln2_g, l, mix_a, mix_b, mix_c, mix_d, w_out_b, w_router, b_router)
        x = _moe(x, h, idx, gate, g2, moe_w_gate, moe_w_up, moe_w_down, l, final_g, final=(l == L - 1))
    return x
```

```python
import functools
import math

import numpy as np
import jax
import jax.numpy as jnp
from jax import lax
from jax.experimental import pallas as pl
from jax.experimental.pallas import tpu as pltpu

F32 = jnp.float32
BF16 = jnp.bfloat16
HIGHEST = lax.Precision.HIGHEST

HEAD_DIM = 64
ROPE_THETA = 10000.0
EPS = 1e-6
NEG_INF = -1e30
LOG2E = 1.4426950408889634
MLA_HEADS, MLA_Q_RANK, MLA_KV_RANK, MLA_NOPE, MLA_ROPE, MLA_V = 4, 256, 128, 64, 32, 64
SWA_HEADS, SWA_KV_HEADS, SWA_WINDOW = 4, 2, 128
DIFF_HEADS, DIFF_QK, DIFF_V = 4, 32, 64
HY_CH, HY_ORDER, HY_DIRS, HY_SHORT, HY_EMB, HY_FFN = 256, 2, 2, 3, 33, 64
HY_BANDS = (HY_EMB - 1) // 2
HY_MIN_DECAY = math.log(1e-2) / 1.5
HY_MAX_DECAY = math.log(1e-2) / 0.3
MLA_COLS = MLA_Q_RANK + MLA_KV_RANK + MLA_ROPE
SWA_COLS = (SWA_HEADS + 2 * SWA_KV_HEADS) * HEAD_DIM
DIFF_COLS = DIFF_HEADS * (4 * DIFF_QK + DIFF_V)
HY_COLS = (HY_ORDER + 1) * HY_CH
N_GROUPS, EXPERTS_PER_GROUP, TOP_K, D_EXPERT = 4, 8, 2, 256
N_EXPERTS = N_GROUPS * EXPERTS_PER_GROUP

V7X_VMEM_LIMIT_BYTES = 56 * 1024 * 1024
LANES = 128
DFT_P2 = 128
DFT_COLS_PER_STEP = 8
MOE_ROWS = 512


def _params(*sem):
    return pltpu.CompilerParams(dimension_semantics=sem, vmem_limit_bytes=V7X_VMEM_LIMIT_BYTES)


def _nt(a, b):
    return lax.dot_general(a, b, (((1,), (1,)), ((), ())), preferred_element_type=F32)


def _tn(a, b):
    return lax.dot_general(a, b, (((0,), (0,)), ((), ())), preferred_element_type=F32)


def _ada_kernel(c_ref, w_ref, b_ref, o_ref):
    c = c_ref[...]
    act = c * jax.nn.sigmoid(c)
    o_ref[0] = jnp.dot(act, w_ref[0], preferred_element_type=F32, precision=HIGHEST) + b_ref[0]


def _ada_modulation(c, ada_w, ada_b):
    L, D, N = ada_w.shape
    B = c.shape[0]
    c8 = jnp.zeros((8, D), F32).at[:B].set(c)
    tn = 1024
    return pl.pallas_call(
        _ada_kernel, name="ada_modulation", grid=(L, N // tn),
        in_specs=[pl.BlockSpec((8, D), lambda l, j: (0, 0)),
                  pl.BlockSpec((1, D, tn), lambda l, j: (l, 0, j)),
                  pl.BlockSpec((1, 1, tn), lambda l, j: (l, 0, j))],
        out_specs=pl.BlockSpec((1, 8, tn), lambda l, j: (l, 0, j)),
        out_shape=jax.ShapeDtypeStruct((L, 8, N), F32),
        compiler_params=_params("parallel", "parallel"))(c8, ada_w, ada_b.reshape(L, 1, N))


def _rope_table_kernel(p_ref, o_ref):
    pos = p_ref[0].astype(F32)
    for half, base in ((32, 0), (16, 128)):
        j = lax.broadcasted_iota(jnp.int32, (half, 1), 0).astype(F32)
        inv = jnp.exp(-math.log(ROPE_THETA) * j / half)
        ang = inv * pos
        cs, sn = jnp.cos(ang), jnp.sin(ang)
        o_ref[0, base:base + half] = cs
        o_ref[0, base + half:base + 2 * half] = cs
        o_ref[0, base + 2 * half:base + 3 * half] = -sn
        o_ref[0, base + 3 * half:base + 4 * half] = sn


def _rope_tables(positions):
    B, S = positions.shape
    return pl.pallas_call(
        _rope_table_kernel, name="rope_tables", grid=(B,),
        in_specs=[pl.BlockSpec((1, 1, S), lambda b: (b, 0, 0))],
        out_specs=pl.BlockSpec((1, 192, S), lambda b: (b, 0, 0)),
        out_shape=jax.ShapeDtypeStruct((B, 192, S), F32),
        compiler_params=_params("parallel"))(positions.reshape(B, 1, S))


def _partner(n_heads, dim):
    r = np.arange(n_heads * dim)
    return (r // dim) * dim + (r % dim + dim // 2) % dim


def _layer_weight_layouts(w_in, mla_q_norm, mla_kv_norm, mla_w_uq, mla_w_ukv):
    L, D, _ = w_in.shape
    oB = MLA_COLS
    oC = oB + SWA_COLS
    oD = oC + DIFF_COLS
    Z = w_in.shape[2]
    wz = jnp.concatenate([w_in, jnp.zeros((L, D, 1), F32)], axis=2)

    kpe = MLA_Q_RANK + MLA_KV_RANK + np.arange(MLA_ROPE)
    pad96 = np.full(LANES - MLA_ROPE, Z)
    swa_k = oB + SWA_HEADS * HEAD_DIM + np.arange(SWA_KV_HEADS * HEAD_DIM)
    h, j, r = np.meshgrid(np.arange(DIFF_HEADS), np.arange(2), np.arange(DIFF_QK), indexing="ij")
    diff_src = (h * 2 * DIFF_QK + j * DIFF_QK + r).transpose(1, 0, 2).reshape(-1)
    diff_q = oC + diff_src
    diff_k = oC + DIFF_HEADS * 2 * DIFF_QK + diff_src
    nat_idx = np.concatenate([
        np.arange(MLA_Q_RANK + MLA_KV_RANK),
        kpe, pad96, kpe[_partner(1, MLA_ROPE)], pad96,
        swa_k, swa_k[_partner(SWA_KV_HEADS, HEAD_DIM)],
        diff_k, diff_k[_partner(2 * DIFF_HEADS, DIFF_QK)],
        oD + np.arange(HY_COLS)])
    w_nat = jnp.take(wz, jnp.asarray(nat_idx), axis=2).astype(BF16)

    swa_q = oB + np.arange(SWA_HEADS * HEAD_DIM)
    swa_v = oB + (SWA_HEADS + SWA_KV_HEADS) * HEAD_DIM + np.arange(SWA_KV_HEADS * HEAD_DIM)
    diff_v = oC + DIFF_HEADS * 4 * DIFF_QK + np.arange(DIFF_HEADS * DIFF_V)
    fm_idx = np.concatenate([
        swa_q, swa_q[_partner(SWA_HEADS, HEAD_DIM)], swa_v,
        diff_q, diff_q[_partner(2 * DIFF_HEADS, DIFF_QK)], diff_v])
    s_swa = HEAD_DIM ** -0.5 * LOG2E
    s_diff = DIFF_QK ** -0.5 * LOG2E
    fm_scale = np.concatenate([np.full(512, s_swa), np.ones(128), np.full(512, s_diff), np.ones(256)]).astype(np.float32)
    w_fm = (jnp.take(wz, jnp.asarray(fm_idx), axis=2) * fm_scale).transpose(0, 2, 1).astype(BF16)

    s_mla = (MLA_NOPE + MLA_ROPE) ** -0.5 * LOG2E
    uq = mla_w_uq * mla_q_norm[:, :, None] * s_mla
    hq = np.arange(MLA_HEADS)[:, None] * (MLA_NOPE + MLA_ROPE)
    qn_idx = (hq + np.arange(MLA_NOPE)[None]).reshape(-1)
    qp_idx = (hq + MLA_NOPE + np.arange(MLA_ROPE)[None]).reshape(-1)
    w_qn = jnp.take(uq, jnp.asarray(qn_idx), axis=2).transpose(0, 2, 1).astype(BF16)
    w_qpa = jnp.take(uq, jnp.asarray(qp_idx), axis=2).transpose(0, 2, 1).astype(BF16)
    w_qpb = jnp.take(uq, jnp.asarray(qp_idx[_partner(MLA_HEADS, MLA_ROPE)]), axis=2).transpose(0, 2, 1).astype(BF16)
    ukv = mla_w_ukv * mla_kv_norm[:, :, None]
    hk = np.arange(MLA_HEADS)[:, None] * (MLA_NOPE + MLA_V)
    kn_idx = (hk + np.arange(MLA_NOPE)[None]).reshape(-1)
    v_idx = (hk + MLA_NOPE + np.arange(MLA_V)[None]).reshape(-1)
    w_kn = jnp.take(ukv, jnp.asarray(kn_idx), axis=2).astype(BF16)
    w_v = jnp.take(ukv, jnp.asarray(v_idx), axis=2).transpose(0, 2, 1).astype(BF16)
    return w_nat, w_fm, w_qn, w_qpa, w_qpb, w_kn, w_v


def _natural_rope_table(tab_fm):
    t = tab_fm.transpose(0, 2, 1)
    cos64, sin64, cos32, sin32 = t[..., 0:64], t[..., 64:128], t[..., 128:160], t[..., 160:192]
    z = jnp.zeros(t.shape[:2] + (LANES - MLA_ROPE,), F32)
    return jnp.concatenate([cos32, z, sin32, z,
                            jnp.tile(cos64, (1, 1, 2)), jnp.tile(sin64, (1, 1, 2)),
                            jnp.tile(cos32, (1, 1, 8)), jnp.tile(sin32, (1, 1, 8))], axis=-1)


def _rms(x):
    return x * lax.rsqrt(jnp.mean(x * x, axis=-1, keepdims=True) + EPS)


def _proj_kernel(x_ref, sc_ref, sh_ref, g_ref, wn_ref, wf_ref, wqn_ref, wqpa_ref, wqpb_ref, wkn_ref, wv_ref,
                 tn_ref, tf_ref,
                 kmla_ref, qn_ref, qp_ref, vmla_ref, kswa_ref, qswa_ref, vswa_ref, kdiff_ref, qdiff_ref, vdiff_ref, hy_ref):
    x = x_ref[0]
    t = x.shape[0]
    h = _rms(x) * g_ref[...] * (1.0 + sc_ref[0]) + sh_ref[0]
    hb = h.astype(BF16)
    cn = jnp.dot(hb, wn_ref[0], preferred_element_type=F32)
    tab = tn_ref[0]
    cqn = _rms(cn[:, 0:256]).astype(BF16)
    ckvn = _rms(cn[:, 256:384]).astype(BF16)
    kpe = (cn[:, 384:512] * tab[:, 0:128] + cn[:, 512:640] * tab[:, 128:256]).astype(BF16)
    kswa_ref[0] = (cn[:, 640:768] * tab[:, 256:384] + cn[:, 768:896] * tab[:, 384:512]).astype(BF16)
    kdiff_ref[0] = (cn[:, 896:1152] * tab[:, 512:768] + cn[:, 1152:1408] * tab[:, 768:1024]).astype(BF16)
    hy_ref[0] = cn[:, 1408:2176]
    kn = jnp.dot(ckvn, wkn_ref[0], preferred_element_type=F32).astype(BF16)
    kmla_ref[0, 0] = jnp.concatenate([kn[:, 0:128], kpe], axis=1)
    kmla_ref[0, 1] = jnp.concatenate([kn[:, 128:256], kpe], axis=1)

    ft = _nt(wf_ref[0], hb)
    tf = tf_ref[0]
    cos64, sin64, cos32, sin32 = tf[0:64], tf[64:128], tf[128:160], tf[160:192]
    qswa = ft[0:256].reshape(4, 64, t) * cos64[None] + ft[256:512].reshape(4, 64, t) * sin64[None]
    qswa_ref[0] = qswa.reshape(256, t).astype(BF16)
    vswa_ref[0] = _with_ones_rows(ft[512:640], t)
    qdiff = ft[640:896].reshape(8, 32, t) * cos32[None] + ft[896:1152].reshape(8, 32, t) * sin32[None]
    qdiff_ref[0] = qdiff.reshape(256, t).astype(BF16)
    vdiff_ref[0] = _with_ones_rows(ft[1152:1408], t)
    qn_ref[0] = _nt(wqn_ref[0], cqn).astype(BF16)
    qp = _nt(wqpa_ref[0], cqn).reshape(4, 32, t) * cos32[None] + _nt(wqpb_ref[0], cqn).reshape(4, 32, t) * sin32[None]
    qp_ref[0] = qp.reshape(128, t).astype(BF16)
    vmla_ref[0] = _with_ones_rows(_nt(wv_ref[0], ckvn), t)


def _project(x, sc1, sh1, ln_g, layer, weights, tab_nat, tab_fm, t=256):
    B, S, D = x.shape
    w_nat, w_fm, w_qn, w_qpa, w_qpb, w_kn, w_v = weights
    wspec = lambda a: pl.BlockSpec((1,) + a.shape[1:], lambda b, i: (layer,) + (0,) * (a.ndim - 1))
    nat = lambda n: pl.BlockSpec((1, t, n), lambda b, i: (b, i, 0))
    fm = lambda n: pl.BlockSpec((1, n, t), lambda b, i: (b, 0, i))
    outs = [
        (jax.ShapeDtypeStruct((B, 2, S, 256), BF16), pl.BlockSpec((1, 2, t, 256), lambda b, i: (b, 0, i, 0))),
        (jax.ShapeDtypeStruct((B, 256, S), BF16), fm(256)),
        (jax.ShapeDtypeStruct((B, 128, S), BF16), fm(128)),
        (jax.ShapeDtypeStruct((B, 4 * V_ROWS, S), BF16), fm(4 * V_ROWS)),
        (jax.ShapeDtypeStruct((B, S, 128), BF16), nat(128)),
        (jax.ShapeDtypeStruct((B, 256, S), BF16), fm(256)),
        (jax.ShapeDtypeStruct((B, 2 * V_ROWS, S), BF16), fm(2 * V_ROWS)),
        (jax.ShapeDtypeStruct((B, S, 256), BF16), nat(256)),
        (jax.ShapeDtypeStruct((B, 256, S), BF16), fm(256)),
        (jax.ShapeDtypeStruct((B, 4 * V_ROWS, S), BF16), fm(4 * V_ROWS)),
        (jax.ShapeDtypeStruct((B, S, HY_COLS), F32), nat(HY_COLS)),
    ]
    return pl.pallas_call(
        _proj_kernel, name="mixer_projection", grid=(B, S // t),
        in_specs=[nat(D),
                  pl.BlockSpec((1, 1, D), lambda b, i: (b, 0, 0)),
                  pl.BlockSpec((1, 1, D), lambda b, i: (b, 0, 0)),
                  pl.BlockSpec((1, D), lambda b, i: (0, 0)),
                  wspec(w_nat), wspec(w_fm), wspec(w_qn), wspec(w_qpa), wspec(w_qpb), wspec(w_kn), wspec(w_v),
                  nat(1024), fm(192)],
        out_specs=[o[1] for o in outs], out_shape=[o[0] for o in outs],
        compiler_params=_params("parallel", "parallel"),
    )(x, sc1, sh1, ln_g[layer].reshape(1, D), w_nat, w_fm, w_qn, w_qpa, w_qpb, w_kn, w_v, tab_nat, tab_fm)


V_ROWS = 80


def _with_ones_rows(v, t):
    tail = (lax.broadcasted_iota(jnp.int32, (V_ROWS - 64, t), 0) == 0).astype(F32)
    parts = []
    for h in range(v.shape[0] // 64):
        parts += [v[h * 64:(h + 1) * 64], tail]
    return jnp.concatenate(parts, axis=0).astype(BF16)


def _flash_scratch(tk, tq):
    return ([pltpu.VMEM((256, tq), BF16)] + [pltpu.VMEM((tk, tq), F32)] * 4 + [pltpu.VMEM((tk, tq), BF16)] * 4
            + [pltpu.VMEM((1, tq), F32)] * 4 + [pltpu.VMEM((V_ROWS, tq), F32), pltpu.VMEM((1, tq), F32)])


def _flash_sweep(k_at, v_at, q_pad, n_blocks, scratch):
    s, p, al = scratch[0:4], scratch[4:8], scratch[8:12]
    acc_ref, m_ref = scratch[12], scratch[13]

    def scores(i, b):
        s[b][...] = jnp.dot(k_at(i), q_pad, preferred_element_type=F32)

    def softmax(src, dst):
        x = s[src][...]
        m_old = m_ref[...]
        m_new = jnp.maximum(m_old, jnp.max(x, axis=0, keepdims=True))
        al[dst][...] = jnp.exp2(m_old - m_new)
        p[dst][...] = jnp.exp2(x - m_new).astype(BF16)
        m_ref[...] = m_new

    def values(i, b):
        acc_ref[...] = al[b][...] * acc_ref[...] + jnp.dot(v_at(i), p[b][...], preferred_element_type=F32)

    m_ref[...] = jnp.full(m_ref.shape, NEG_INF, F32)
    acc_ref[...] = jnp.zeros(acc_ref.shape, F32)
    scores(0, 2)
    scores(1, 3)
    softmax(2, 0)
    softmax(3, 1)
    scores(2, 0)
    scores(3, 1)

    def body(j, carry):
        i = 4 * j
        scores(i + 4, 2), scores(i + 5, 3)
        softmax(0, 2), softmax(1, 3)
        values(i, 0), values(i + 1, 1)
        scores(i + 6, 0), scores(i + 7, 1)
        softmax(2, 0), softmax(3, 1)
        values(i + 2, 2), values(i + 3, 3)
        return carry

    lax.fori_loop(0, n_blocks // 4 - 1, body, 0)
    n = n_blocks
    softmax(0, 2), softmax(1, 3)
    values(n - 4, 0), values(n - 3, 1), values(n - 2, 2), values(n - 1, 3)
    acc = acc_ref[...]
    return acc[0:64] / acc[64:65]


def _mla_kernel(k_ref, qn_ref, qp_ref, v_ref, o_ref, qpad_ref, *scratch, tk):
    slot = pl.program_id(1) % 2
    S = k_ref.shape[2]
    tq = qn_ref.shape[2]
    qn = qn_ref[0]
    zero = jnp.zeros_like(qn)
    qpad_ref[0:64, :] = jnp.where(slot == 0, qn, zero)
    qpad_ref[64:128, :] = jnp.where(slot == 1, qn, zero)
    qpad_ref[128:160, :] = qp_ref[0]
    qpad_ref[160:256, :] = jnp.zeros((96, tq), BF16)
    o_ref[0] = _flash_sweep(lambda i: k_ref[0, 0, pl.ds(pl.multiple_of(i * tk, tk), tk), :],
                            lambda i: v_ref[0, :, pl.ds(pl.multiple_of(i * tk, tk), tk)],
                            qpad_ref[...], S // tk, scratch)


def _mla_attention(k, qn, qp, v, tq=1024, tk=512):
    B, _, S, _ = k.shape
    tq, tk = min(tq, S), min(tk, S // 4)
    return pl.pallas_call(
        functools.partial(_mla_kernel, tk=tk), name="mla_attention", grid=(B, MLA_HEADS, S // tq),
        in_specs=[pl.BlockSpec((1, 1, S, 256), lambda b, h, i: (b, h // 2, 0, 0)),
                  pl.BlockSpec((1, 64, tq), lambda b, h, i: (b, h, i)),
                  pl.BlockSpec((1, 32, tq), lambda b, h, i: (b, h, i)),
                  pl.BlockSpec((1, V_ROWS, S), lambda b, h, i: (b, h, 0))],
        out_specs=pl.BlockSpec((1, 64, tq), lambda b, h, i: (b, h, i)),
        out_shape=jax.ShapeDtypeStruct((B, 256, S), F32),
        scratch_shapes=_flash_scratch(tk, tq),
        compiler_params=_params("parallel", "parallel", "arbitrary"))(k, qn, qp, v)


def _diff_kernel(lam_ref, g_ref, k_ref, q1_ref, q2_ref, v_ref, o_ref, qpad_ref, *scratch, tk, lambda_init):
    head = pl.program_id(1)
    S = k_ref.shape[1]
    maps = []
    for j, q_ref in enumerate((q1_ref, q2_ref)):
        q = q_ref[0]
        qpad_ref[...] = jnp.zeros(qpad_ref.shape, BF16)
        for s in range(DIFF_HEADS):
            r0 = j * 128 + s * DIFF_QK
            qpad_ref[r0:r0 + DIFF_QK, :] = jnp.where(head == s, q, jnp.zeros_like(q))
        maps.append(_flash_sweep(lambda i: k_ref[0, pl.ds(pl.multiple_of(i * tk, tk), tk), :],
                                 lambda i: v_ref[0, :, pl.ds(pl.multiple_of(i * tk, tk), tk)],
                                 qpad_ref[...], S // tk, scratch))
    lv = lam_ref[...]
    lam = (jnp.exp(jnp.sum(lv[0:1] * lv[1:2], axis=1, keepdims=True))
           - jnp.exp(jnp.sum(lv[2:3] * lv[3:4], axis=1, keepdims=True)) + lambda_init)
    o = maps[0] - lam * maps[1]
    o = o * lax.rsqrt(jnp.mean(o * o, axis=0, keepdims=True) + EPS)
    o_ref[0] = o * g_ref[...] * (1.0 - lambda_init)


def _diff_attention(k, q, v, lam_vecs, norm_g, lambda_init, tq=1024, tk=512):
    B, S, _ = k.shape
    tq, tk = min(tq, S), min(tk, S // 4)
    return pl.pallas_call(
        functools.partial(_diff_kernel, tk=tk, lambda_init=lambda_init), name="diff_attention",
        grid=(B, DIFF_HEADS, S // tq),
        in_specs=[pl.BlockSpec((4, DIFF_QK), lambda b, h, i: (0, 0)),
                  pl.BlockSpec((DIFF_V, 1), lambda b, h, i: (0, 0)),
                  pl.BlockSpec((1, S, 256), lambda b, h, i: (b, 0, 0)),
                  pl.BlockSpec((1, 32, tq), lambda b, h, i: (b, h, i)),
                  pl.BlockSpec((1, 32, tq), lambda b, h, i: (b, DIFF_HEADS + h, i)),
                  pl.BlockSpec((1, V_ROWS, S), lambda b, h, i: (b, h, 0))],
        out_specs=pl.BlockSpec((1, 64, tq), lambda b, h, i: (b, h, i)),
        out_shape=jax.ShapeDtypeStruct((B, 256, S), F32),
        scratch_shapes=_flash_scratch(tk, tq),
        compiler_params=_params("parallel", "parallel", "arbitrary"))(lam_vecs, norm_g.reshape(DIFF_V, 1), k, q, q, v)


def _swa_kernel(sink_ref, q_ref, k_prev, k_main, k_next, v_prev, v_main, v_next, o_ref, *, S):
    qi = pl.program_id(1)
    tq = q_ref.shape[2]
    k = jnp.concatenate([k_prev[0], k_main[0], k_next[0]], axis=0)
    v = jnp.concatenate([v_prev[0], v_main[0], v_next[0]], axis=1)
    shape = (tq + 2 * SWA_WINDOW, tq)
    kpos = qi * tq - SWA_WINDOW + lax.broadcasted_iota(jnp.int32, shape, 0)
    qpos = qi * tq + lax.broadcasted_iota(jnp.int32, shape, 1)
    valid = (jnp.abs(kpos - qpos) <= SWA_WINDOW) & (kpos >= 0) & (kpos < S)
    for head in range(SWA_HEADS):
        kv = head // (SWA_HEADS // SWA_KV_HEADS)
        q = q_ref[0, head * HEAD_DIM:(head + 1) * HEAD_DIM, :]
        zero = jnp.zeros_like(q)
        q_pad = jnp.concatenate([q, zero] if kv == 0 else [zero, q], axis=0)
        s = jnp.where(valid, jnp.dot(k, q_pad, preferred_element_type=F32), NEG_INF)
        sink = sink_ref[head] * LOG2E
        m = jnp.maximum(jnp.max(s, axis=0, keepdims=True), sink)
        e = jnp.exp2(s - m).astype(BF16)
        acc = jnp.dot(v[kv * V_ROWS:(kv + 1) * V_ROWS], e, preferred_element_type=F32)
        o_ref[0, head * HEAD_DIM:(head + 1) * HEAD_DIM, :] = acc[0:HEAD_DIM] / (acc[HEAD_DIM:HEAD_DIM + 1] + jnp.exp2(sink - m))


def _swa_attention(k, q, v, sink, tq=512):
    B, S, _ = k.shape
    W = SWA_WINDOW
    tq = min(tq, S)
    r = tq // W
    nb = S // W
    prev = lambda i: jnp.maximum(i * r - 1, 0)
    nxt = lambda i: jnp.minimum((i + 1) * r, nb - 1)
    return pl.pallas_call(
        functools.partial(_swa_kernel, S=S), name="swa_attention", grid=(B, S // tq),
        in_specs=[pl.BlockSpec(memory_space=pltpu.SMEM),
                  pl.BlockSpec((1, 256, tq), lambda b, i: (b, 0, i)),
                  pl.BlockSpec((1, W, 128), lambda b, i: (b, prev(i), 0)),
                  pl.BlockSpec((1, tq, 128), lambda b, i: (b, i, 0)),
                  pl.BlockSpec((1, W, 128), lambda b, i: (b, nxt(i), 0)),
                  pl.BlockSpec((1, 2 * V_ROWS, W), lambda b, i: (b, 0, prev(i))),
                  pl.BlockSpec((1, 2 * V_ROWS, tq), lambda b, i: (b, 0, i)),
                  pl.BlockSpec((1, 2 * V_ROWS, W), lambda b, i: (b, 0, nxt(i)))],
        out_specs=pl.BlockSpec((1, 256, tq), lambda b, i: (b, 0, i)),
        out_shape=jax.ShapeDtypeStruct((B, 256, S), F32),
        compiler_params=_params("parallel", "parallel"))(sink, q, k, k, k, v, v, v)


def _hyena_constants(S):
    N = 2 * S
    P2 = DFT_P2
    P1 = N // P2
    t = np.linspace(0.0, 1.0, S)
    w = 2.0 * math.pi * np.arange(S) / S
    f = np.linspace(1e-4, HY_BANDS - 1, HY_BANDS)
    z = np.concatenate([t[:, None], np.cos(w[:, None] * f), -np.sin(w[:, None] * f), np.zeros((S, 40 - HY_EMB))], axis=1)
    decay = np.linspace(HY_MIN_DECAY, HY_MAX_DECAY, HY_CH)
    window = np.exp(-t[:, None] * np.abs(decay))
    k1 = np.arange(P1)
    n1 = np.arange(P1 // 2)
    a1 = 2.0 * math.pi * np.outer(k1, n1) / P1
    m_fwd = np.concatenate([np.cos(a1), -np.sin(a1)], axis=0)
    m_inv = np.concatenate([np.cos(a1.T), -np.sin(a1.T)], axis=1)
    a2 = 2.0 * math.pi * np.outer(np.arange(P2), np.arange(P2)) / P2
    c2, s2 = np.cos(a2), np.sin(a2)
    g = np.block([[c2, s2], [-s2, c2]])
    at = 2.0 * math.pi * np.outer(k1, np.arange(P2)) / N
    eye = np.eye(DFT_COLS_PER_STEP)
    return dict(z=jnp.asarray(z, F32), window=jnp.asarray(window, F32),
                m_fwd=jnp.asarray(np.kron(m_fwd, eye), BF16), m_inv=jnp.asarray(np.kron(m_inv, eye), BF16),
                g=jnp.asarray(g, BF16), gt=jnp.asarray(g.T, BF16),
                tw_r=jnp.asarray(np.cos(at)[:, :, None], F32), tw_i=jnp.asarray(-np.sin(at)[:, :, None], F32))


def _filter_kernel(z_ref, win_ref, w1_ref, b1_ref, w2_ref, b2_ref, w3_ref, b3_ref, w4_ref, f_ref, o_ref, st_ref):
    dot = functools.partial(jnp.dot, preferred_element_type=F32, precision=HIGHEST)
    fr = f_ref[0]
    h = jnp.sin(fr[0:1] * (dot(z_ref[...], w1_ref[0]) + b1_ref[0]))
    h = jnp.sin(fr[1:2] * (dot(h, w2_ref[0]) + b2_ref[0]))
    h = jnp.sin(fr[2:3] * (dot(h, w3_ref[0]) + b3_ref[0]))
    out = dot(h, w4_ref[0])
    win = win_ref[...]
    parts = [out[:, q * HY_CH:(q + 1) * HY_CH] * win for q in range(4)]
    for q in range(4):
        o_ref[0, q] = parts[q]
    filt = jnp.concatenate(parts, axis=1)
    ssq = jnp.sum(filt * filt, axis=0, keepdims=True)

    @pl.when(pl.program_id(1) == 0)
    def _():
        cross = filt[0:1, 0:512] * filt[0:1, 512:1024]
        st_ref[0] = jnp.concatenate([ssq, jnp.concatenate([cross, jnp.zeros_like(cross)], axis=1),
                                     jnp.zeros((6, 1024), F32)], axis=0)

    @pl.when(pl.program_id(1) != 0)
    def _():
        st_ref[0, 0:1] = st_ref[0, 0:1] + ssq


def _hyena_filters(consts, hy_w1, hy_b1, hy_w2, hy_b2, hy_w3, hy_b3, hy_w4, hy_sin_freq, S):
    L = hy_w1.shape[0]
    ts = min(512, S)
    w1 = jnp.concatenate([hy_w1, jnp.zeros((L, 40 - HY_EMB, HY_FFN), F32)], axis=1)
    lw = lambda a: pl.BlockSpec((1,) + a.shape[1:], lambda l, i: (l,) + (0,) * (a.ndim - 1))
    b1, b2, b3 = (b.reshape(L, 1, HY_FFN) for b in (hy_b1, hy_b2, hy_b3))
    return pl.pallas_call(
        _filter_kernel, name="hyena_filters", grid=(L, S // ts),
        in_specs=[pl.BlockSpec((ts, 40), lambda l, i: (i, 0)), pl.BlockSpec((ts, HY_CH), lambda l, i: (i, 0)),
                  lw(w1), lw(b1), lw(hy_w2), lw(b2), lw(hy_w3), lw(b3), lw(hy_w4), lw(hy_sin_freq)],
        out_specs=[pl.BlockSpec((1, 4, ts, HY_CH), lambda l, i: (l, 0, i, 0)),
                   pl.BlockSpec((1, 8, 1024), lambda l, i: (l, 0, 0))],
        out_shape=[jax.ShapeDtypeStruct((L, 4, S, HY_CH), F32), jax.ShapeDtypeStruct((L, 8, 1024), F32)],
        compiler_params=_params("parallel", "arbitrary"),
    )(consts["z"], consts["window"], w1, b1, hy_w2, b2, hy_w3, b3, hy_w4, hy_sin_freq)


def _dft_rows_kernel(m_ref, x_ref, o_ref):
    x = x_ref[0]
    x = x.reshape(x.shape[0] * x.shape[1], x.shape[2]).astype(BF16)
    a = jnp.dot(m_ref[...], x, preferred_element_type=F32)
    o_ref[0] = a.reshape(o_ref.shape[1:])


def _dft_rows(consts, x):
    nb, p1h, p2, c = x.shape
    tj = DFT_COLS_PER_STEP
    return pl.pallas_call(
        _dft_rows_kernel, name="hyena_dft_rows", grid=(nb, p2 // tj),
        in_specs=[pl.BlockSpec((4 * p1h * tj, p1h * tj), lambda b, j: (0, 0)),
                  pl.BlockSpec((1, p1h, tj, c), lambda b, j: (b, 0, j, 0))],
        out_specs=pl.BlockSpec((1, 2, 2 * p1h, tj, c), lambda b, j: (b, 0, 0, j, 0)),
        out_shape=jax.ShapeDtypeStruct((nb, 2, 2 * p1h, p2, c), F32),
        compiler_params=_params("parallel", "parallel"))(consts["m_fwd"], x)


def _twiddled_spectrum(a_ref, kk, tr, ti, g):
    ar, ai = a_ref[0, 0, kk], a_ref[0, 1, kk]
    pre = jnp.concatenate([ar * tr - ai * ti, ar * ti + ai * tr], axis=0).astype(BF16)
    return jnp.dot(g, pre, preferred_element_type=F32)


def _filter_spectrum_kernel(g_ref, twr_ref, twi_ref, st_ref, af_ref, ab_ref, h_ref, *, order, n_total):
    P2 = DFT_P2
    kb = af_ref.shape[2]
    st = st_ref[0]
    c0 = order * HY_CH
    ssq = st[0:1, c0:c0 + HY_CH] + st[0:1, 512 + c0:512 + c0 + HY_CH] + 2.0 * st[1:2, c0:c0 + HY_CH]
    scale = lax.rsqrt(ssq + EPS) * (1.0 / n_total)
    g = g_ref[...]
    for kk in range(kb):
        tr, ti = twr_ref[kk], twi_ref[kk]
        xf = _twiddled_spectrum(af_ref, kk, tr, ti, g)
        xb = _twiddled_spectrum(ab_ref, kk, tr, ti, g)
        h_ref[0, 0, kk] = (xf[:P2] + xb[:P2]) * scale
        h_ref[0, 1, kk] = (xf[P2:] - xb[P2:]) * scale


def _filter_spectra(consts, filters, stats, S, kb=8):
    L = filters.shape[0]
    P2, C = DFT_P2, HY_CH
    P1 = 2 * S // P2
    kb = min(kb, P1)
    a = _dft_rows(consts, filters.reshape(L * 4, P1 // 2, P2, C))
    outs = []
    for order in range(HY_ORDER):
        outs.append(pl.pallas_call(
            functools.partial(_filter_spectrum_kernel, order=order, n_total=2 * S),
            name=f"hyena_filter_spectrum_{order}", grid=(L, P1 // kb),
            in_specs=[pl.BlockSpec((2 * P2, 2 * P2), lambda l, i: (0, 0)),
                      pl.BlockSpec((kb, P2, 1), lambda l, i: (i, 0, 0)),
                      pl.BlockSpec((kb, P2, 1), lambda l, i: (i, 0, 0)),
                      pl.BlockSpec((1, 8, 1024), lambda l, i: (l, 0, 0)),
                      pl.BlockSpec((1, 2, kb, P2, C), lambda l, i: (l * 4 + order, 0, i, 0, 0)),
                      pl.BlockSpec((1, 2, kb, P2, C), lambda l, i: (l * 4 + 2 + order, 0, i, 0, 0))],
            out_specs=pl.BlockSpec((1, 2, kb, P2, C), lambda l, i: (l, 0, i, 0, 0)),
            out_shape=jax.ShapeDtypeStruct((L, 2, P1, P2, C), F32),
            compiler_params=_params("parallel", "parallel"),
        )(consts["g"], consts["tw_r"], consts["tw_i"], stats, a, a))
    return outs


def _spectral_filter_kernel(g_ref, gt_ref, twr_ref, twi_ref, h_ref, a_ref, o_ref):
    P2 = DFT_P2
    g, gt = g_ref[...], gt_ref[...]
    for kk in range(a_ref.shape[2]):
        tr, ti = twr_ref[kk], twi_ref[kk]
        x = _twiddled_spectrum(a_ref, kk, tr, ti, g)
        xr, xi = x[:P2], x[P2:]
        hr, hi = h_ref[0, 0, kk], h_ref[0, 1, kk]
        y = jnp.concatenate([xr * hr - xi * hi, xr * hi + xi * hr], axis=0).astype(BF16)
        bm = jnp.dot(gt, y, preferred_element_type=F32)
        br, bi = bm[:P2], bm[P2:]
        o_ref[0, 0, kk] = br * tr + bi * ti
        o_ref[0, 1, kk] = bi * tr - br * ti


def _spectral_filter(consts, a, h, layer, kb=8):
    B, _, P1, P2, C = a.shape
    kb = min(kb, P1)
    blk = lambda f: pl.BlockSpec((1, 2, kb, P2, C), f)
    return pl.pallas_call(
        _spectral_filter_kernel, name="hyena_spectral_filter", grid=(B, P1 // kb),
        in_specs=[pl.BlockSpec((2 * P2, 2 * P2), lambda b, i: (0, 0)),
                  pl.BlockSpec((2 * P2, 2 * P2), lambda b, i: (0, 0)),
                  pl.BlockSpec((kb, P2, 1), lambda b, i: (i, 0, 0)),
                  pl.BlockSpec((kb, P2, 1), lambda b, i: (i, 0, 0)),
                  blk(lambda b, i: (layer, 0, i, 0, 0)), blk(lambda b, i: (b, 0, i, 0, 0))],
        out_specs=blk(lambda b, i: (b, 0, i, 0, 0)),
        out_shape=jax.ShapeDtypeStruct(a.shape, F32),
        compiler_params=_params("parallel", "parallel"),
    )(consts["g"], consts["gt"], consts["tw_r"], consts["tw_i"], h, a)


def _idft_rows_kernel(m_ref, b_ref, u_ref, gate_ref, bias_ref, o_ref):
    bm = b_ref[0]
    bm = bm.reshape(bm.shape[0] * bm.shape[1] * bm.shape[2], bm.shape[3]).astype(BF16)
    y = jnp.dot(m_ref[...], bm, preferred_element_type=F32)
    o_ref[0] = gate_ref[0] * (y.reshape(o_ref.shape[1:]) + bias_ref[...] * u_ref[0])


def _idft_rows_gated(consts, bm, u, gate, bias):
    B, p1h, p2, c = u.shape
    tj = DFT_COLS_PER_STEP
    row = lambda: pl.BlockSpec((1, p1h, tj, c), lambda b, j: (b, 0, j, 0))
    return pl.pallas_call(
        _idft_rows_kernel, name="hyena_idft_rows", grid=(B, p2 // tj),
        in_specs=[pl.BlockSpec((p1h * tj, 4 * p1h * tj), lambda b, j: (0, 0)),
                  pl.BlockSpec((1, 2, 2 * p1h, tj, c), lambda b, j: (b, 0, 0, j, 0)),
                  row(), row(), pl.BlockSpec((1, c), lambda b, j: (0, 0))],
        out_specs=row(), out_shape=jax.ShapeDtypeStruct(u.shape, F32),
        compiler_params=_params("parallel", "parallel"),
    )(consts["m_inv"], bm, u, gate, bias.reshape(1, c))


def _short_conv_kernel(x_ref, p_ref, n_ref, w_ref, b_ref, v_ref, x1_ref, x2_ref):
    i = pl.program_id(1)
    x = x_ref[0]
    t = x.shape[0]
    row = lax.broadcasted_iota(jnp.int32, x.shape, 0)
    prev_row = jnp.where(i == 0, 0.0, p_ref[0, 7:8, :])
    next_row = jnp.where(i == pl.num_programs(1) - 1, 0.0, n_ref[0, 0:1, :])
    before = jnp.where(row == 0, prev_row, pltpu.roll(x, 1, axis=0))
    after = jnp.where(row == t - 1, next_row, pltpu.roll(x, t - 1, axis=0))
    w = w_ref[...]
    u = before * w[0:1] + x * w[1:2] + after * w[2:3] + b_ref[...]
    v_ref[0] = u[:, 0:HY_CH]
    x1_ref[0] = u[:, HY_CH:2 * HY_CH]
    x2_ref[0] = u[:, 2 * HY_CH:3 * HY_CH]


def _short_conv(cols, conv_w, conv_b, t=512):
    B, S, W = cols.shape
    t = min(t, S)
    r = t // 8
    out = pl.BlockSpec((1, t, HY_CH), lambda b, i: (b, i, 0))
    return pl.pallas_call(
        _short_conv_kernel, name="hyena_short_conv", grid=(B, S // t),
        in_specs=[pl.BlockSpec((1, t, W), lambda b, i: (b, i, 0)),
                  pl.BlockSpec((1, 8, W), lambda b, i: (b, jnp.maximum(i * r - 1, 0), 0)),
                  pl.BlockSpec((1, 8, W), lambda b, i: (b, jnp.minimum((i + 1) * r, S // 8 - 1), 0)),
                  pl.BlockSpec((HY_SHORT, W), lambda b, i: (0, 0)), pl.BlockSpec((1, W), lambda b, i: (0, 0))],
        out_specs=[out, out, out], out_shape=[jax.ShapeDtypeStruct((B, S, HY_CH), F32)] * 3,
        compiler_params=_params("parallel", "parallel"))(cols, cols, cols, conv_w, conv_b.reshape(1, W))


def _hyena_mixer(consts, cols, conv_w, conv_b, spectra, bias, layer):
    B, S, _ = cols.shape
    P2, C = DFT_P2, HY_CH
    P1 = 2 * S // P2
    v, x1, x2 = _short_conv(cols, conv_w, conv_b)
    rows = lambda a: a.reshape(B, P1 // 2, P2, C)
    z = rows(v)
    for order, gate in enumerate((x1, x2)):
        bm = _spectral_filter(consts, _dft_rows(consts, z), spectra[order], layer)
        z = _idft_rows_gated(consts, bm, z, rows(gate), bias[order])
    return z.reshape(B, S, C)


def _route(lt):
    t = lt.shape[1]
    row = lax.broadcasted_iota(jnp.int32, (8, t), 0)
    lg = jnp.where(row < N_GROUPS, lt[0:8], NEG_INF)
    mg = jnp.max(lg, axis=0, keepdims=True)
    p_top = 1.0 / jnp.sum(jnp.exp(lg - mg), axis=0, keepdims=True)
    g_sel = jnp.min(jnp.where(lg == mg, row, 8), axis=0, keepdims=True)
    le = jnp.zeros((8, t), F32)
    for g in range(N_GROUPS):
        le = jnp.where(g_sel == g, lt[8 + 8 * g:16 + 8 * g], le)
    v1 = jnp.max(le, axis=0, keepdims=True)
    i1 = jnp.min(jnp.where(le == v1, row, 8), axis=0, keepdims=True)
    rest = jnp.where(row == i1, NEG_INF, le)
    v2 = jnp.max(rest, axis=0, keepdims=True)
    i2 = jnp.min(jnp.where(rest == v2, row, 8), axis=0, keepdims=True)
    e = jnp.exp(v2 - v1)
    w1 = 1.0 / (1.0 + e)
    base = g_sel * EXPERTS_PER_GROUP
    idx = jnp.where(row == 0, base + i1, jnp.where(row == 1, base + i2, 0))
    gate = jnp.where(row == 0, p_top * w1, jnp.where(row == 1, p_top * (e * w1), 0.0))
    return idx, gate


def _post_kernel(x_ref, g1_ref, sc_ref, sh_ref, ln_ref, a_ref, b_ref, c_ref, d_ref, wo_ref, wr_ref, br_ref,
                 xo_ref, h_ref, idx_ref, gate_ref):
    wo = wo_ref[0]
    y = (_tn(a_ref[0].astype(BF16), wo[0:256]) + _tn(b_ref[0].astype(BF16), wo[256:512])
         + _tn(c_ref[0].astype(BF16), wo[512:768])
         + jnp.dot(d_ref[0].astype(BF16), wo[768:1024], preferred_element_type=F32))
    x = x_ref[0] + g1_ref[0] * y
    xo_ref[0] = x
    h = _rms(x) * ln_ref[...] * (1.0 + sc_ref[0]) + sh_ref[0]
    h_ref[0] = h
    lt = lax.dot_general(wr_ref[0], h, (((1,), (1,)), ((), ())), preferred_element_type=F32, precision=HIGHEST) + br_ref[0]
    idx, gate = _route(lt)
    idx_ref[...] = idx
    gate_ref[...] = gate


def _post_mixer(x, g1, sc2, sh2, ln_g, layer, mix_a, mix_b, mix_c, mix_d, w_out, w_router, b_router, t=256):
    B, S, D = x.shape
    t = min(t, S)
    nt = S // t
    nat = lambda n: pl.BlockSpec((1, t, n), lambda b, i: (b, i, 0))
    fm = pl.BlockSpec((1, 256, t), lambda b, i: (b, 0, i))
    mod = pl.BlockSpec((1, 1, D), lambda b, i: (b, 0, 0))
    lw = lambda a: pl.BlockSpec((1,) + a.shape[1:], lambda b, i: (layer,) + (0,) * (a.ndim - 1))
    tok = pl.BlockSpec((8, t), lambda b, i: (0, b * nt + i))
    return pl.pallas_call(
        _post_kernel, name="mixer_output_router", grid=(B, nt),
        in_specs=[nat(D), mod, mod, mod, pl.BlockSpec((1, D), lambda b, i: (0, 0)), fm, fm, fm, nat(256),
                  lw(w_out), lw(w_router), lw(b_router)],
        out_specs=[nat(D), nat(D), tok, tok],
        out_shape=[jax.ShapeDtypeStruct((B, S, D), F32), jax.ShapeDtypeStruct((B, S, D), F32),
                   jax.ShapeDtypeStruct((8, B * S), jnp.int32), jax.ShapeDtypeStruct((8, B * S), F32)],
        compiler_params=_params("parallel", "parallel"),
    )(x, g1, sc2, sh2, ln_g[layer].reshape(1, D), mix_a, mix_b, mix_c, mix_d, w_out, w_router, b_router)


def _start_row_gather(src_hbm, dst_ref, sem, row_of):
    for r in range(dst_ref.shape[0]):
        pltpu.make_async_copy(src_hbm.at[pl.ds(row_of(r), 1)], dst_ref.at[pl.ds(r, 1)], sem).start(priority=r % 2)


def _wait_row_gather(src_hbm, dst_ref, sem):
    pltpu.make_async_copy(src_hbm.at[pl.ds(0, dst_ref.shape[0])], dst_ref, sem).wait()


def _expert_kernel(be_ref, tok_ref, h_hbm, wg_ref, wu_ref, wd_ref, o_ref, x_buf, sem):
    i = pl.program_id(0)
    last = pl.num_programs(0) - 1
    slot = i % 2

    @pl.when(i == 0)
    def _():
        _start_row_gather(h_hbm, x_buf.at[0], sem.at[0], lambda r: tok_ref[r])

    nxt = jnp.minimum(i + 1, last) * MOE_ROWS
    _start_row_gather(h_hbm, x_buf.at[1 - slot], sem.at[1 - slot], lambda r: tok_ref[nxt + r])
    _wait_row_gather(h_hbm, x_buf.at[slot], sem.at[slot])

    @pl.when(i == last)
    def _():
        _wait_row_gather(h_hbm, x_buf.at[1 - slot], sem.at[1 - slot])

    x = x_buf[slot].astype(BF16)
    g = jnp.dot(x, wg_ref[0, 0].astype(BF16), preferred_element_type=F32)
    u = jnp.dot(x, wu_ref[0, 0].astype(BF16), preferred_element_type=F32)
    a = (g * jax.nn.sigmoid(g) * u).astype(BF16)
    o_ref[...] = jnp.dot(a, wd_ref[0, 0].astype(BF16), preferred_element_type=F32)


def _expert_ffn(h, slot_tok, block_expert, w_gate, w_up, w_down, layer):
    D = h.shape[1]
    P = slot_tok.shape[0]
    wmap = lambda i, be, tok: (layer, be[i], 0, 0)
    return pl.pallas_call(
        _expert_kernel, name="expert_ffn",
        grid_spec=pltpu.PrefetchScalarGridSpec(
            num_scalar_prefetch=2, grid=(P // MOE_ROWS,),
            in_specs=[pl.BlockSpec(memory_space=pl.ANY),
                      pl.BlockSpec((1, 1, D, D_EXPERT), wmap), pl.BlockSpec((1, 1, D, D_EXPERT), wmap),
                      pl.BlockSpec((1, 1, D_EXPERT, D), wmap)],
            out_specs=pl.BlockSpec((MOE_ROWS, D), lambda i, be, tok: (i, 0)),
            scratch_shapes=[pltpu.VMEM((2, MOE_ROWS, D), F32), pltpu.SemaphoreType.DMA((2,))]),
        out_shape=jax.ShapeDtypeStruct((P, D), F32),
        compiler_params=_params("arbitrary"))(block_expert, slot_tok, h, w_gate, w_up, w_down)


def _dispatch_plan(idx, T):
    e = idx[0:TOP_K]
    onehot = (e.reshape(-1)[:, None] == jnp.arange(N_EXPERTS)[None]).astype(jnp.int32)
    before = jnp.cumsum(onehot, axis=0) - onehot
    rank = jnp.sum(before * onehot, axis=1)
    counts = jnp.sum(onehot, axis=0)
    padded = (counts + MOE_ROWS - 1) // MOE_ROWS * MOE_ROWS
    pad_end = jnp.cumsum(padded)
    dest = ((pad_end - padded)[e.reshape(-1)] + rank).reshape(TOP_K, T)
    n_blocks = (TOP_K * T + N_EXPERTS * (MOE_ROWS - 1) + MOE_ROWS - 1) // MOE_ROWS
    tok = jnp.tile(jnp.arange(T, dtype=jnp.int32), TOP_K)
    slot_tok = jnp.zeros((n_blocks * MOE_ROWS,), jnp.int32).at[dest.reshape(-1)].set(tok)
    starts = jnp.arange(n_blocks, dtype=jnp.int32) * MOE_ROWS
    block_expert = jnp.minimum(jnp.sum((pad_end[None, :] <= starts[:, None]).astype(jnp.int32), axis=1), N_EXPERTS - 1)
    return dest, slot_tok, block_expert.astype(jnp.int32)


def _combine_kernel(dest_ref, x_ref, g2_ref, gate_ref, fg_ref, y_hbm, o_ref, y_buf, sem, *, final, n_tokens):
    t = x_ref.shape[1]
    step = pl.program_id(0) * pl.num_programs(1) + pl.program_id(1)
    last = pl.num_programs(0) * pl.num_programs(1) - 1
    slot = step % 2

    def start(tile, s):
        for k in range(TOP_K):
            _start_row_gather(y_hbm, y_buf.at[s, k], sem.at[s, k], lambda r, k=k: dest_ref[k * n_tokens + tile * t + r])

    def wait(s):
        for k in range(TOP_K):
            _wait_row_gather(y_hbm, y_buf.at[s, k], sem.at[s, k])

    @pl.when(step == 0)
    def _():
        start(0, 0)

    start(jnp.minimum(step + 1, last), 1 - slot)
    wait(slot)

    @pl.when(step == last)
    def _():
        wait(1 - slot)

    gt = gate_ref[...]
    x = x_ref[0] + g2_ref[0] * (gt[:, 0:1] * y_buf[slot, 0] + gt[:, 1:2] * y_buf[slot, 1])
    if final:
        x = _rms(x) * fg_ref[...]
    o_ref[0] = x


def _combine(x, g2, yb, dest, gate_t, final_g, final, t=256):
    B, S, D = x.shape
    t = min(t, S)
    nt = S // t
    nat = pl.BlockSpec((1, t, D), lambda b, i, d: (b, i, 0))
    return pl.pallas_call(
        functools.partial(_combine_kernel, final=final, n_tokens=B * S), name="moe_combine",
        grid_spec=pltpu.PrefetchScalarGridSpec(
            num_scalar_prefetch=1, grid=(B, nt),
            in_specs=[nat, pl.BlockSpec((1, 1, D), lambda b, i, d: (b, 0, 0)),
                      pl.BlockSpec((t, 8), lambda b, i, d: (b * nt + i, 0)),
                      pl.BlockSpec((1, D), lambda b, i, d: (0, 0)),
                      pl.BlockSpec(memory_space=pl.ANY)],
            out_specs=nat,
            scratch_shapes=[pltpu.VMEM((2, TOP_K, t, D), F32), pltpu.SemaphoreType.DMA((2, TOP_K))]),
        out_shape=jax.ShapeDtypeStruct((B, S, D), F32),
        compiler_params=_params("arbitrary", "arbitrary"))(dest.reshape(-1), x, g2, gate_t, final_g.reshape(1, D), yb)


def _moe(x, h, idx, gate, g2, w_gate, w_up, w_down, layer, final_g, final):
    B, S, D = x.shape
    T = B * S
    dest, slot_tok, block_expert = _dispatch_plan(idx, T)
    yb = _expert_ffn(h.reshape(T, D), slot_tok, block_expert, w_gate, w_up, w_down, layer)
    return _combine(x, g2, yb, dest, gate.T, final_g, final)


def kernel(x, c, positions, ln1_g, ln2_g, ada_w, ada_b, w_in, mla_q_norm, mla_kv_norm, mla_w_uq, mla_w_ukv, swa_sink, diff_lambda, diff_norm_g, hy_conv_w, hy_conv_b, hy_w1, hy_b1, hy_w2, hy_b2, hy_w3, hy_b3, hy_w4, hy_sin_freq, hy_bias, w_out, router_g_w, router_g_b, router_e_w, router_e_b, moe_w_gate, moe_w_up, moe_w_down, final_g):
    B, S, D = x.shape
    L = w_in.shape[0]
    mod = _ada_modulation(c, ada_w, ada_b)[:, :B].reshape(L, B, 6, 1, D)
    tab_fm = _rope_tables(positions)
    tab_nat = _natural_rope_table(tab_fm)
    weights = _layer_weight_layouts(w_in, mla_q_norm, mla_kv_norm, mla_w_uq, mla_w_ukv)
    w_out_b = w_out.astype(BF16)
    w_router = jnp.concatenate([router_g_w.transpose(0, 2, 1), jnp.zeros((L, 8 - N_GROUPS, D), F32),
                                router_e_w.transpose(0, 2, 1)], axis=1)
    b_router = jnp.concatenate([router_g_b, jnp.zeros((L, 8 - N_GROUPS), F32), router_e_b], axis=1)[:, :, None]
    consts = _hyena_constants(S)
    filters, stats = _hyena_filters(consts, hy_w1, hy_b1, hy_w2, hy_b2, hy_w3, hy_b3, hy_w4, hy_sin_freq, S)
    spectra = _filter_spectra(consts, filters, stats, S)
    for l in range(L):
        sh1, sc1, g1, sh2, sc2, g2 = (mod[l, :, i] for i in range(6))
        kmla, qn, qp, vmla, kswa, qswa, vswa, kdiff, qdiff, vdiff, hy = _project(
            x, sc1, sh1, ln1_g, l, weights, tab_nat, tab_fm)
        mix_a = _mla_attention(kmla, qn, qp, vmla)
        mix_b = _swa_attention(kswa, qswa, vswa, swa_sink[l])
        lambda_init = 0.8 - 0.6 * math.exp(-0.3 * l)
        mix_c = _diff_attention(kdiff, qdiff, vdiff, diff_lambda[l], diff_norm_g[l], lambda_init)
        mix_d = _hyena_mixer(consts, hy, hy_conv_w[l], hy_conv_b[l], spectra, hy_bias[l], l)
        x, h, idx, gate = _post_mixer(x, g1, sc2, sh2, ln2_g, l, mix_a, mix_b, mix_c, mix_d, w_out_b, w_router, b_router)
        x = _moe(x, h, idx, gate, g2, moe_w_gate, moe_w_up, moe_w_down, l, final_g, final=(l == L - 1))
    return x
```

```python
import functools
import math

import numpy as np
import jax
import jax.numpy as jnp
from jax import lax
from jax.experimental import pallas as pl
from jax.experimental.pallas import tpu as pltpu

F32 = jnp.float32
BF16 = jnp.bfloat16
HIGHEST = lax.Precision.HIGHEST

HEAD_DIM = 64
ROPE_THETA = 10000.0
EPS = 1e-6
NEG_INF = -1e30
LOG2E = 1.4426950408889634
MLA_HEADS, MLA_Q_RANK, MLA_KV_RANK, MLA_NOPE, MLA_ROPE, MLA_V = 4, 256, 128, 64, 32, 64
SWA_HEADS, SWA_KV_HEADS, SWA_WINDOW = 4, 2, 128
DIFF_HEADS, DIFF_QK, DIFF_V = 4, 32, 64
HY_CH, HY_ORDER, HY_DIRS, HY_SHORT, HY_EMB, HY_FFN = 256, 2, 2, 3, 33, 64
HY_BANDS = (HY_EMB - 1) // 2
HY_MIN_DECAY = math.log(1e-2) / 1.5
HY_MAX_DECAY = math.log(1e-2) / 0.3
MLA_COLS = MLA_Q_RANK + MLA_KV_RANK + MLA_ROPE
SWA_COLS = (SWA_HEADS + 2 * SWA_KV_HEADS) * HEAD_DIM
DIFF_COLS = DIFF_HEADS * (4 * DIFF_QK + DIFF_V)
HY_COLS = (HY_ORDER + 1) * HY_CH
N_GROUPS, EXPERTS_PER_GROUP, TOP_K, D_EXPERT = 4, 8, 2, 256
N_EXPERTS = N_GROUPS * EXPERTS_PER_GROUP

V7X_VMEM_LIMIT_BYTES = 56 * 1024 * 1024
LANES = 128
DFT_P2 = 128
DFT_COLS_PER_STEP = 8
MOE_ROWS = 512


def _params(*sem):
    return pltpu.CompilerParams(dimension_semantics=sem, vmem_limit_bytes=V7X_VMEM_LIMIT_BYTES)


def _nt(a, b):
    return lax.dot_general(a, b, (((1,), (1,)), ((), ())), preferred_element_type=F32)


def _tn(a, b):
    return lax.dot_general(a, b, (((0,), (0,)), ((), ())), preferred_element_type=F32)


def _ada_kernel(c_ref, w_ref, b_ref, o_ref):
    c = c_ref[...]
    act = c * jax.nn.sigmoid(c)
    o_ref[0] = jnp.dot(act, w_ref[0], preferred_element_type=F32, precision=HIGHEST) + b_ref[0]


def _ada_modulation(c, ada_w, ada_b):
    L, D, N = ada_w.shape
    B = c.shape[0]
    c8 = jnp.zeros((8, D), F32).at[:B].set(c)
    tn = 1024
    return pl.pallas_call(
        _ada_kernel, name="ada_modulation", grid=(L, N // tn),
        in_specs=[pl.BlockSpec((8, D), lambda l, j: (0, 0)),
                  pl.BlockSpec((1, D, tn), lambda l, j: (l, 0, j)),
                  pl.BlockSpec((1, 1, tn), lambda l, j: (l, 0, j))],
        out_specs=pl.BlockSpec((1, 8, tn), lambda l, j: (l, 0, j)),
        out_shape=jax.ShapeDtypeStruct((L, 8, N), F32),
        compiler_params=_params("parallel", "parallel"))(c8, ada_w, ada_b.reshape(L, 1, N))


def _rope_table_kernel(p_ref, o_ref):
    pos = p_ref[0].astype(F32)
    for half, base in ((32, 0), (16, 128)):
        j = lax.broadcasted_iota(jnp.int32, (half, 1), 0).astype(F32)
        inv = jnp.exp(-math.log(ROPE_THETA) * j / half)
        ang = inv * pos
        cs, sn = jnp.cos(ang), jnp.sin(ang)
        o_ref[0, base:base + half] = cs
        o_ref[0, base + half:base + 2 * half] = cs
        o_ref[0, base + 2 * half:base + 3 * half] = -sn
        o_ref[0, base + 3 * half:base + 4 * half] = sn


def _rope_tables(positions):
    B, S = positions.shape
    return pl.pallas_call(
        _rope_table_kernel, name="rope_tables", grid=(B,),
        in_specs=[pl.BlockSpec((1, 1, S), lambda b: (b, 0, 0))],
        out_specs=pl.BlockSpec((1, 192, S), lambda b: (b, 0, 0)),
        out_shape=jax.ShapeDtypeStruct((B, 192, S), F32),
        compiler_params=_params("parallel"))(positions.reshape(B, 1, S))


def _partner(n_heads, dim):
    r = np.arange(n_heads * dim)
    return (r // dim) * dim + (r % dim + dim // 2) % dim


def _layer_weight_layouts(w_in, mla_q_norm, mla_kv_norm, mla_w_uq, mla_w_ukv):
    L, D, _ = w_in.shape
    oB = MLA_COLS
    oC = oB + SWA_COLS
    oD = oC + DIFF_COLS
    Z = w_in.shape[2]
    wz = jnp.concatenate([w_in, jnp.zeros((L, D, 1), F32)], axis=2)

    kpe = MLA_Q_RANK + MLA_KV_RANK + np.arange(MLA_ROPE)
    pad96 = np.full(LANES - MLA_ROPE, Z)
    swa_k = oB + SWA_HEADS * HEAD_DIM + np.arange(SWA_KV_HEADS * HEAD_DIM)
    h, j, r = np.meshgrid(np.arange(DIFF_HEADS), np.arange(2), np.arange(DIFF_QK), indexing="ij")
    diff_src = (h * 2 * DIFF_QK + j * DIFF_QK + r).transpose(1, 0, 2).reshape(-1)
    diff_q = oC + diff_src
    diff_k = oC + DIFF_HEADS * 2 * DIFF_QK + diff_src
    nat_idx = np.concatenate([
        np.arange(MLA_Q_RANK + MLA_KV_RANK),
        kpe, pad96, kpe[_partner(1, MLA_ROPE)], pad96,
        swa_k, swa_k[_partner(SWA_KV_HEADS, HEAD_DIM)],
        diff_k, diff_k[_partner(2 * DIFF_HEADS, DIFF_QK)],
        oD + np.arange(HY_COLS)])
    w_nat = jnp.take(wz, jnp.asarray(nat_idx), axis=2).astype(BF16)

    swa_q = oB + np.arange(SWA_HEADS * HEAD_DIM)
    swa_v = oB + (SWA_HEADS + SWA_KV_HEADS) * HEAD_DIM + np.arange(SWA_KV_HEADS * HEAD_DIM)
    diff_v = oC + DIFF_HEADS * 4 * DIFF_QK + np.arange(DIFF_HEADS * DIFF_V)
    fm_idx = np.concatenate([
        swa_q, swa_q[_partner(SWA_HEADS, HEAD_DIM)], swa_v,
        diff_q, diff_q[_partner(2 * DIFF_HEADS, DIFF_QK)], diff_v])
    s_swa = HEAD_DIM ** -0.5 * LOG2E
    s_diff = DIFF_QK ** -0.5 * LOG2E
    fm_scale = np.concatenate([np.full(512, s_swa), np.ones(128), np.full(512, s_diff), np.ones(256)]).astype(np.float32)
    w_fm = (jnp.take(wz, jnp.asarray(fm_idx), axis=2) * fm_scale).transpose(0, 2, 1).astype(BF16)

    s_mla = (MLA_NOPE + MLA_ROPE) ** -0.5 * LOG2E
    uq = mla_w_uq * mla_q_norm[:, :, None] * s_mla
    hq = np.arange(MLA_HEADS)[:, None] * (MLA_NOPE + MLA_ROPE)
    qn_idx = (hq + np.arange(MLA_NOPE)[None]).reshape(-1)
    qp_idx = (hq + MLA_NOPE + np.arange(MLA_ROPE)[None]).reshape(-1)
    w_qn = jnp.take(uq, jnp.asarray(qn_idx), axis=2).transpose(0, 2, 1).astype(BF16)
    w_qpa = jnp.take(uq, jnp.asarray(qp_idx), axis=2).transpose(0, 2, 1).astype(BF16)
    w_qpb = jnp.take(uq, jnp.asarray(qp_idx[_partner(MLA_HEADS, MLA_ROPE)]), axis=2).transpose(0, 2, 1).astype(BF16)
    ukv = mla_w_ukv * mla_kv_norm[:, :, None]
    hk = np.arange(MLA_HEADS)[:, None] * (MLA_NOPE + MLA_V)
    kn_idx = (hk + np.arange(MLA_NOPE)[None]).reshape(-1)
    v_idx = (hk + MLA_NOPE + np.arange(MLA_V)[None]).reshape(-1)
    w_kn = jnp.take(ukv, jnp.asarray(kn_idx), axis=2).astype(BF16)
    w_v = jnp.take(ukv, jnp.asarray(v_idx), axis=2).transpose(0, 2, 1).astype(BF16)
    return w_nat, w_fm, w_qn, w_qpa, w_qpb, w_kn, w_v


def _natural_rope_table(tab_fm):
    t = tab_fm.transpose(0, 2, 1)
    cos64, sin64, cos32, sin32 = t[..., 0:64], t[..., 64:128], t[..., 128:160], t[..., 160:192]
    z = jnp.zeros(t.shape[:2] + (LANES - MLA_ROPE,), F32)
    return jnp.concatenate([cos32, z, sin32, z,
                            jnp.tile(cos64, (1, 1, 2)), jnp.tile(sin64, (1, 1, 2)),
                            jnp.tile(cos32, (1, 1, 8)), jnp.tile(sin32, (1, 1, 8))], axis=-1)


def _rms(x):
    return x * lax.rsqrt(jnp.mean(x * x, axis=-1, keepdims=True) + EPS)


def _proj_kernel(x_ref, sc_ref, sh_ref, g_ref, wn_ref, wf_ref, wqn_ref, wqpa_ref, wqpb_ref, wkn_ref, wv_ref,
                 tn_ref, tf_ref,
                 kmla_ref, qn_ref, qp_ref, vmla_ref, kswa_ref, qswa_ref, vswa_ref, kdiff_ref, qdiff_ref, vdiff_ref, hy_ref):
    x = x_ref[0]
    t = x.shape[0]
    h = _rms(x) * g_ref[...] * (1.0 + sc_ref[0]) + sh_ref[0]
    hb = h.astype(BF16)
    cn = jnp.dot(hb, wn_ref[0], preferred_element_type=F32)
    tab = tn_ref[0]
    cqn = _rms(cn[:, 0:256]).astype(BF16)
    ckvn = _rms(cn[:, 256:384]).astype(BF16)
    kpe = (cn[:, 384:512] * tab[:, 0:128] + cn[:, 512:640] * tab[:, 128:256]).astype(BF16)
    kswa_ref[0] = (cn[:, 640:768] * tab[:, 256:384] + cn[:, 768:896] * tab[:, 384:512]).astype(BF16)
    kdiff_ref[0] = (cn[:, 896:1152] * tab[:, 512:768] + cn[:, 1152:1408] * tab[:, 768:1024]).astype(BF16)
    hy_ref[0] = cn[:, 1408:2176]
    kn = jnp.dot(ckvn, wkn_ref[0], preferred_element_type=F32).astype(BF16)
    kmla_ref[0, 0] = jnp.concatenate([kn[:, 0:128], kpe], axis=1)
    kmla_ref[0, 1] = jnp.concatenate([kn[:, 128:256], kpe], axis=1)

    ft = _nt(wf_ref[0], hb)
    tf = tf_ref[0]
    cos64, sin64, cos32, sin32 = tf[0:64], tf[64:128], tf[128:160], tf[160:192]
    qswa = ft[0:256].reshape(4, 64, t) * cos64[None] + ft[256:512].reshape(4, 64, t) * sin64[None]
    qswa_ref[0] = qswa.reshape(256, t).astype(BF16)
    vswa_ref[0] = _with_ones_rows(ft[512:640], t)
    qdiff = ft[640:896].reshape(8, 32, t) * cos32[None] + ft[896:1152].reshape(8, 32, t) * sin32[None]
    qdiff_ref[0] = qdiff.reshape(256, t).astype(BF16)
    vdiff_ref[0] = _with_ones_rows(ft[1152:1408], t)
    qn_ref[0] = _nt(wqn_ref[0], cqn).astype(BF16)
    qp = _nt(wqpa_ref[0], cqn).reshape(4, 32, t) * cos32[None] + _nt(wqpb_ref[0], cqn).reshape(4, 32, t) * sin32[None]
    qp_ref[0] = qp.reshape(128, t).astype(BF16)
    vmla_ref[0] = _with_ones_rows(_nt(wv_ref[0], ckvn), t)


def _project(x, sc1, sh1, ln_g, layer, weights, tab_nat, tab_fm, t=512):
    B, S, D = x.shape
    w_nat, w_fm, w_qn, w_qpa, w_qpb, w_kn, w_v = weights
    wspec = lambda a: pl.BlockSpec((1,) + a.shape[1:], lambda b, i: (layer,) + (0,) * (a.ndim - 1))
    nat = lambda n: pl.BlockSpec((1, t, n), lambda b, i: (b, i, 0))
    fm = lambda n: pl.BlockSpec((1, n, t), lambda b, i: (b, 0, i))
    outs = [
        (jax.ShapeDtypeStruct((B, 2, S, 256), BF16), pl.BlockSpec((1, 2, t, 256), lambda b, i: (b, 0, i, 0))),
        (jax.ShapeDtypeStruct((B, 256, S), BF16), fm(256)),
        (jax.ShapeDtypeStruct((B, 128, S), BF16), fm(128)),
        (jax.ShapeDtypeStruct((B, 4 * V_ROWS, S), BF16), fm(4 * V_ROWS)),
        (jax.ShapeDtypeStruct((B, S, 128), BF16), nat(128)),
        (jax.ShapeDtypeStruct((B, 256, S), BF16), fm(256)),
        (jax.ShapeDtypeStruct((B, 2 * V_ROWS, S), BF16), fm(2 * V_ROWS)),
        (jax.ShapeDtypeStruct((B, S, 256), BF16), nat(256)),
        (jax.ShapeDtypeStruct((B, 256, S), BF16), fm(256)),
        (jax.ShapeDtypeStruct((B, 4 * V_ROWS, S), BF16), fm(4 * V_ROWS)),
        (jax.ShapeDtypeStruct((B, S, HY_COLS), F32), nat(HY_COLS)),
    ]
    return pl.pallas_call(
        _proj_kernel, name="mixer_projection", grid=(B, S // t),
        in_specs=[nat(D),
                  pl.BlockSpec((1, 1, D), lambda b, i: (b, 0, 0)),
                  pl.BlockSpec((1, 1, D), lambda b, i: (b, 0, 0)),
                  pl.BlockSpec((1, D), lambda b, i: (0, 0)),
                  wspec(w_nat), wspec(w_fm), wspec(w_qn), wspec(w_qpa), wspec(w_qpb), wspec(w_kn), wspec(w_v),
                  nat(1024), fm(192)],
        out_specs=[o[1] for o in outs], out_shape=[o[0] for o in outs],
        compiler_params=_params("parallel", "parallel"),
    )(x, sc1, sh1, ln_g[layer].reshape(1, D), w_nat, w_fm, w_qn, w_qpa, w_qpb, w_kn, w_v, tab_nat, tab_fm)


V_ROWS = 80


def _with_ones_rows(v, t):
    tail = (lax.broadcasted_iota(jnp.int32, (V_ROWS - 64, t), 0) == 0).astype(F32)
    parts = []
    for h in range(v.shape[0] // 64):
        parts += [v[h * 64:(h + 1) * 64], tail]
    return jnp.concatenate(parts, axis=0).astype(BF16)


def _flash_scratch(tk, tq):
    return ([pltpu.VMEM((256, tq), BF16)] + [pltpu.VMEM((tk, tq), F32)] * 4 + [pltpu.VMEM((tk, tq), BF16)] * 4
            + [pltpu.VMEM((1, tq), F32)] * 8 + [pltpu.VMEM((V_ROWS, tq), F32), pltpu.VMEM((1, tq), F32)])


def _flash_sweep(k_at, v_at, q_pad, n_blocks, scratch):
    s, p, al, top = scratch[0:4], scratch[4:8], scratch[8:12], scratch[12:16]
    acc_ref, m_ref = scratch[16], scratch[17]

    def scores(i, b):
        x = jnp.dot(k_at(i), q_pad, preferred_element_type=F32)
        s[b][...] = x
        top[b][...] = jnp.max(x, axis=0, keepdims=True)

    def softmax(src, dst):
        m_old = m_ref[...]
        m_new = jnp.maximum(m_old, top[src][...])
        al[dst][...] = jnp.exp2(m_old - m_new)
        p[dst][...] = jnp.exp2(s[src][...] - m_new).astype(BF16)
        m_ref[...] = m_new

    def values(i, b):
        acc_ref[...] = al[b][...] * acc_ref[...] + jnp.dot(v_at(i), p[b][...], preferred_element_type=F32)

    m_ref[...] = jnp.full(m_ref.shape, NEG_INF, F32)
    acc_ref[...] = jnp.zeros(acc_ref.shape, F32)
    scores(0, 2)
    scores(1, 3)
    softmax(2, 0)
    softmax(3, 1)
    scores(2, 0)
    scores(3, 1)

    def body(j, carry):
        i = 4 * j
        scores(i + 4, 2), scores(i + 5, 3)
        softmax(0, 2), softmax(1, 3)
        values(i, 0), values(i + 1, 1)
        scores(i + 6, 0), scores(i + 7, 1)
        softmax(2, 0), softmax(3, 1)
        values(i + 2, 2), values(i + 3, 3)
        return carry

    lax.fori_loop(0, n_blocks // 4 - 1, body, 0)
    n = n_blocks
    softmax(0, 2), softmax(1, 3)
    values(n - 4, 0), values(n - 3, 1), values(n - 2, 2), values(n - 1, 3)
    acc = acc_ref[...]
    return acc[0:64] / acc[64:65]


def _mla_kernel(k_ref, qn_ref, qp_ref, v_ref, o_ref, qpad_ref, *scratch, tk):
    slot = pl.program_id(1) % 2
    S = k_ref.shape[2]
    tq = qn_ref.shape[2]
    qn = qn_ref[0]
    zero = jnp.zeros_like(qn)
    qpad_ref[0:64, :] = jnp.where(slot == 0, qn, zero)
    qpad_ref[64:128, :] = jnp.where(slot == 1, qn, zero)
    qpad_ref[128:160, :] = qp_ref[0]
    qpad_ref[160:256, :] = jnp.zeros((96, tq), BF16)
    o_ref[0] = _flash_sweep(lambda i: k_ref[0, 0, pl.ds(pl.multiple_of(i * tk, tk), tk), :],
                            lambda i: v_ref[0, :, pl.ds(pl.multiple_of(i * tk, tk), tk)],
                            qpad_ref[...], S // tk, scratch)


def _mla_attention(k, qn, qp, v, tq=1024, tk=512):
    B, _, S, _ = k.shape
    tq, tk = min(tq, S), min(tk, S // 4)
    return pl.pallas_call(
        functools.partial(_mla_kernel, tk=tk), name="mla_attention", grid=(B, MLA_HEADS, S // tq),
        in_specs=[pl.BlockSpec((1, 1, S, 256), lambda b, h, i: (b, h // 2, 0, 0)),
                  pl.BlockSpec((1, 64, tq), lambda b, h, i: (b, h, i)),
                  pl.BlockSpec((1, 32, tq), lambda b, h, i: (b, h, i)),
                  pl.BlockSpec((1, V_ROWS, S), lambda b, h, i: (b, h, 0))],
        out_specs=pl.BlockSpec((1, 64, tq), lambda b, h, i: (b, h, i)),
        out_shape=jax.ShapeDtypeStruct((B, 256, S), F32),
        scratch_shapes=_flash_scratch(tk, tq),
        compiler_params=_params("parallel", "parallel", "arbitrary"))(k, qn, qp, v)


def _diff_kernel(lam_ref, g_ref, k_ref, q1_ref, q2_ref, v_ref, o_ref, qpad_ref, *scratch, tk, lambda_init):
    head = pl.program_id(1)
    S = k_ref.shape[1]
    maps = []
    for j, q_ref in enumerate((q1_ref, q2_ref)):
        q = q_ref[0]
        qpad_ref[...] = jnp.zeros(qpad_ref.shape, BF16)
        for s in range(DIFF_HEADS):
            r0 = j * 128 + s * DIFF_QK
            qpad_ref[r0:r0 + DIFF_QK, :] = jnp.where(head == s, q, jnp.zeros_like(q))
        maps.append(_flash_sweep(lambda i: k_ref[0, pl.ds(pl.multiple_of(i * tk, tk), tk), :],
                                 lambda i: v_ref[0, :, pl.ds(pl.multiple_of(i * tk, tk), tk)],
                                 qpad_ref[...], S // tk, scratch))
    lv = lam_ref[...]
    lam = (jnp.exp(jnp.sum(lv[0:1] * lv[1:2], axis=1, keepdims=True))
           - jnp.exp(jnp.sum(lv[2:3] * lv[3:4], axis=1, keepdims=True)) + lambda_init)
    o = maps[0] - lam * maps[1]
    o = o * lax.rsqrt(jnp.mean(o * o, axis=0, keepdims=True) + EPS)
    o_ref[0] = o * g_ref[...] * (1.0 - lambda_init)


def _diff_attention(k, q, v, lam_vecs, norm_g, lambda_init, tq=1024, tk=512):
    B, S, _ = k.shape
    tq, tk = min(tq, S), min(tk, S // 4)
    return pl.pallas_call(
        functools.partial(_diff_kernel, tk=tk, lambda_init=lambda_init), name="diff_attention",
        grid=(B, DIFF_HEADS, S // tq),
        in_specs=[pl.BlockSpec((4, DIFF_QK), lambda b, h, i: (0, 0)),
                  pl.BlockSpec((DIFF_V, 1), lambda b, h, i: (0, 0)),
                  pl.BlockSpec((1, S, 256), lambda b, h, i: (b, 0, 0)),
                  pl.BlockSpec((1, 32, tq), lambda b, h, i: (b, h, i)),
                  pl.BlockSpec((1, 32, tq), lambda b, h, i: (b, DIFF_HEADS + h, i)),
                  pl.BlockSpec((1, V_ROWS, S), lambda b, h, i: (b, h, 0))],
        out_specs=pl.BlockSpec((1, 64, tq), lambda b, h, i: (b, h, i)),
        out_shape=jax.ShapeDtypeStruct((B, 256, S), F32),
        scratch_shapes=_flash_scratch(tk, tq),
        compiler_params=_params("parallel", "parallel", "arbitrary"))(lam_vecs, norm_g.reshape(DIFF_V, 1), k, q, q, v)


def _swa_kernel(sink_ref, q_ref, k_prev, k_main, k_next, v_prev, v_main, v_next, o_ref, *, S):
    qi = pl.program_id(1)
    tq = q_ref.shape[2]
    k = jnp.concatenate([k_prev[0], k_main[0], k_next[0]], axis=0)
    v = jnp.concatenate([v_prev[0], v_main[0], v_next[0]], axis=1)
    shape = (tq + 2 * SWA_WINDOW, tq)
    kpos = qi * tq - SWA_WINDOW + lax.broadcasted_iota(jnp.int32, shape, 0)
    qpos = qi * tq + lax.broadcasted_iota(jnp.int32, shape, 1)
    valid = (jnp.abs(kpos - qpos) <= SWA_WINDOW) & (kpos >= 0) & (kpos < S)
    for head in range(SWA_HEADS):
        kv = head // (SWA_HEADS // SWA_KV_HEADS)
        q = q_ref[0, head * HEAD_DIM:(head + 1) * HEAD_DIM, :]
        zero = jnp.zeros_like(q)
        q_pad = jnp.concatenate([q, zero] if kv == 0 else [zero, q], axis=0)
        s = jnp.where(valid, jnp.dot(k, q_pad, preferred_element_type=F32), NEG_INF)
        sink = sink_ref[head] * LOG2E
        m = jnp.maximum(jnp.max(s, axis=0, keepdims=True), sink)
        e = jnp.exp2(s - m).astype(BF16)
        acc = jnp.dot(v[kv * V_ROWS:(kv + 1) * V_ROWS], e, preferred_element_type=F32)
        o_ref[0, head * HEAD_DIM:(head + 1) * HEAD_DIM, :] = acc[0:HEAD_DIM] / (acc[HEAD_DIM:HEAD_DIM + 1] + jnp.exp2(sink - m))


def _swa_attention(k, q, v, sink, tq=512):
    B, S, _ = k.shape
    W = SWA_WINDOW
    tq = min(tq, S)
    r = tq // W
    nb = S // W
    prev = lambda i: jnp.maximum(i * r - 1, 0)
    nxt = lambda i: jnp.minimum((i + 1) * r, nb - 1)
    return pl.pallas_call(
        functools.partial(_swa_kernel, S=S), name="swa_attention", grid=(B, S // tq),
        in_specs=[pl.BlockSpec(memory_space=pltpu.SMEM),
                  pl.BlockSpec((1, 256, tq), lambda b, i: (b, 0, i)),
                  pl.BlockSpec((1, W, 128), lambda b, i: (b, prev(i), 0)),
                  pl.BlockSpec((1, tq, 128), lambda b, i: (b, i, 0)),
                  pl.BlockSpec((1, W, 128), lambda b, i: (b, nxt(i), 0)),
                  pl.BlockSpec((1, 2 * V_ROWS, W), lambda b, i: (b, 0, prev(i))),
                  pl.BlockSpec((1, 2 * V_ROWS, tq), lambda b, i: (b, 0, i)),
                  pl.BlockSpec((1, 2 * V_ROWS, W), lambda b, i: (b, 0, nxt(i)))],
        out_specs=pl.BlockSpec((1, 256, tq), lambda b, i: (b, 0, i)),
        out_shape=jax.ShapeDtypeStruct((B, 256, S), F32),
        compiler_params=_params("parallel", "parallel"))(sink, q, k, k, k, v, v, v)


def _hyena_constants(S):
    N = 2 * S
    P2 = DFT_P2
    P1 = N // P2
    t = np.linspace(0.0, 1.0, S)
    w = 2.0 * math.pi * np.arange(S) / S
    f = np.linspace(1e-4, HY_BANDS - 1, HY_BANDS)
    z = np.concatenate([t[:, None], np.cos(w[:, None] * f), -np.sin(w[:, None] * f), np.zeros((S, 40 - HY_EMB))], axis=1)
    decay = np.linspace(HY_MIN_DECAY, HY_MAX_DECAY, HY_CH)
    window = np.exp(-t[:, None] * np.abs(decay))
    k1 = np.arange(P1)
    n1 = np.arange(P1 // 2)
    a1 = 2.0 * math.pi * np.outer(k1, n1) / P1
    m_fwd = np.concatenate([np.cos(a1), -np.sin(a1)], axis=0)
    m_inv = np.concatenate([np.cos(a1.T), -np.sin(a1.T)], axis=1)
    a2 = 2.0 * math.pi * np.outer(np.arange(P2), np.arange(P2)) / P2
    c2, s2 = np.cos(a2), np.sin(a2)
    g = np.block([[c2, s2], [-s2, c2]])
    at = 2.0 * math.pi * np.outer(k1, np.arange(P2)) / N
    eye = np.eye(DFT_COLS_PER_STEP)
    return dict(z=jnp.asarray(z, F32), window=jnp.asarray(window, F32),
                m_fwd=jnp.asarray(np.kron(m_fwd, eye), BF16), m_inv=jnp.asarray(np.kron(m_inv, eye), BF16),
                g=jnp.asarray(g, BF16), gt=jnp.asarray(g.T, BF16),
                tw_r=jnp.asarray(np.cos(at)[:, :, None], F32), tw_i=jnp.asarray(-np.sin(at)[:, :, None], F32))


def _filter_kernel(z_ref, win_ref, w1_ref, b1_ref, w2_ref, b2_ref, w3_ref, b3_ref, w4_ref, f_ref, o_ref, st_ref):
    dot = functools.partial(jnp.dot, preferred_element_type=F32, precision=HIGHEST)
    fr = f_ref[0]
    h = jnp.sin(fr[0:1] * (dot(z_ref[...], w1_ref[0]) + b1_ref[0]))
    h = jnp.sin(fr[1:2] * (dot(h, w2_ref[0]) + b2_ref[0]))
    h = jnp.sin(fr[2:3] * (dot(h, w3_ref[0]) + b3_ref[0]))
    out = dot(h, w4_ref[0])
    win = win_ref[...]
    parts = [out[:, q * HY_CH:(q + 1) * HY_CH] * win for q in range(4)]
    for q in range(4):
        o_ref[0, q] = parts[q]
    filt = jnp.concatenate(parts, axis=1)
    ssq = jnp.sum(filt * filt, axis=0, keepdims=True)

    @pl.when(pl.program_id(1) == 0)
    def _():
        cross = filt[0:1, 0:512] * filt[0:1, 512:1024]
        st_ref[0] = jnp.concatenate([ssq, jnp.concatenate([cross, jnp.zeros_like(cross)], axis=1),
                                     jnp.zeros((6, 1024), F32)], axis=0)

    @pl.when(pl.program_id(1) != 0)
    def _():
        st_ref[0, 0:1] = st_ref[0, 0:1] + ssq


def _hyena_filters(consts, hy_w1, hy_b1, hy_w2, hy_b2, hy_w3, hy_b3, hy_w4, hy_sin_freq, S):
    L = hy_w1.shape[0]
    ts = min(512, S)
    w1 = jnp.concatenate([hy_w1, jnp.zeros((L, 40 - HY_EMB, HY_FFN), F32)], axis=1)
    lw = lambda a: pl.BlockSpec((1,) + a.shape[1:], lambda l, i: (l,) + (0,) * (a.ndim - 1))
    b1, b2, b3 = (b.reshape(L, 1, HY_FFN) for b in (hy_b1, hy_b2, hy_b3))
    return pl.pallas_call(
        _filter_kernel, name="hyena_filters", grid=(L, S // ts),
        in_specs=[pl.BlockSpec((ts, 40), lambda l, i: (i, 0)), pl.BlockSpec((ts, HY_CH), lambda l, i: (i, 0)),
                  lw(w1), lw(b1), lw(hy_w2), lw(b2), lw(hy_w3), lw(b3), lw(hy_w4), lw(hy_sin_freq)],
        out_specs=[pl.BlockSpec((1, 4, ts, HY_CH), lambda l, i: (l, 0, i, 0)),
                   pl.BlockSpec((1, 8, 1024), lambda l, i: (l, 0, 0))],
        out_shape=[jax.ShapeDtypeStruct((L, 4, S, HY_CH), F32), jax.ShapeDtypeStruct((L, 8, 1024), F32)],
        compiler_params=_params("parallel", "arbitrary"),
    )(consts["z"], consts["window"], w1, b1, hy_w2, b2, hy_w3, b3, hy_w4, hy_sin_freq)


def _dft_rows_kernel(m_ref, x_ref, o_ref):
    x = x_ref[0]
    x = x.reshape(x.shape[0] * x.shape[1], x.shape[2]).astype(BF16)
    a = jnp.dot(m_ref[...], x, preferred_element_type=F32)
    o_ref[0] = a.reshape(o_ref.shape[1:])


def _dft_rows(consts, x):
    nb, p1h, p2, c = x.shape
    tj = DFT_COLS_PER_STEP
    return pl.pallas_call(
        _dft_rows_kernel, name="hyena_dft_rows", grid=(nb, p2 // tj),
        in_specs=[pl.BlockSpec((4 * p1h * tj, p1h * tj), lambda b, j: (0, 0)),
                  pl.BlockSpec((1, p1h, tj, c), lambda b, j: (b, 0, j, 0))],
        out_specs=pl.BlockSpec((1, 2, 2 * p1h, tj, c), lambda b, j: (b, 0, 0, j, 0)),
        out_shape=jax.ShapeDtypeStruct((nb, 2, 2 * p1h, p2, c), F32),
        compiler_params=_params("parallel", "parallel"))(consts["m_fwd"], x)


def _twiddled_spectrum(a_ref, kk, tr, ti, g):
    ar, ai = a_ref[0, 0, kk], a_ref[0, 1, kk]
    pre = jnp.concatenate([ar * tr - ai * ti, ar * ti + ai * tr], axis=0).astype(BF16)
    return jnp.dot(g, pre, preferred_element_type=F32)


def _filter_spectrum_kernel(g_ref, twr_ref, twi_ref, st_ref, af_ref, ab_ref, h_ref, *, order, n_total):
    P2 = DFT_P2
    kb = af_ref.shape[2]
    st = st_ref[0]
    c0 = order * HY_CH
    ssq = st[0:1, c0:c0 + HY_CH] + st[0:1, 512 + c0:512 + c0 + HY_CH] + 2.0 * st[1:2, c0:c0 + HY_CH]
    scale = lax.rsqrt(ssq + EPS) * (1.0 / n_total)
    g = g_ref[...]
    for kk in range(kb):
        tr, ti = twr_ref[kk], twi_ref[kk]
        xf = _twiddled_spectrum(af_ref, kk, tr, ti, g)
        xb = _twiddled_spectrum(ab_ref, kk, tr, ti, g)
        h_ref[0, 0, kk] = (xf[:P2] + xb[:P2]) * scale
        h_ref[0, 1, kk] = (xf[P2:] - xb[P2:]) * scale


def _filter_spectra(consts, filters, stats, S, kb=8):
    L = filters.shape[0]
    P2, C = DFT_P2, HY_CH
    P1 = 2 * S // P2
    kb = min(kb, P1)
    a = _dft_rows(consts, filters.reshape(L * 4, P1 // 2, P2, C))
    outs = []
    for order in range(HY_ORDER):
        outs.append(pl.pallas_call(
            functools.partial(_filter_spectrum_kernel, order=order, n_total=2 * S),
            name=f"hyena_filter_spectrum_{order}", grid=(L, P1 // kb),
            in_specs=[pl.BlockSpec((2 * P2, 2 * P2), lambda l, i: (0, 0)),
                      pl.BlockSpec((kb, P2, 1), lambda l, i: (i, 0, 0)),
                      pl.BlockSpec((kb, P2, 1), lambda l, i: (i, 0, 0)),
                      pl.BlockSpec((1, 8, 1024), lambda l, i: (l, 0, 0)),
                      pl.BlockSpec((1, 2, kb, P2, C), lambda l, i: (l * 4 + order, 0, i, 0, 0)),
                      pl.BlockSpec((1, 2, kb, P2, C), lambda l, i: (l * 4 + 2 + order, 0, i, 0, 0))],
            out_specs=pl.BlockSpec((1, 2, kb, P2, C), lambda l, i: (l, 0, i, 0, 0)),
            out_shape=jax.ShapeDtypeStruct((L, 2, P1, P2, C), F32),
            compiler_params=_params("parallel", "parallel"),
        )(consts["g"], consts["tw_r"], consts["tw_i"], stats, a, a))
    return outs


def _spectral_filter_kernel(g_ref, gt_ref, twr_ref, twi_ref, h_ref, a_ref, o_ref):
    P2 = DFT_P2
    g, gt = g_ref[...], gt_ref[...]
    for kk in range(a_ref.shape[2]):
        tr, ti = twr_ref[kk], twi_ref[kk]
        x = _twiddled_spectrum(a_ref, kk, tr, ti, g)
        xr, xi = x[:P2], x[P2:]
        hr, hi = h_ref[0, 0, kk], h_ref[0, 1, kk]
        y = jnp.concatenate([xr * hr - xi * hi, xr * hi + xi * hr], axis=0).astype(BF16)
        bm = jnp.dot(gt, y, preferred_element_type=F32)
        br, bi = bm[:P2], bm[P2:]
        o_ref[0, 0, kk] = br * tr + bi * ti
        o_ref[0, 1, kk] = bi * tr - br * ti


def _spectral_filter(consts, a, h, layer, kb=8):
    B, _, P1, P2, C = a.shape
    kb = min(kb, P1)
    blk = lambda f: pl.BlockSpec((1, 2, kb, P2, C), f)
    return pl.pallas_call(
        _spectral_filter_kernel, name="hyena_spectral_filter", grid=(B, P1 // kb),
        in_specs=[pl.BlockSpec((2 * P2, 2 * P2), lambda b, i: (0, 0)),
                  pl.BlockSpec((2 * P2, 2 * P2), lambda b, i: (0, 0)),
                  pl.BlockSpec((kb, P2, 1), lambda b, i: (i, 0, 0)),
                  pl.BlockSpec((kb, P2, 1), lambda b, i: (i, 0, 0)),
                  blk(lambda b, i: (layer, 0, i, 0, 0)), blk(lambda b, i: (b, 0, i, 0, 0))],
        out_specs=blk(lambda b, i: (b, 0, i, 0, 0)),
        out_shape=jax.ShapeDtypeStruct(a.shape, F32),
        compiler_params=_params("parallel", "parallel"),
    )(consts["g"], consts["gt"], consts["tw_r"], consts["tw_i"], h, a)


def _idft_rows_kernel(m_ref, b_ref, u_ref, gate_ref, bias_ref, o_ref):
    bm = b_ref[0]
    bm = bm.reshape(bm.shape[0] * bm.shape[1] * bm.shape[2], bm.shape[3]).astype(BF16)
    y = jnp.dot(m_ref[...], bm, preferred_element_type=F32)
    o_ref[0] = gate_ref[0] * (y.reshape(o_ref.shape[1:]) + bias_ref[...] * u_ref[0])


def _idft_rows_gated(consts, bm, u, gate, bias):
    B, p1h, p2, c = u.shape
    tj = DFT_COLS_PER_STEP
    row = lambda: pl.BlockSpec((1, p1h, tj, c), lambda b, j: (b, 0, j, 0))
    return pl.pallas_call(
        _idft_rows_kernel, name="hyena_idft_rows", grid=(B, p2 // tj),
        in_specs=[pl.BlockSpec((p1h * tj, 4 * p1h * tj), lambda b, j: (0, 0)),
                  pl.BlockSpec((1, 2, 2 * p1h, tj, c), lambda b, j: (b, 0, 0, j, 0)),
                  row(), row(), pl.BlockSpec((1, c), lambda b, j: (0, 0))],
        out_specs=row(), out_shape=jax.ShapeDtypeStruct(u.shape, F32),
        compiler_params=_params("parallel", "parallel"),
    )(consts["m_inv"], bm, u, gate, bias.reshape(1, c))


def _short_conv_kernel(x_ref, p_ref, n_ref, w_ref, b_ref, v_ref, x1_ref, x2_ref):
    i = pl.program_id(1)
    x = x_ref[0]
    t = x.shape[0]
    row = lax.broadcasted_iota(jnp.int32, x.shape, 0)
    prev_row = jnp.where(i == 0, 0.0, p_ref[0, 7:8, :])
    next_row = jnp.where(i == pl.num_programs(1) - 1, 0.0, n_ref[0, 0:1, :])
    before = jnp.where(row == 0, prev_row, pltpu.roll(x, 1, axis=0))
    after = jnp.where(row == t - 1, next_row, pltpu.roll(x, t - 1, axis=0))
    w = w_ref[...]
    u = before * w[0:1] + x * w[1:2] + after * w[2:3] + b_ref[...]
    v_ref[0] = u[:, 0:HY_CH]
    x1_ref[0] = u[:, HY_CH:2 * HY_CH]
    x2_ref[0] = u[:, 2 * HY_CH:3 * HY_CH]


def _short_conv(cols, conv_w, conv_b, t=512):
    B, S, W = cols.shape
    t = min(t, S)
    r = t // 8
    out = pl.BlockSpec((1, t, HY_CH), lambda b, i: (b, i, 0))
    return pl.pallas_call(
        _short_conv_kernel, name="hyena_short_conv", grid=(B, S // t),
        in_specs=[pl.BlockSpec((1, t, W), lambda b, i: (b, i, 0)),
                  pl.BlockSpec((1, 8, W), lambda b, i: (b, jnp.maximum(i * r - 1, 0), 0)),
                  pl.BlockSpec((1, 8, W), lambda b, i: (b, jnp.minimum((i + 1) * r, S // 8 - 1), 0)),
                  pl.BlockSpec((HY_SHORT, W), lambda b, i: (0, 0)), pl.BlockSpec((1, W), lambda b, i: (0, 0))],
        out_specs=[out, out, out], out_shape=[jax.ShapeDtypeStruct((B, S, HY_CH), F32)] * 3,
        compiler_params=_params("parallel", "parallel"))(cols, cols, cols, conv_w, conv_b.reshape(1, W))


def _hyena_mixer(consts, cols, conv_w, conv_b, spectra, bias, layer):
    B, S, _ = cols.shape
    P2, C = DFT_P2, HY_CH
    P1 = 2 * S // P2
    v, x1, x2 = _short_conv(cols, conv_w, conv_b)
    rows = lambda a: a.reshape(B, P1 // 2, P2, C)
    z = rows(v)
    for order, gate in enumerate((x1, x2)):
        bm = _spectral_filter(consts, _dft_rows(consts, z), spectra[order], layer)
        z = _idft_rows_gated(consts, bm, z, rows(gate), bias[order])
    return z.reshape(B, S, C)


def _route(lt):
    t = lt.shape[1]
    row = lax.broadcasted_iota(jnp.int32, (8, t), 0)
    lg = jnp.where(row < N_GROUPS, lt[0:8], NEG_INF)
    mg = jnp.max(lg, axis=0, keepdims=True)
    p_top = 1.0 / jnp.sum(jnp.exp(lg - mg), axis=0, keepdims=True)
    g_sel = jnp.min(jnp.where(lg == mg, row, 8), axis=0, keepdims=True)
    le = jnp.zeros((8, t), F32)
    for g in range(N_GROUPS):
        le = jnp.where(g_sel == g, lt[8 + 8 * g:16 + 8 * g], le)
    v1 = jnp.max(le, axis=0, keepdims=True)
    i1 = jnp.min(jnp.where(le == v1, row, 8), axis=0, keepdims=True)
    rest = jnp.where(row == i1, NEG_INF, le)
    v2 = jnp.max(rest, axis=0, keepdims=True)
    i2 = jnp.min(jnp.where(rest == v2, row, 8), axis=0, keepdims=True)
    e = jnp.exp(v2 - v1)
    w1 = 1.0 / (1.0 + e)
    base = g_sel * EXPERTS_PER_GROUP
    idx = jnp.where(row == 0, base + i1, jnp.where(row == 1, base + i2, 0))
    gate = jnp.where(row == 0, p_top * w1, jnp.where(row == 1, p_top * (e * w1), 0.0))
    return idx, gate


def _post_kernel(x_ref, g1_ref, sc_ref, sh_ref, ln_ref, a_ref, b_ref, c_ref, d_ref, wo_ref, wr_ref, br_ref,
                 xo_ref, h_ref, idx_ref, gate_ref):
    wo = wo_ref[0]
    y = (_tn(a_ref[0].astype(BF16), wo[0:256]) + _tn(b_ref[0].astype(BF16), wo[256:512])
         + _tn(c_ref[0].astype(BF16), wo[512:768])
         + jnp.dot(d_ref[0].astype(BF16), wo[768:1024], preferred_element_type=F32))
    x = x_ref[0] + g1_ref[0] * y
    xo_ref[0] = x
    h = _rms(x) * ln_ref[...] * (1.0 + sc_ref[0]) + sh_ref[0]
    h_ref[0] = h
    lt = lax.dot_general(wr_ref[0], h, (((1,), (1,)), ((), ())), preferred_element_type=F32, precision=HIGHEST) + br_ref[0]
    idx, gate = _route(lt)
    idx_ref[...] = idx
    gate_ref[...] = gate


def _post_mixer(x, g1, sc2, sh2, ln_g, layer, mix_a, mix_b, mix_c, mix_d, w_out, w_router, b_router, t=512):
    B, S, D = x.shape
    t = min(t, S)
    nt = S // t
    nat = lambda n: pl.BlockSpec((1, t, n), lambda b, i: (b, i, 0))
    fm = pl.BlockSpec((1, 256, t), lambda b, i: (b, 0, i))
    mod = pl.BlockSpec((1, 1, D), lambda b, i: (b, 0, 0))
    lw = lambda a: pl.BlockSpec((1,) + a.shape[1:], lambda b, i: (layer,) + (0,) * (a.ndim - 1))
    tok = pl.BlockSpec((8, t), lambda b, i: (0, b * nt + i))
    return pl.pallas_call(
        _post_kernel, name="mixer_output_router", grid=(B, nt),
        in_specs=[nat(D), mod, mod, mod, pl.BlockSpec((1, D), lambda b, i: (0, 0)), fm, fm, fm, nat(256),
                  lw(w_out), lw(w_router), lw(b_router)],
        out_specs=[nat(D), nat(D), tok, tok],
        out_shape=[jax.ShapeDtypeStruct((B, S, D), F32), jax.ShapeDtypeStruct((B, S, D), F32),
                   jax.ShapeDtypeStruct((8, B * S), jnp.int32), jax.ShapeDtypeStruct((8, B * S), F32)],
        compiler_params=_params("parallel", "parallel"),
    )(x, g1, sc2, sh2, ln_g[layer].reshape(1, D), mix_a, mix_b, mix_c, mix_d, w_out, w_router, b_router)


def _start_row_gather(src_hbm, dst_ref, sem, row_of, priorities=(0, 1)):
    n = dst_ref.shape[0]
    for i in range(n):
        r = (i % 8) * (n // 8) + i // 8
        pltpu.make_async_copy(src_hbm.at[pl.ds(row_of(r), 1)], dst_ref.at[pl.ds(r, 1)], sem).start(
            priority=priorities[i % len(priorities)])


def _wait_row_gather(src_hbm, dst_ref, sem):
    pltpu.make_async_copy(src_hbm.at[pl.ds(0, dst_ref.shape[0])], dst_ref, sem).wait()


def _expert_kernel(be_ref, tok_ref, h_hbm, wg_ref, wu_ref, wd_ref, o_ref, x_buf, sem):
    i = pl.program_id(0)
    last = pl.num_programs(0) - 1
    slot = i % 2

    @pl.when(i == 0)
    def _():
        _start_row_gather(h_hbm, x_buf.at[0], sem.at[0], lambda r: tok_ref[r])

    nxt = jnp.minimum(i + 1, last) * MOE_ROWS
    _start_row_gather(h_hbm, x_buf.at[1 - slot], sem.at[1 - slot], lambda r: tok_ref[nxt + r])
    _wait_row_gather(h_hbm, x_buf.at[slot], sem.at[slot])

    @pl.when(i == last)
    def _():
        _wait_row_gather(h_hbm, x_buf.at[1 - slot], sem.at[1 - slot])

    x = x_buf[slot].astype(BF16)
    g = jnp.dot(x, wg_ref[0, 0].astype(BF16), preferred_element_type=F32)
    u = jnp.dot(x, wu_ref[0, 0].astype(BF16), preferred_element_type=F32)
    a = (g * jax.nn.sigmoid(g) * u).astype(BF16)
    o_ref[...] = jnp.dot(a, wd_ref[0, 0].astype(BF16), preferred_element_type=F32)


def _expert_ffn(h, slot_tok, block_expert, w_gate, w_up, w_down, layer):
    D = h.shape[1]
    P = slot_tok.shape[0]
    wmap = lambda i, be, tok: (layer, be[i], 0, 0)
    return pl.pallas_call(
        _expert_kernel, name="expert_ffn",
        grid_spec=pltpu.PrefetchScalarGridSpec(
            num_scalar_prefetch=2, grid=(P // MOE_ROWS,),
            in_specs=[pl.BlockSpec(memory_space=pl.ANY),
                      pl.BlockSpec((1, 1, D, D_EXPERT), wmap), pl.BlockSpec((1, 1, D, D_EXPERT), wmap),
                      pl.BlockSpec((1, 1, D_EXPERT, D), wmap)],
            out_specs=pl.BlockSpec((MOE_ROWS, D), lambda i, be, tok: (i, 0)),
            scratch_shapes=[pltpu.VMEM((2, MOE_ROWS, D), F32), pltpu.SemaphoreType.DMA((2,))]),
        out_shape=jax.ShapeDtypeStruct((P, D), F32),
        compiler_params=_params("arbitrary"))(block_expert, slot_tok, h, w_gate, w_up, w_down)


def _dispatch_plan(idx, T):
    e = idx[0:TOP_K]
    onehot = (e.reshape(-1)[:, None] == jnp.arange(N_EXPERTS)[None]).astype(jnp.int32)
    before = jnp.cumsum(onehot, axis=0) - onehot
    rank = jnp.sum(before * onehot, axis=1)
    counts = jnp.sum(onehot, axis=0)
    padded = (counts + MOE_ROWS - 1) // MOE_ROWS * MOE_ROWS
    pad_end = jnp.cumsum(padded)
    dest = ((pad_end - padded)[e.reshape(-1)] + rank).reshape(TOP_K, T)
    n_blocks = (TOP_K * T + N_EXPERTS * (MOE_ROWS - 1) + MOE_ROWS - 1) // MOE_ROWS
    tok = jnp.tile(jnp.arange(T, dtype=jnp.int32), TOP_K)
    slot_tok = (jnp.arange(n_blocks * MOE_ROWS, dtype=jnp.int32) % T).at[dest.reshape(-1)].set(tok)
    starts = jnp.arange(n_blocks, dtype=jnp.int32) * MOE_ROWS
    block_expert = jnp.minimum(jnp.sum((pad_end[None, :] <= starts[:, None]).astype(jnp.int32), axis=1), N_EXPERTS - 1)
    return dest, slot_tok, block_expert.astype(jnp.int32)


def _combine_kernel(dest_ref, x_ref, g2_ref, gate_ref, fg_ref, y_hbm, o_ref, y_buf, sem, *, final, n_tokens):
    t = x_ref.shape[1]
    step = pl.program_id(0) * pl.num_programs(1) + pl.program_id(1)
    last = pl.num_programs(0) * pl.num_programs(1) - 1
    slot = step % 2

    def start(tile, s):
        for k in range(TOP_K):
            _start_row_gather(y_hbm, y_buf.at[s, k], sem.at[s, k], lambda r, k=k: dest_ref[k * n_tokens + tile * t + r])

    def wait(s):
        for k in range(TOP_K):
            _wait_row_gather(y_hbm, y_buf.at[s, k], sem.at[s, k])

    @pl.when(step == 0)
    def _():
        start(0, 0)

    start(jnp.minimum(step + 1, last), 1 - slot)
    wait(slot)

    @pl.when(step == last)
    def _():
        wait(1 - slot)

    gt = gate_ref[...]
    x = x_ref[0] + g2_ref[0] * (gt[:, 0:1] * y_buf[slot, 0] + gt[:, 1:2] * y_buf[slot, 1])
    if final:
        x = _rms(x) * fg_ref[...]
    o_ref[0] = x


def _combine(x, g2, yb, dest, gate_t, final_g, final, t=256):
    B, S, D = x.shape
    t = min(t, S)
    nt = S // t
    nat = pl.BlockSpec((1, t, D), lambda b, i, d: (b, i, 0))
    return pl.pallas_call(
        functools.partial(_combine_kernel, final=final, n_tokens=B * S), name="moe_combine",
        grid_spec=pltpu.PrefetchScalarGridSpec(
            num_scalar_prefetch=1, grid=(B, nt),
            in_specs=[nat, pl.BlockSpec((1, 1, D), lambda b, i, d: (b, 0, 0)),
                      pl.BlockSpec((t, 8), lambda b, i, d: (b * nt + i, 0)),
                      pl.BlockSpec((1, D), lambda b, i, d: (0, 0)),
                      pl.BlockSpec(memory_space=pl.ANY)],
            out_specs=nat,
            scratch_shapes=[pltpu.VMEM((2, TOP_K, t, D), F32), pltpu.SemaphoreType.DMA((2, TOP_K))]),
        out_shape=jax.ShapeDtypeStruct((B, S, D), F32),
        compiler_params=_params("arbitrary", "arbitrary"))(dest.reshape(-1), x, g2, gate_t, final_g.reshape(1, D), yb)


def _moe(x, h, idx, gate, g2, w_gate, w_up, w_down, layer, final_g, final):
    B, S, D = x.shape
    T = B * S
    dest, slot_tok, block_expert = _dispatch_plan(idx, T)
    yb = _expert_ffn(h.reshape(T, D), slot_tok, block_expert, w_gate, w_up, w_down, layer)
    return _combine(x, g2, yb, dest, gate.T, final_g, final)


def kernel(x, c, positions, ln1_g, ln2_g, ada_w, ada_b, w_in, mla_q_norm, mla_kv_norm, mla_w_uq, mla_w_ukv, swa_sink, diff_lambda, diff_norm_g, hy_conv_w, hy_conv_b, hy_w1, hy_b1, hy_w2, hy_b2, hy_w3, hy_b3, hy_w4, hy_sin_freq, hy_bias, w_out, router_g_w, router_g_b, router_e_w, router_e_b, moe_w_gate, moe_w_up, moe_w_down, final_g):
    B, S, D = x.shape
    L = w_in.shape[0]
    mod = _ada_modulation(c, ada_w, ada_b)[:, :B].reshape(L, B, 6, 1, D)
    tab_fm = _rope_tables(positions)
    tab_nat = _natural_rope_table(tab_fm)
    weights = _layer_weight_layouts(w_in, mla_q_norm, mla_kv_norm, mla_w_uq, mla_w_ukv)
    w_out_b = w_out.astype(BF16)
    w_router = jnp.concatenate([router_g_w.transpose(0, 2, 1), jnp.zeros((L, 8 - N_GROUPS, D), F32),
                                router_e_w.transpose(0, 2, 1)], axis=1)
    b_router = jnp.concatenate([router_g_b, jnp.zeros((L, 8 - N_GROUPS), F32), router_e_b], axis=1)[:, :, None]
    consts = _hyena_constants(S)
    filters, stats = _hyena_filters(consts, hy_w1, hy_b1, hy_w2, hy_b2, hy_w3, hy_b3, hy_w4, hy_sin_freq, S)
    spectra = _filter_spectra(consts, filters, stats, S)
    for l in range(L):
        sh1, sc1, g1, sh2, sc2, g2 = (mod[l, :, i] for i in range(6))
        kmla, qn, qp, vmla, kswa, qswa, vswa, kdiff, qdiff, vdiff, hy = _project(
            x, sc1, sh1, ln1_g, l, weights, tab_nat, tab_fm)
        mix_a = _mla_attention(kmla, qn, qp, vmla)
        mix_b = _swa_attention(kswa, qswa, vswa, swa_sink[l])
        lambda_init = 0.8 - 0.6 * math.exp(-0.3 * l)
        mix_c = _diff_attention(kdiff, qdiff, vdiff, diff_lambda[l], diff_norm_g[l], lambda_init)
        mix_d = _hyena_mixer(consts, hy, hy_conv_w[l], hy_conv_b[l], spectra, hy_bias[l], l)
        x, h, idx, gate = _post_mixer(x, g1, sc2, sh2, ln2_g, l, mix_a, mix_b, mix_c, mix_d, w_out_b, w_router, b_router)
        x = _moe(x, h, idx, gate, g2, moe_w_gate, moe_w_up, moe_w_down, l, final_g, final=(l == L - 1))
    return x
```

```python
import functools
import math

import numpy as np
import jax
import jax.numpy as jnp
from jax import lax
from jax.experimental import pallas as pl
from jax.experimental.pallas import tpu as pltpu

F32 = jnp.float32
BF16 = jnp.bfloat16
HIGHEST = lax.Precision.HIGHEST

HEAD_DIM = 64
ROPE_THETA = 10000.0
EPS = 1e-6
NEG_INF = -1e30
LOG2E = 1.4426950408889634
MLA_HEADS, MLA_Q_RANK, MLA_KV_RANK, MLA_NOPE, MLA_ROPE, MLA_V = 4, 256, 128, 64, 32, 64
SWA_HEADS, SWA_KV_HEADS, SWA_WINDOW = 4, 2, 128
DIFF_HEADS, DIFF_QK, DIFF_V = 4, 32, 64
HY_CH, HY_ORDER, HY_DIRS, HY_SHORT, HY_EMB, HY_FFN = 256, 2, 2, 3, 33, 64
HY_BANDS = (HY_EMB - 1) // 2
HY_MIN_DECAY = math.log(1e-2) / 1.5
HY_MAX_DECAY = math.log(1e-2) / 0.3
MLA_COLS = MLA_Q_RANK + MLA_KV_RANK + MLA_ROPE
SWA_COLS = (SWA_HEADS + 2 * SWA_KV_HEADS) * HEAD_DIM
DIFF_COLS = DIFF_HEADS * (4 * DIFF_QK + DIFF_V)
HY_COLS = (HY_ORDER + 1) * HY_CH
N_GROUPS, EXPERTS_PER_GROUP, TOP_K, D_EXPERT = 4, 8, 2, 256
N_EXPERTS = N_GROUPS * EXPERTS_PER_GROUP

V7X_VMEM_LIMIT_BYTES = 56 * 1024 * 1024
LANES = 128
DFT_P2 = 128
DFT_COLS_PER_STEP = 8
MOE_ROWS = 512


def _params(*sem):
    return pltpu.CompilerParams(dimension_semantics=sem, vmem_limit_bytes=V7X_VMEM_LIMIT_BYTES)


def _nt(a, b):
    return lax.dot_general(a, b, (((1,), (1,)), ((), ())), preferred_element_type=F32)


def _tn(a, b):
    return lax.dot_general(a, b, (((0,), (0,)), ((), ())), preferred_element_type=F32)


def _ada_kernel(c_ref, w_ref, b_ref, o_ref):
    c = c_ref[...]
    act = c * jax.nn.sigmoid(c)
    o_ref[0] = jnp.dot(act, w_ref[0], preferred_element_type=F32, precision=HIGHEST) + b_ref[0]


def _ada_modulation(c, ada_w, ada_b):
    L, D, N = ada_w.shape
    B = c.shape[0]
    c8 = jnp.zeros((8, D), F32).at[:B].set(c)
    tn = 1024
    return pl.pallas_call(
        _ada_kernel, name="ada_modulation", grid=(L, N // tn),
        in_specs=[pl.BlockSpec((8, D), lambda l, j: (0, 0)),
                  pl.BlockSpec((1, D, tn), lambda l, j: (l, 0, j)),
                  pl.BlockSpec((1, 1, tn), lambda l, j: (l, 0, j))],
        out_specs=pl.BlockSpec((1, 8, tn), lambda l, j: (l, 0, j)),
        out_shape=jax.ShapeDtypeStruct((L, 8, N), F32),
        compiler_params=_params("parallel", "parallel"))(c8, ada_w, ada_b.reshape(L, 1, N))


def _rope_table_kernel(p_ref, o_ref):
    pos = p_ref[0].astype(F32)
    for half, base in ((32, 0), (16, 128)):
        j = lax.broadcasted_iota(jnp.int32, (half, 1), 0).astype(F32)
        inv = jnp.exp(-math.log(ROPE_THETA) * j / half)
        ang = inv * pos
        cs, sn = jnp.cos(ang), jnp.sin(ang)
        o_ref[0, base:base + half] = cs
        o_ref[0, base + half:base + 2 * half] = cs
        o_ref[0, base + 2 * half:base + 3 * half] = -sn
        o_ref[0, base + 3 * half:base + 4 * half] = sn


def _rope_tables(positions):
    B, S = positions.shape
    return pl.pallas_call(
        _rope_table_kernel, name="rope_tables", grid=(B,),
        in_specs=[pl.BlockSpec((1, 1, S), lambda b: (b, 0, 0))],
        out_specs=pl.BlockSpec((1, 192, S), lambda b: (b, 0, 0)),
        out_shape=jax.ShapeDtypeStruct((B, 192, S), F32),
        compiler_params=_params("parallel"))(positions.reshape(B, 1, S))


def _partner(n_heads, dim):
    r = np.arange(n_heads * dim)
    return (r // dim) * dim + (r % dim + dim // 2) % dim


def _layer_weight_layouts(w_in, mla_q_norm, mla_kv_norm, mla_w_uq, mla_w_ukv):
    L, D, _ = w_in.shape
    oB = MLA_COLS
    oC = oB + SWA_COLS
    oD = oC + DIFF_COLS
    Z = w_in.shape[2]
    wz = jnp.concatenate([w_in, jnp.zeros((L, D, 1), F32)], axis=2)

    kpe = MLA_Q_RANK + MLA_KV_RANK + np.arange(MLA_ROPE)
    pad96 = np.full(LANES - MLA_ROPE, Z)
    swa_k = oB + SWA_HEADS * HEAD_DIM + np.arange(SWA_KV_HEADS * HEAD_DIM)
    h, j, r = np.meshgrid(np.arange(DIFF_HEADS), np.arange(2), np.arange(DIFF_QK), indexing="ij")
    diff_src = (h * 2 * DIFF_QK + j * DIFF_QK + r).transpose(1, 0, 2).reshape(-1)
    diff_q = oC + diff_src
    diff_k = oC + DIFF_HEADS * 2 * DIFF_QK + diff_src
    nat_idx = np.concatenate([
        np.arange(MLA_Q_RANK + MLA_KV_RANK),
        kpe, pad96, kpe[_partner(1, MLA_ROPE)], pad96,
        swa_k, swa_k[_partner(SWA_KV_HEADS, HEAD_DIM)],
        diff_k, diff_k[_partner(2 * DIFF_HEADS, DIFF_QK)],
        oD + np.arange(HY_COLS)])
    w_nat = jnp.take(wz, jnp.asarray(nat_idx), axis=2).astype(BF16)

    swa_q = oB + np.arange(SWA_HEADS * HEAD_DIM)
    swa_v = oB + (SWA_HEADS + SWA_KV_HEADS) * HEAD_DIM + np.arange(SWA_KV_HEADS * HEAD_DIM)
    diff_v = oC + DIFF_HEADS * 4 * DIFF_QK + np.arange(DIFF_HEADS * DIFF_V)
    fm_idx = np.concatenate([
        swa_q, swa_q[_partner(SWA_HEADS, HEAD_DIM)], swa_v,
        diff_q, diff_q[_partner(2 * DIFF_HEADS, DIFF_QK)], diff_v])
    s_swa = HEAD_DIM ** -0.5 * LOG2E
    s_diff = DIFF_QK ** -0.5 * LOG2E
    fm_scale = np.concatenate([np.full(512, s_swa), np.ones(128), np.full(512, s_diff), np.ones(256)]).astype(np.float32)
    w_fm = (jnp.take(wz, jnp.asarray(fm_idx), axis=2) * fm_scale).transpose(0, 2, 1).astype(BF16)

    s_mla = (MLA_NOPE + MLA_ROPE) ** -0.5 * LOG2E
    uq = mla_w_uq * mla_q_norm[:, :, None] * s_mla
    hq = np.arange(MLA_HEADS)[:, None] * (MLA_NOPE + MLA_ROPE)
    qn_idx = (hq + np.arange(MLA_NOPE)[None]).reshape(-1)
    qp_idx = (hq + MLA_NOPE + np.arange(MLA_ROPE)[None]).reshape(-1)
    w_qn = jnp.take(uq, jnp.asarray(qn_idx), axis=2).transpose(0, 2, 1).astype(BF16)
    w_qpa = jnp.take(uq, jnp.asarray(qp_idx), axis=2).transpose(0, 2, 1).astype(BF16)
    w_qpb = jnp.take(uq, jnp.asarray(qp_idx[_partner(MLA_HEADS, MLA_ROPE)]), axis=2).transpose(0, 2, 1).astype(BF16)
    ukv = mla_w_ukv * mla_kv_norm[:, :, None]
    hk = np.arange(MLA_HEADS)[:, None] * (MLA_NOPE + MLA_V)
    kn_idx = (hk + np.arange(MLA_NOPE)[None]).reshape(-1)
    v_idx = (hk + MLA_NOPE + np.arange(MLA_V)[None]).reshape(-1)
    w_kn = jnp.take(ukv, jnp.asarray(kn_idx), axis=2).astype(BF16)
    w_v = jnp.take(ukv, jnp.asarray(v_idx), axis=2).transpose(0, 2, 1).astype(BF16)
    return w_nat, w_fm, w_qn, w_qpa, w_qpb, w_kn, w_v


def _natural_rope_table(tab_fm):
    t = tab_fm.transpose(0, 2, 1)
    cos64, sin64, cos32, sin32 = t[..., 0:64], t[..., 64:128], t[..., 128:160], t[..., 160:192]
    z = jnp.zeros(t.shape[:2] + (LANES - MLA_ROPE,), F32)
    return jnp.concatenate([cos32, z, sin32, z,
                            jnp.tile(cos64, (1, 1, 2)), jnp.tile(sin64, (1, 1, 2)),
                            jnp.tile(cos32, (1, 1, 8)), jnp.tile(sin32, (1, 1, 8))], axis=-1)


def _rms(x):
    return x * lax.rsqrt(jnp.mean(x * x, axis=-1, keepdims=True) + EPS)


def _proj_kernel(x_ref, sc_ref, sh_ref, g_ref, wn_ref, wf_ref, wqn_ref, wqpa_ref, wqpb_ref, wkn_ref, wv_ref,
                 tn_ref, tf_ref,
                 kmla_ref, qn_ref, qp_ref, vmla_ref, kswa_ref, qswa_ref, vswa_ref, kdiff_ref, qdiff_ref, vdiff_ref, hy_ref):
    x = x_ref[0]
    t = x.shape[0]
    h = _rms(x) * g_ref[...] * (1.0 + sc_ref[0]) + sh_ref[0]
    hb = h.astype(BF16)
    cn = jnp.dot(hb, wn_ref[0], preferred_element_type=F32)
    tab = tn_ref[0]
    cqn = _rms(cn[:, 0:256]).astype(BF16)
    ckvn = _rms(cn[:, 256:384]).astype(BF16)
    kpe = (cn[:, 384:512] * tab[:, 0:128] + cn[:, 512:640] * tab[:, 128:256]).astype(BF16)
    kswa_ref[0] = (cn[:, 640:768] * tab[:, 256:384] + cn[:, 768:896] * tab[:, 384:512]).astype(BF16)
    kdiff_ref[0] = (cn[:, 896:1152] * tab[:, 512:768] + cn[:, 1152:1408] * tab[:, 768:1024]).astype(BF16)
    hy_ref[0] = cn[:, 1408:2176]
    kn = jnp.dot(ckvn, wkn_ref[0], preferred_element_type=F32).astype(BF16)
    kmla_ref[0, 0] = jnp.concatenate([kn[:, 0:128], kpe], axis=1)
    kmla_ref[0, 1] = jnp.concatenate([kn[:, 128:256], kpe], axis=1)

    ft = _nt(wf_ref[0], hb)
    tf = tf_ref[0]
    cos64, sin64, cos32, sin32 = tf[0:64], tf[64:128], tf[128:160], tf[160:192]
    qswa = ft[0:256].reshape(4, 64, t) * cos64[None] + ft[256:512].reshape(4, 64, t) * sin64[None]
    qswa_ref[0] = qswa.reshape(256, t).astype(BF16)
    vswa_ref[0] = _with_ones_rows(ft[512:640], t)
    qdiff = ft[640:896].reshape(8, 32, t) * cos32[None] + ft[896:1152].reshape(8, 32, t) * sin32[None]
    qdiff_ref[0] = qdiff.reshape(256, t).astype(BF16)
    vdiff_ref[0] = _with_ones_rows(ft[1152:1408], t)
    qn_ref[0] = _nt(wqn_ref[0], cqn).astype(BF16)
    qp = _nt(wqpa_ref[0], cqn).reshape(4, 32, t) * cos32[None] + _nt(wqpb_ref[0], cqn).reshape(4, 32, t) * sin32[None]
    qp_ref[0] = qp.reshape(128, t).astype(BF16)
    vmla_ref[0] = _with_ones_rows(_nt(wv_ref[0], ckvn), t)


def _project(x, sc1, sh1, ln_g, layer, weights, tab_nat, tab_fm, t=512):
    B, S, D = x.shape
    w_nat, w_fm, w_qn, w_qpa, w_qpb, w_kn, w_v = weights
    wspec = lambda a: pl.BlockSpec((1,) + a.shape[1:], lambda b, i: (layer,) + (0,) * (a.ndim - 1))
    nat = lambda n: pl.BlockSpec((1, t, n), lambda b, i: (b, i, 0))
    fm = lambda n: pl.BlockSpec((1, n, t), lambda b, i: (b, 0, i))
    outs = [
        (jax.ShapeDtypeStruct((B, 2, S, 256), BF16), pl.BlockSpec((1, 2, t, 256), lambda b, i: (b, 0, i, 0))),
        (jax.ShapeDtypeStruct((B, 256, S), BF16), fm(256)),
        (jax.ShapeDtypeStruct((B, 128, S), BF16), fm(128)),
        (jax.ShapeDtypeStruct((B, 4 * V_ROWS, S), BF16), fm(4 * V_ROWS)),
        (jax.ShapeDtypeStruct((B, S, 128), BF16), nat(128)),
        (jax.ShapeDtypeStruct((B, 256, S), BF16), fm(256)),
        (jax.ShapeDtypeStruct((B, 2 * V_ROWS, S), BF16), fm(2 * V_ROWS)),
        (jax.ShapeDtypeStruct((B, S, 256), BF16), nat(256)),
        (jax.ShapeDtypeStruct((B, 256, S), BF16), fm(256)),
        (jax.ShapeDtypeStruct((B, 4 * V_ROWS, S), BF16), fm(4 * V_ROWS)),
        (jax.ShapeDtypeStruct((B, S, HY_COLS), F32), nat(HY_COLS)),
    ]
    return pl.pallas_call(
        _proj_kernel, name="mixer_projection", grid=(B, S // t),
        in_specs=[nat(D),
                  pl.BlockSpec((1, 1, D), lambda b, i: (b, 0, 0)),
                  pl.BlockSpec((1, 1, D), lambda b, i: (b, 0, 0)),
                  pl.BlockSpec((1, D), lambda b, i: (0, 0)),
                  wspec(w_nat), wspec(w_fm), wspec(w_qn), wspec(w_qpa), wspec(w_qpb), wspec(w_kn), wspec(w_v),
                  nat(1024), fm(192)],
        out_specs=[o[1] for o in outs], out_shape=[o[0] for o in outs],
        compiler_params=_params("parallel", "parallel"),
    )(x, sc1, sh1, ln_g[layer].reshape(1, D), w_nat, w_fm, w_qn, w_qpa, w_qpb, w_kn, w_v, tab_nat, tab_fm)


V_ROWS = 80


def _with_ones_rows(v, t):
    tail = (lax.broadcasted_iota(jnp.int32, (V_ROWS - 64, t), 0) == 0).astype(F32)
    parts = []
    for h in range(v.shape[0] // 64):
        parts += [v[h * 64:(h + 1) * 64], tail]
    return jnp.concatenate(parts, axis=0).astype(BF16)


def _flash_scratch(tk, tq):
    return ([pltpu.VMEM((256, tq), BF16)] + [pltpu.VMEM((tk, tq), F32)] * 4 + [pltpu.VMEM((tk, tq), BF16)] * 4
            + [pltpu.VMEM((1, tq), F32)] * 8 + [pltpu.VMEM((V_ROWS, tq), F32), pltpu.VMEM((1, tq), F32)])


def _flash_sweep(k_at, v_at, q_pad, n_blocks, scratch):
    s, p, al, top = scratch[0:4], scratch[4:8], scratch[8:12], scratch[12:16]
    acc_ref, m_ref = scratch[16], scratch[17]

    def scores(i, b):
        x = jnp.dot(k_at(i), q_pad, preferred_element_type=F32)
        s[b][...] = x
        top[b][...] = jnp.max(x, axis=0, keepdims=True)

    def softmax(src, dst):
        m_old = m_ref[...]
        m_new = jnp.maximum(m_old, top[src][...])
        al[dst][...] = jnp.exp2(m_old - m_new)
        p[dst][...] = jnp.exp2(s[src][...] - m_new).astype(BF16)
        m_ref[...] = m_new

    def values(i, b):
        acc_ref[...] = al[b][...] * acc_ref[...] + jnp.dot(v_at(i), p[b][...], preferred_element_type=F32)

    m_ref[...] = jnp.full(m_ref.shape, NEG_INF, F32)
    acc_ref[...] = jnp.zeros(acc_ref.shape, F32)
    scores(0, 2)
    scores(1, 3)
    softmax(2, 0)
    softmax(3, 1)
    scores(2, 0)
    scores(3, 1)

    def body(j, carry):
        i = 4 * j
        scores(i + 4, 2), scores(i + 5, 3)
        softmax(0, 2), softmax(1, 3)
        values(i, 0), values(i + 1, 1)
        scores(i + 6, 0), scores(i + 7, 1)
        softmax(2, 0), softmax(3, 1)
        values(i + 2, 2), values(i + 3, 3)
        return carry

    lax.fori_loop(0, n_blocks // 4 - 1, body, 0)
    n = n_blocks
    softmax(0, 2), softmax(1, 3)
    values(n - 4, 0), values(n - 3, 1), values(n - 2, 2), values(n - 1, 3)
    acc = acc_ref[...]
    return acc[0:64] / acc[64:65]


def _mla_kernel(k_ref, qn_ref, qp_ref, v_ref, o_ref, qpad_ref, *scratch, tk):
    slot = pl.program_id(1) % 2
    S = k_ref.shape[2]
    tq = qn_ref.shape[2]
    qn = qn_ref[0]
    zero = jnp.zeros_like(qn)
    qpad_ref[0:64, :] = jnp.where(slot == 0, qn, zero)
    qpad_ref[64:128, :] = jnp.where(slot == 1, qn, zero)
    qpad_ref[128:160, :] = qp_ref[0]
    qpad_ref[160:256, :] = jnp.zeros((96, tq), BF16)
    o_ref[0] = _flash_sweep(lambda i: k_ref[0, 0, pl.ds(pl.multiple_of(i * tk, tk), tk), :],
                            lambda i: v_ref[0, :, pl.ds(pl.multiple_of(i * tk, tk), tk)],
                            qpad_ref[...], S // tk, scratch)


def _mla_attention(k, qn, qp, v, tq=1024, tk=512):
    B, _, S, _ = k.shape
    tq, tk = min(tq, S), min(tk, S // 4)
    return pl.pallas_call(
        functools.partial(_mla_kernel, tk=tk), name="mla_attention", grid=(B, MLA_HEADS, S // tq),
        in_specs=[pl.BlockSpec((1, 1, S, 256), lambda b, h, i: (b, h // 2, 0, 0)),
                  pl.BlockSpec((1, 64, tq), lambda b, h, i: (b, h, i)),
                  pl.BlockSpec((1, 32, tq), lambda b, h, i: (b, h, i)),
                  pl.BlockSpec((1, V_ROWS, S), lambda b, h, i: (b, h, 0))],
        out_specs=pl.BlockSpec((1, 64, tq), lambda b, h, i: (b, h, i)),
        out_shape=jax.ShapeDtypeStruct((B, 256, S), F32),
        scratch_shapes=_flash_scratch(tk, tq),
        compiler_params=_params("parallel", "parallel", "arbitrary"))(k, qn, qp, v)


def _diff_kernel(lam_ref, g_ref, k_ref, q1_ref, q2_ref, v_ref, o_ref, qpad_ref, *scratch, tk, lambda_init):
    head = pl.program_id(1)
    S = k_ref.shape[1]
    maps = []
    for j, q_ref in enumerate((q1_ref, q2_ref)):
        q = q_ref[0]
        qpad_ref[...] = jnp.zeros(qpad_ref.shape, BF16)
        for s in range(DIFF_HEADS):
            r0 = j * 128 + s * DIFF_QK
            qpad_ref[r0:r0 + DIFF_QK, :] = jnp.where(head == s, q, jnp.zeros_like(q))
        maps.append(_flash_sweep(lambda i: k_ref[0, pl.ds(pl.multiple_of(i * tk, tk), tk), :],
                                 lambda i: v_ref[0, :, pl.ds(pl.multiple_of(i * tk, tk), tk)],
                                 qpad_ref[...], S // tk, scratch))
    lv = lam_ref[...]
    lam = (jnp.exp(jnp.sum(lv[0:1] * lv[1:2], axis=1, keepdims=True))
           - jnp.exp(jnp.sum(lv[2:3] * lv[3:4], axis=1, keepdims=True)) + lambda_init)
    o = maps[0] - lam * maps[1]
    o = o * lax.rsqrt(jnp.mean(o * o, axis=0, keepdims=True) + EPS)
    o_ref[0] = o * g_ref[...] * (1.0 - lambda_init)


def _diff_attention(k, q, v, lam_vecs, norm_g, lambda_init, tq=1024, tk=512):
    B, S, _ = k.shape
    tq, tk = min(tq, S), min(tk, S // 4)
    return pl.pallas_call(
        functools.partial(_diff_kernel, tk=tk, lambda_init=lambda_init), name="diff_attention",
        grid=(B, DIFF_HEADS, S // tq),
        in_specs=[pl.BlockSpec((4, DIFF_QK), lambda b, h, i: (0, 0)),
                  pl.BlockSpec((DIFF_V, 1), lambda b, h, i: (0, 0)),
                  pl.BlockSpec((1, S, 256), lambda b, h, i: (b, 0, 0)),
                  pl.BlockSpec((1, 32, tq), lambda b, h, i: (b, h, i)),
                  pl.BlockSpec((1, 32, tq), lambda b, h, i: (b, DIFF_HEADS + h, i)),
                  pl.BlockSpec((1, V_ROWS, S), lambda b, h, i: (b, h, 0))],
        out_specs=pl.BlockSpec((1, 64, tq), lambda b, h, i: (b, h, i)),
        out_shape=jax.ShapeDtypeStruct((B, 256, S), F32),
        scratch_shapes=_flash_scratch(tk, tq),
        compiler_params=_params("parallel", "parallel", "arbitrary"))(lam_vecs, norm_g.reshape(DIFF_V, 1), k, q, q, v)


def _swa_kernel(sink_ref, q_ref, k_prev, k_main, k_next, v_prev, v_main, v_next, o_ref, *, S):
    qi = pl.program_id(1)
    tq = q_ref.shape[2]
    k = jnp.concatenate([k_prev[0], k_main[0], k_next[0]], axis=0)
    v = jnp.concatenate([v_prev[0], v_main[0], v_next[0]], axis=1)
    shape = (tq + 2 * SWA_WINDOW, tq)
    kpos = qi * tq - SWA_WINDOW + lax.broadcasted_iota(jnp.int32, shape, 0)
    qpos = qi * tq + lax.broadcasted_iota(jnp.int32, shape, 1)
    valid = (jnp.abs(kpos - qpos) <= SWA_WINDOW) & (kpos >= 0) & (kpos < S)
    for head in range(SWA_HEADS):
        kv = head // (SWA_HEADS // SWA_KV_HEADS)
        q = q_ref[0, head * HEAD_DIM:(head + 1) * HEAD_DIM, :]
        zero = jnp.zeros_like(q)
        q_pad = jnp.concatenate([q, zero] if kv == 0 else [zero, q], axis=0)
        s = jnp.where(valid, jnp.dot(k, q_pad, preferred_element_type=F32), NEG_INF)
        sink = sink_ref[head] * LOG2E
        m = jnp.maximum(jnp.max(s, axis=0, keepdims=True), sink)
        e = jnp.exp2(s - m).astype(BF16)
        acc = jnp.dot(v[kv * V_ROWS:(kv + 1) * V_ROWS], e, preferred_element_type=F32)
        o_ref[0, head * HEAD_DIM:(head + 1) * HEAD_DIM, :] = acc[0:HEAD_DIM] / (acc[HEAD_DIM:HEAD_DIM + 1] + jnp.exp2(sink - m))


def _swa_attention(k, q, v, sink, tq=512):
    B, S, _ = k.shape
    W = SWA_WINDOW
    tq = min(tq, S)
    r = tq // W
    nb = S // W
    prev = lambda i: jnp.maximum(i * r - 1, 0)
    nxt = lambda i: jnp.minimum((i + 1) * r, nb - 1)
    return pl.pallas_call(
        functools.partial(_swa_kernel, S=S), name="swa_attention", grid=(B, S // tq),
        in_specs=[pl.BlockSpec(memory_space=pltpu.SMEM),
                  pl.BlockSpec((1, 256, tq), lambda b, i: (b, 0, i)),
                  pl.BlockSpec((1, W, 128), lambda b, i: (b, prev(i), 0)),
                  pl.BlockSpec((1, tq, 128), lambda b, i: (b, i, 0)),
                  pl.BlockSpec((1, W, 128), lambda b, i: (b, nxt(i), 0)),
                  pl.BlockSpec((1, 2 * V_ROWS, W), lambda b, i: (b, 0, prev(i))),
                  pl.BlockSpec((1, 2 * V_ROWS, tq), lambda b, i: (b, 0, i)),
                  pl.BlockSpec((1, 2 * V_ROWS, W), lambda b, i: (b, 0, nxt(i)))],
        out_specs=pl.BlockSpec((1, 256, tq), lambda b, i: (b, 0, i)),
        out_shape=jax.ShapeDtypeStruct((B, 256, S), F32),
        compiler_params=_params("parallel", "parallel"))(sink, q, k, k, k, v, v, v)


def _hyena_constants(S):
    N = 2 * S
    P2 = DFT_P2
    P1 = N // P2
    t = np.linspace(0.0, 1.0, S)
    w = 2.0 * math.pi * np.arange(S) / S
    f = np.linspace(1e-4, HY_BANDS - 1, HY_BANDS)
    z = np.concatenate([t[:, None], np.cos(w[:, None] * f), -np.sin(w[:, None] * f), np.zeros((S, 40 - HY_EMB))], axis=1)
    decay = np.linspace(HY_MIN_DECAY, HY_MAX_DECAY, HY_CH)
    window = np.exp(-t[:, None] * np.abs(decay))
    k1 = np.arange(P1)
    n1 = np.arange(P1 // 2)
    a1 = 2.0 * math.pi * np.outer(k1, n1) / P1
    m_fwd = np.concatenate([np.cos(a1), -np.sin(a1)], axis=0)
    m_inv = np.concatenate([np.cos(a1.T), -np.sin(a1.T)], axis=1)
    a2 = 2.0 * math.pi * np.outer(np.arange(P2), np.arange(P2)) / P2
    c2, s2 = np.cos(a2), np.sin(a2)
    g = np.block([[c2, s2], [-s2, c2]])
    at = 2.0 * math.pi * np.outer(k1, np.arange(P2)) / N
    eye = np.eye(DFT_COLS_PER_STEP)
    return dict(z=jnp.asarray(z, F32), window=jnp.asarray(window, F32),
                m_fwd=jnp.asarray(np.kron(m_fwd, eye), BF16), m_inv=jnp.asarray(np.kron(m_inv, eye), BF16),
                g=jnp.asarray(g, BF16), gt=jnp.asarray(g.T, BF16),
                tw_r=jnp.asarray(np.cos(at)[:, :, None], F32), tw_i=jnp.asarray(-np.sin(at)[:, :, None], F32))


def _filter_kernel(z_ref, win_ref, w1_ref, b1_ref, w2_ref, b2_ref, w3_ref, b3_ref, w4_ref, f_ref, o_ref, st_ref):
    dot = functools.partial(jnp.dot, preferred_element_type=F32, precision=HIGHEST)
    fr = f_ref[0]
    h = jnp.sin(fr[0:1] * (dot(z_ref[...], w1_ref[0]) + b1_ref[0]))
    h = jnp.sin(fr[1:2] * (dot(h, w2_ref[0]) + b2_ref[0]))
    h = jnp.sin(fr[2:3] * (dot(h, w3_ref[0]) + b3_ref[0]))
    out = dot(h, w4_ref[0])
    win = win_ref[...]
    parts = [out[:, q * HY_CH:(q + 1) * HY_CH] * win for q in range(4)]
    for q in range(4):
        o_ref[0, q] = parts[q]
    filt = jnp.concatenate(parts, axis=1)
    ssq = jnp.sum(filt * filt, axis=0, keepdims=True)

    @pl.when(pl.program_id(1) == 0)
    def _():
        cross = filt[0:1, 0:512] * filt[0:1, 512:1024]
        st_ref[0] = jnp.concatenate([ssq, jnp.concatenate([cross, jnp.zeros_like(cross)], axis=1),
                                     jnp.zeros((6, 1024), F32)], axis=0)

    @pl.when(pl.program_id(1) != 0)
    def _():
        st_ref[0, 0:1] = st_ref[0, 0:1] + ssq


def _hyena_filters(consts, hy_w1, hy_b1, hy_w2, hy_b2, hy_w3, hy_b3, hy_w4, hy_sin_freq, S):
    L = hy_w1.shape[0]
    ts = min(512, S)
    w1 = jnp.concatenate([hy_w1, jnp.zeros((L, 40 - HY_EMB, HY_FFN), F32)], axis=1)
    lw = lambda a: pl.BlockSpec((1,) + a.shape[1:], lambda l, i: (l,) + (0,) * (a.ndim - 1))
    b1, b2, b3 = (b.reshape(L, 1, HY_FFN) for b in (hy_b1, hy_b2, hy_b3))
    return pl.pallas_call(
        _filter_kernel, name="hyena_filters", grid=(L, S // ts),
        in_specs=[pl.BlockSpec((ts, 40), lambda l, i: (i, 0)), pl.BlockSpec((ts, HY_CH), lambda l, i: (i, 0)),
                  lw(w1), lw(b1), lw(hy_w2), lw(b2), lw(hy_w3), lw(b3), lw(hy_w4), lw(hy_sin_freq)],
        out_specs=[pl.BlockSpec((1, 4, ts, HY_CH), lambda l, i: (l, 0, i, 0)),
                   pl.BlockSpec((1, 8, 1024), lambda l, i: (l, 0, 0))],
        out_shape=[jax.ShapeDtypeStruct((L, 4, S, HY_CH), F32), jax.ShapeDtypeStruct((L, 8, 1024), F32)],
        compiler_params=_params("parallel", "arbitrary"),
    )(consts["z"], consts["window"], w1, b1, hy_w2, b2, hy_w3, b3, hy_w4, hy_sin_freq)


def _dft_rows_kernel(m_ref, x_ref, o_ref):
    x = x_ref[0]
    x = x.reshape(x.shape[0] * x.shape[1], x.shape[2]).astype(BF16)
    a = jnp.dot(m_ref[...], x, preferred_element_type=F32)
    o_ref[0] = a.reshape(o_ref.shape[1:]).astype(o_ref.dtype)


def _dft_rows(consts, x):
    nb, p1h, p2, c = x.shape
    tj = DFT_COLS_PER_STEP
    return pl.pallas_call(
        _dft_rows_kernel, name="hyena_dft_rows", grid=(nb, p2 // tj),
        in_specs=[pl.BlockSpec((4 * p1h * tj, p1h * tj), lambda b, j: (0, 0)),
                  pl.BlockSpec((1, p1h, tj, c), lambda b, j: (b, 0, j, 0))],
        out_specs=pl.BlockSpec((1, 2, 2 * p1h, tj, c), lambda b, j: (b, 0, 0, j, 0)),
        out_shape=jax.ShapeDtypeStruct((nb, 2, 2 * p1h, p2, c), BF16),
        compiler_params=_params("parallel", "parallel"))(consts["m_fwd"], x)


def _twiddled_spectrum(a_ref, kk, tr, ti, g):
    ar, ai = a_ref[0, 0, kk].astype(F32), a_ref[0, 1, kk].astype(F32)
    pre = jnp.concatenate([ar * tr - ai * ti, ar * ti + ai * tr], axis=0).astype(BF16)
    return jnp.dot(g, pre, preferred_element_type=F32)


def _filter_spectrum_kernel(g_ref, twr_ref, twi_ref, st_ref, af_ref, ab_ref, h_ref, *, order, n_total):
    P2 = DFT_P2
    kb = af_ref.shape[2]
    st = st_ref[0]
    c0 = order * HY_CH
    ssq = st[0:1, c0:c0 + HY_CH] + st[0:1, 512 + c0:512 + c0 + HY_CH] + 2.0 * st[1:2, c0:c0 + HY_CH]
    scale = lax.rsqrt(ssq + EPS) * (1.0 / n_total)
    g = g_ref[...]
    for kk in range(kb):
        tr, ti = twr_ref[kk], twi_ref[kk]
        xf = _twiddled_spectrum(af_ref, kk, tr, ti, g)
        xb = _twiddled_spectrum(ab_ref, kk, tr, ti, g)
        h_ref[0, 0, kk] = ((xf[:P2] + xb[:P2]) * scale).astype(BF16)
        h_ref[0, 1, kk] = ((xf[P2:] - xb[P2:]) * scale).astype(BF16)


def _filter_spectra(consts, filters, stats, S, kb=8):
    L = filters.shape[0]
    P2, C = DFT_P2, HY_CH
    P1 = 2 * S // P2
    kb = min(kb, P1)
    a = _dft_rows(consts, filters.reshape(L * 4, P1 // 2, P2, C))
    outs = []
    for order in range(HY_ORDER):
        outs.append(pl.pallas_call(
            functools.partial(_filter_spectrum_kernel, order=order, n_total=2 * S),
            name=f"hyena_filter_spectrum_{order}", grid=(L, P1 // kb),
            in_specs=[pl.BlockSpec((2 * P2, 2 * P2), lambda l, i: (0, 0)),
                      pl.BlockSpec((kb, P2, 1), lambda l, i: (i, 0, 0)),
                      pl.BlockSpec((kb, P2, 1), lambda l, i: (i, 0, 0)),
                      pl.BlockSpec((1, 8, 1024), lambda l, i: (l, 0, 0)),
                      pl.BlockSpec((1, 2, kb, P2, C), lambda l, i: (l * 4 + order, 0, i, 0, 0)),
                      pl.BlockSpec((1, 2, kb, P2, C), lambda l, i: (l * 4 + 2 + order, 0, i, 0, 0))],
            out_specs=pl.BlockSpec((1, 2, kb, P2, C), lambda l, i: (l, 0, i, 0, 0)),
            out_shape=jax.ShapeDtypeStruct((L, 2, P1, P2, C), BF16),
            compiler_params=_params("parallel", "parallel"),
        )(consts["g"], consts["tw_r"], consts["tw_i"], stats, a, a))
    return outs


def _spectral_filter_kernel(g_ref, gt_ref, twr_ref, twi_ref, h_ref, a_ref, o_ref):
    P2 = DFT_P2
    g, gt = g_ref[...], gt_ref[...]
    for kk in range(a_ref.shape[2]):
        tr, ti = twr_ref[kk], twi_ref[kk]
        x = _twiddled_spectrum(a_ref, kk, tr, ti, g)
        xr, xi = x[:P2], x[P2:]
        hr, hi = h_ref[0, 0, kk].astype(F32), h_ref[0, 1, kk].astype(F32)
        y = jnp.concatenate([xr * hr - xi * hi, xr * hi + xi * hr], axis=0).astype(BF16)
        bm = jnp.dot(gt, y, preferred_element_type=F32)
        br, bi = bm[:P2], bm[P2:]
        o_ref[0, 0, kk] = (br * tr + bi * ti).astype(BF16)
        o_ref[0, 1, kk] = (bi * tr - br * ti).astype(BF16)


def _spectral_filter(consts, a, h, layer, kb=8):
    B, _, P1, P2, C = a.shape
    kb = min(kb, P1)
    blk = lambda f: pl.BlockSpec((1, 2, kb, P2, C), f)
    return pl.pallas_call(
        _spectral_filter_kernel, name="hyena_spectral_filter", grid=(B, P1 // kb),
        in_specs=[pl.BlockSpec((2 * P2, 2 * P2), lambda b, i: (0, 0)),
                  pl.BlockSpec((2 * P2, 2 * P2), lambda b, i: (0, 0)),
                  pl.BlockSpec((kb, P2, 1), lambda b, i: (i, 0, 0)),
                  pl.BlockSpec((kb, P2, 1), lambda b, i: (i, 0, 0)),
                  blk(lambda b, i: (layer, 0, i, 0, 0)), blk(lambda b, i: (b, 0, i, 0, 0))],
        out_specs=blk(lambda b, i: (b, 0, i, 0, 0)),
        out_shape=jax.ShapeDtypeStruct(a.shape, BF16),
        compiler_params=_params("parallel", "parallel"),
    )(consts["g"], consts["gt"], consts["tw_r"], consts["tw_i"], h, a)


def _idft_rows_kernel(m_ref, b_ref, u_ref, gate_ref, bias_ref, o_ref):
    bm = b_ref[0]
    bm = bm.reshape(bm.shape[0] * bm.shape[1] * bm.shape[2], bm.shape[3]).astype(BF16)
    y = jnp.dot(m_ref[...], bm, preferred_element_type=F32)
    o_ref[0] = gate_ref[0] * (y.reshape(o_ref.shape[1:]) + bias_ref[...] * u_ref[0])


def _idft_rows_gated(consts, bm, u, gate, bias):
    B, p1h, p2, c = u.shape
    tj = DFT_COLS_PER_STEP
    row = lambda: pl.BlockSpec((1, p1h, tj, c), lambda b, j: (b, 0, j, 0))
    return pl.pallas_call(
        _idft_rows_kernel, name="hyena_idft_rows", grid=(B, p2 // tj),
        in_specs=[pl.BlockSpec((p1h * tj, 4 * p1h * tj), lambda b, j: (0, 0)),
                  pl.BlockSpec((1, 2, 2 * p1h, tj, c), lambda b, j: (b, 0, 0, j, 0)),
                  row(), row(), pl.BlockSpec((1, c), lambda b, j: (0, 0))],
        out_specs=row(), out_shape=jax.ShapeDtypeStruct(u.shape, F32),
        compiler_params=_params("parallel", "parallel"),
    )(consts["m_inv"], bm, u, gate, bias.reshape(1, c))


def _short_conv_kernel(x_ref, p_ref, n_ref, w_ref, b_ref, v_ref, x1_ref, x2_ref):
    i = pl.program_id(1)
    x = x_ref[0]
    t = x.shape[0]
    row = lax.broadcasted_iota(jnp.int32, x.shape, 0)
    prev_row = jnp.where(i == 0, 0.0, p_ref[0, 7:8, :])
    next_row = jnp.where(i == pl.num_programs(1) - 1, 0.0, n_ref[0, 0:1, :])
    before = jnp.where(row == 0, prev_row, pltpu.roll(x, 1, axis=0))
    after = jnp.where(row == t - 1, next_row, pltpu.roll(x, t - 1, axis=0))
    w = w_ref[...]
    u = before * w[0:1] + x * w[1:2] + after * w[2:3] + b_ref[...]
    v_ref[0] = u[:, 0:HY_CH]
    x1_ref[0] = u[:, HY_CH:2 * HY_CH]
    x2_ref[0] = u[:, 2 * HY_CH:3 * HY_CH]


def _short_conv(cols, conv_w, conv_b, t=512):
    B, S, W = cols.shape
    t = min(t, S)
    r = t // 8
    out = pl.BlockSpec((1, t, HY_CH), lambda b, i: (b, i, 0))
    return pl.pallas_call(
        _short_conv_kernel, name="hyena_short_conv", grid=(B, S // t),
        in_specs=[pl.BlockSpec((1, t, W), lambda b, i: (b, i, 0)),
                  pl.BlockSpec((1, 8, W), lambda b, i: (b, jnp.maximum(i * r - 1, 0), 0)),
                  pl.BlockSpec((1, 8, W), lambda b, i: (b, jnp.minimum((i + 1) * r, S // 8 - 1), 0)),
                  pl.BlockSpec((HY_SHORT, W), lambda b, i: (0, 0)), pl.BlockSpec((1, W), lambda b, i: (0, 0))],
        out_specs=[out, out, out], out_shape=[jax.ShapeDtypeStruct((B, S, HY_CH), F32)] * 3,
        compiler_params=_params("parallel", "parallel"))(cols, cols, cols, conv_w, conv_b.reshape(1, W))


def _hyena_mixer(consts, cols, conv_w, conv_b, spectra, bias, layer):
    B, S, _ = cols.shape
    P2, C = DFT_P2, HY_CH
    P1 = 2 * S // P2
    v, x1, x2 = _short_conv(cols, conv_w, conv_b)
    rows = lambda a: a.reshape(B, P1 // 2, P2, C)
    z = rows(v)
    for order, gate in enumerate((x1, x2)):
        bm = _spectral_filter(consts, _dft_rows(consts, z), spectra[order], layer)
        z = _idft_rows_gated(consts, bm, z, rows(gate), bias[order])
    return z.reshape(B, S, C)


def _route(lt):
    t = lt.shape[1]
    row = lax.broadcasted_iota(jnp.int32, (8, t), 0)
    lg = jnp.where(row < N_GROUPS, lt[0:8], NEG_INF)
    mg = jnp.max(lg, axis=0, keepdims=True)
    p_top = 1.0 / jnp.sum(jnp.exp(lg - mg), axis=0, keepdims=True)
    g_sel = jnp.min(jnp.where(lg == mg, row, 8), axis=0, keepdims=True)
    le = jnp.zeros((8, t), F32)
    for g in range(N_GROUPS):
        le = jnp.where(g_sel == g, lt[8 + 8 * g:16 + 8 * g], le)
    v1 = jnp.max(le, axis=0, keepdims=True)
    i1 = jnp.min(jnp.where(le == v1, row, 8), axis=0, keepdims=True)
    rest = jnp.where(row == i1, NEG_INF, le)
    v2 = jnp.max(rest, axis=0, keepdims=True)
    i2 = jnp.min(jnp.where(rest == v2, row, 8), axis=0, keepdims=True)
    e = jnp.exp(v2 - v1)
    w1 = 1.0 / (1.0 + e)
    base = g_sel * EXPERTS_PER_GROUP
    idx = jnp.where(row == 0, base + i1, jnp.where(row == 1, base + i2, 0))
    gate = jnp.where(row == 0, p_top * w1, jnp.where(row == 1, p_top * (e * w1), 0.0))
    return idx, gate


def _post_kernel(x_ref, g1_ref, sc_ref, sh_ref, ln_ref, a_ref, b_ref, c_ref, d_ref, wo_ref, wr_ref, br_ref,
                 xo_ref, h_ref, idx_ref, gate_ref):
    wo = wo_ref[0]
    y = (_tn(a_ref[0].astype(BF16), wo[0:256]) + _tn(b_ref[0].astype(BF16), wo[256:512])
         + _tn(c_ref[0].astype(BF16), wo[512:768])
         + jnp.dot(d_ref[0].astype(BF16), wo[768:1024], preferred_element_type=F32))
    x = x_ref[0] + g1_ref[0] * y
    xo_ref[0] = x
    h = _rms(x) * ln_ref[...] * (1.0 + sc_ref[0]) + sh_ref[0]
    h_ref[0] = h
    lt = lax.dot_general(wr_ref[0], h, (((1,), (1,)), ((), ())), preferred_element_type=F32, precision=HIGHEST) + br_ref[0]
    idx, gate = _route(lt)
    idx_ref[...] = idx
    gate_ref[...] = gate


def _post_mixer(x, g1, sc2, sh2, ln_g, layer, mix_a, mix_b, mix_c, mix_d, w_out, w_router, b_router, t=512):
    B, S, D = x.shape
    t = min(t, S)
    nt = S // t
    nat = lambda n: pl.BlockSpec((1, t, n), lambda b, i: (b, i, 0))
    fm = pl.BlockSpec((1, 256, t), lambda b, i: (b, 0, i))
    mod = pl.BlockSpec((1, 1, D), lambda b, i: (b, 0, 0))
    lw = lambda a: pl.BlockSpec((1,) + a.shape[1:], lambda b, i: (layer,) + (0,) * (a.ndim - 1))
    tok = pl.BlockSpec((8, t), lambda b, i: (0, b * nt + i))
    return pl.pallas_call(
        _post_kernel, name="mixer_output_router", grid=(B, nt),
        in_specs=[nat(D), mod, mod, mod, pl.BlockSpec((1, D), lambda b, i: (0, 0)), fm, fm, fm, nat(256),
                  lw(w_out), lw(w_router), lw(b_router)],
        out_specs=[nat(D), nat(D), tok, tok],
        out_shape=[jax.ShapeDtypeStruct((B, S, D), F32), jax.ShapeDtypeStruct((B, S, D), F32),
                   jax.ShapeDtypeStruct((8, B * S), jnp.int32), jax.ShapeDtypeStruct((8, B * S), F32)],
        compiler_params=_params("parallel", "parallel"),
    )(x, g1, sc2, sh2, ln_g[layer].reshape(1, D), mix_a, mix_b, mix_c, mix_d, w_out, w_router, b_router)


def _start_row_gather(src_hbm, dst_ref, sem, row_of, priorities=(0, 1)):
    n = dst_ref.shape[0]
    for i in range(n):
        r = (i % 8) * (n // 8) + i // 8
        pltpu.make_async_copy(src_hbm.at[pl.ds(row_of(r), 1)], dst_ref.at[pl.ds(r, 1)], sem).start(
            priority=priorities[i % len(priorities)])


def _wait_row_gather(src_hbm, dst_ref, sem):
    pltpu.make_async_copy(src_hbm.at[pl.ds(0, dst_ref.shape[0])], dst_ref, sem).wait()


def _expert_kernel(be_ref, tok_ref, h_hbm, wg_ref, wu_ref, wd_ref, o_ref, x_buf, sem):
    i = pl.program_id(0)
    last = pl.num_programs(0) - 1
    slot = i % 2

    @pl.when(i == 0)
    def _():
        _start_row_gather(h_hbm, x_buf.at[0], sem.at[0], lambda r: tok_ref[r])

    nxt = jnp.minimum(i + 1, last) * MOE_ROWS
    _start_row_gather(h_hbm, x_buf.at[1 - slot], sem.at[1 - slot], lambda r: tok_ref[nxt + r])
    _wait_row_gather(h_hbm, x_buf.at[slot], sem.at[slot])

    @pl.when(i == last)
    def _():
        _wait_row_gather(h_hbm, x_buf.at[1 - slot], sem.at[1 - slot])

    x = x_buf[slot].astype(BF16)
    g = jnp.dot(x, wg_ref[0, 0].astype(BF16), preferred_element_type=F32)
    u = jnp.dot(x, wu_ref[0, 0].astype(BF16), preferred_element_type=F32)
    a = (g * jax.nn.sigmoid(g) * u).astype(BF16)
    o_ref[...] = jnp.dot(a, wd_ref[0, 0].astype(BF16), preferred_element_type=F32)


def _expert_ffn(h, slot_tok, block_expert, w_gate, w_up, w_down, layer):
    D = h.shape[1]
    P = slot_tok.shape[0]
    wmap = lambda i, be, tok: (layer, be[i], 0, 0)
    return pl.pallas_call(
        _expert_kernel, name="expert_ffn",
        grid_spec=pltpu.PrefetchScalarGridSpec(
            num_scalar_prefetch=2, grid=(P // MOE_ROWS,),
            in_specs=[pl.BlockSpec(memory_space=pl.ANY),
                      pl.BlockSpec((1, 1, D, D_EXPERT), wmap), pl.BlockSpec((1, 1, D, D_EXPERT), wmap),
                      pl.BlockSpec((1, 1, D_EXPERT, D), wmap)],
            out_specs=pl.BlockSpec((MOE_ROWS, D), lambda i, be, tok: (i, 0)),
            scratch_shapes=[pltpu.VMEM((2, MOE_ROWS, D), F32), pltpu.SemaphoreType.DMA((2,))]),
        out_shape=jax.ShapeDtypeStruct((P, D), F32),
        compiler_params=_params("arbitrary"))(block_expert, slot_tok, h, w_gate, w_up, w_down)


def _dispatch_plan(idx, T):
    e = idx[0:TOP_K]
    onehot = (e.reshape(-1)[:, None] == jnp.arange(N_EXPERTS)[None]).astype(jnp.int32)
    before = jnp.cumsum(onehot, axis=0) - onehot
    rank = jnp.sum(before * onehot, axis=1)
    counts = jnp.sum(onehot, axis=0)
    padded = (counts + MOE_ROWS - 1) // MOE_ROWS * MOE_ROWS
    pad_end = jnp.cumsum(padded)
    dest = ((pad_end - padded)[e.reshape(-1)] + rank).reshape(TOP_K, T)
    n_blocks = (TOP_K * T + N_EXPERTS * (MOE_ROWS - 1) + MOE_ROWS - 1) // MOE_ROWS
    tok = jnp.tile(jnp.arange(T, dtype=jnp.int32), TOP_K)
    slot_tok = (jnp.arange(n_blocks * MOE_ROWS, dtype=jnp.int32) % T).at[dest.reshape(-1)].set(tok)
    starts = jnp.arange(n_blocks, dtype=jnp.int32) * MOE_ROWS
    block_expert = jnp.minimum(jnp.sum((pad_end[None, :] <= starts[:, None]).astype(jnp.int32), axis=1), N_EXPERTS - 1)
    return dest, slot_tok, block_expert.astype(jnp.int32)


def _combine_kernel(dest_ref, x_ref, g2_ref, gate_ref, fg_ref, y_hbm, o_ref, y_buf, sem, *, final, n_tokens):
    t = x_ref.shape[1]
    step = pl.program_id(0) * pl.num_programs(1) + pl.program_id(1)
    last = pl.num_programs(0) * pl.num_programs(1) - 1
    slot = step % 2

    def start(tile, s):
        for k in range(TOP_K):
            _start_row_gather(y_hbm, y_buf.at[s, k], sem.at[s, k], lambda r, k=k: dest_ref[k * n_tokens + tile * t + r])

    def wait(s):
        for k in range(TOP_K):
            _wait_row_gather(y_hbm, y_buf.at[s, k], sem.at[s, k])

    @pl.when(step == 0)
    def _():
        start(0, 0)

    start(jnp.minimum(step + 1, last), 1 - slot)
    wait(slot)

    @pl.when(step == last)
    def _():
        wait(1 - slot)

    gt = gate_ref[...]
    x = x_ref[0] + g2_ref[0] * (gt[:, 0:1] * y_buf[slot, 0] + gt[:, 1:2] * y_buf[slot, 1])
    if final:
        x = _rms(x) * fg_ref[...]
    o_ref[0] = x


def _combine(x, g2, yb, dest, gate_t, final_g, final, t=256):
    B, S, D = x.shape
    t = min(t, S)
    nt = S // t
    nat = pl.BlockSpec((1, t, D), lambda b, i, d: (b, i, 0))
    return pl.pallas_call(
        functools.partial(_combine_kernel, final=final, n_tokens=B * S), name="moe_combine",
        grid_spec=pltpu.PrefetchScalarGridSpec(
            num_scalar_prefetch=1, grid=(B, nt),
            in_specs=[nat, pl.BlockSpec((1, 1, D), lambda b, i, d: (b, 0, 0)),
                      pl.BlockSpec((t, 8), lambda b, i, d: (b * nt + i, 0)),
                      pl.BlockSpec((1, D), lambda b, i, d: (0, 0)),
                      pl.BlockSpec(memory_space=pl.ANY)],
            out_specs=nat,
            scratch_shapes=[pltpu.VMEM((2, TOP_K, t, D), F32), pltpu.SemaphoreType.DMA((2, TOP_K))]),
        out_shape=jax.ShapeDtypeStruct((B, S, D), F32),
        compiler_params=_params("arbitrary", "arbitrary"))(dest.reshape(-1), x, g2, gate_t, final_g.reshape(1, D), yb)


def _moe(x, h, idx, gate, g2, w_gate, w_up, w_down, layer, final_g, final):
    B, S, D = x.shape
    T = B * S
    dest, slot_tok, block_expert = _dispatch_plan(idx, T)
    yb = _expert_ffn(h.reshape(T, D), slot_tok, block_expert, w_gate, w_up, w_down, layer)
    return _combine(x, g2, yb, dest, gate.T, final_g, final)


def kernel(x, c, positions, ln1_g, ln2_g, ada_w, ada_b, w_in, mla_q_norm, mla_kv_norm, mla_w_uq, mla_w_ukv, swa_sink, diff_lambda, diff_norm_g, hy_conv_w, hy_conv_b, hy_w1, hy_b1, hy_w2, hy_b2, hy_w3, hy_b3, hy_w4, hy_sin_freq, hy_bias, w_out, router_g_w, router_g_b, router_e_w, router_e_b, moe_w_gate, moe_w_up, moe_w_down, final_g):
    B, S, D = x.shape
    L = w_in.shape[0]
    mod = _ada_modulation(c, ada_w, ada_b)[:, :B].reshape(L, B, 6, 1, D)
    tab_fm = _rope_tables(positions)
    tab_nat = _natural_rope_table(tab_fm)
    weights = _layer_weight_layouts(w_in, mla_q_norm, mla_kv_norm, mla_w_uq, mla_w_ukv)
    w_out_b = w_out.astype(BF16)
    w_router = jnp.concatenate([router_g_w.transpose(0, 2, 1), jnp.zeros((L, 8 - N_GROUPS, D), F32),
                                router_e_w.transpose(0, 2, 1)], axis=1)
    b_router = jnp.concatenate([router_g_b, jnp.zeros((L, 8 - N_GROUPS), F32), router_e_b], axis=1)[:, :, None]
    consts = _hyena_constants(S)
    filters, stats = _hyena_filters(consts, hy_w1, hy_b1, hy_w2, hy_b2, hy_w3, hy_b3, hy_w4, hy_sin_freq, S)
    spectra = _filter_spectra(consts, filters, stats, S)
    for l in range(L):
        sh1, sc1, g1, sh2, sc2, g2 = (mod[l, :, i] for i in range(6))
        kmla, qn, qp, vmla, kswa, qswa, vswa, kdiff, qdiff, vdiff, hy = _project(
            x, sc1, sh1, ln1_g, l, weights, tab_nat, tab_fm)
        mix_a = _mla_attention(kmla, qn, qp, vmla)
        mix_b = _swa_attention(kswa, qswa, vswa, swa_sink[l])
        lambda_init = 0.8 - 0.6 * math.exp(-0.3 * l)
        mix_c = _diff_attention(kdiff, qdiff, vdiff, diff_lambda[l], diff_norm_g[l], lambda_init)
        mix_d = _hyena_mixer(consts, hy, hy_conv_w[l], hy_conv_b[l], spectra, hy_bias[l], l)
        x, h, idx, gate = _post_mixer(x, g1, sc2, sh2, ln2_g, l, mix_a, mix_b, mix_c, mix_d, w_out_b, w_router, b_router)
        x = _moe(x, h, idx, gate, g2, moe_w_gate, moe_w_up, moe_w_down, l, final_g, final=(l == L - 1))
    return x
```

```python
import functools
import math

import numpy as np
import jax
import jax.numpy as jnp
from jax import lax
from jax.experimental import pallas as pl
from jax.experimental.pallas import tpu as pltpu

F32 = jnp.float32
BF16 = jnp.bfloat16
HIGHEST = lax.Precision.HIGHEST

HEAD_DIM = 64
ROPE_THETA = 10000.0
EPS = 1e-6
NEG_INF = -1e30
LOG2E = 1.4426950408889634
MLA_HEADS, MLA_Q_RANK, MLA_KV_RANK, MLA_NOPE, MLA_ROPE, MLA_V = 4, 256, 128, 64, 32, 64
SWA_HEADS, SWA_KV_HEADS, SWA_WINDOW = 4, 2, 128
DIFF_HEADS, DIFF_QK, DIFF_V = 4, 32, 64
HY_CH, HY_ORDER, HY_DIRS, HY_SHORT, HY_EMB, HY_FFN = 256, 2, 2, 3, 33, 64
HY_BANDS = (HY_EMB - 1) // 2
HY_MIN_DECAY = math.log(1e-2) / 1.5
HY_MAX_DECAY = math.log(1e-2) / 0.3
MLA_COLS = MLA_Q_RANK + MLA_KV_RANK + MLA_ROPE
SWA_COLS = (SWA_HEADS + 2 * SWA_KV_HEADS) * HEAD_DIM
DIFF_COLS = DIFF_HEADS * (4 * DIFF_QK + DIFF_V)
HY_COLS = (HY_ORDER + 1) * HY_CH
N_GROUPS, EXPERTS_PER_GROUP, TOP_K, D_EXPERT = 4, 8, 2, 256
N_EXPERTS = N_GROUPS * EXPERTS_PER_GROUP

V7X_VMEM_LIMIT_BYTES = 56 * 1024 * 1024
LANES = 128
DFT_P2 = 128
DFT_COLS_PER_STEP = 8
MOE_ROWS = 512


def _params(*sem):
    return pltpu.CompilerParams(dimension_semantics=sem, vmem_limit_bytes=V7X_VMEM_LIMIT_BYTES)


def _nt(a, b):
    return lax.dot_general(a, b, (((1,), (1,)), ((), ())), preferred_element_type=F32)


def _tn(a, b):
    return lax.dot_general(a, b, (((0,), (0,)), ((), ())), preferred_element_type=F32)


def _ada_kernel(c_ref, w_ref, b_ref, o_ref):
    c = c_ref[...]
    act = c * jax.nn.sigmoid(c)
    o_ref[0] = jnp.dot(act, w_ref[0], preferred_element_type=F32, precision=HIGHEST) + b_ref[0]


def _ada_modulation(c, ada_w, ada_b):
    L, D, N = ada_w.shape
    B = c.shape[0]
    c8 = jnp.zeros((8, D), F32).at[:B].set(c)
    tn = 1024
    return pl.pallas_call(
        _ada_kernel, name="ada_modulation", grid=(L, N // tn),
        in_specs=[pl.BlockSpec((8, D), lambda l, j: (0, 0)),
                  pl.BlockSpec((1, D, tn), lambda l, j: (l, 0, j)),
                  pl.BlockSpec((1, 1, tn), lambda l, j: (l, 0, j))],
        out_specs=pl.BlockSpec((1, 8, tn), lambda l, j: (l, 0, j)),
        out_shape=jax.ShapeDtypeStruct((L, 8, N), F32),
        compiler_params=_params("parallel", "parallel"))(c8, ada_w, ada_b.reshape(L, 1, N))


def _rope_table_kernel(p_ref, o_ref):
    pos = p_ref[0].astype(F32)
    for half, base in ((32, 0), (16, 128)):
        j = lax.broadcasted_iota(jnp.int32, (half, 1), 0).astype(F32)
        inv = jnp.exp(-math.log(ROPE_THETA) * j / half)
        ang = inv * pos
        cs, sn = jnp.cos(ang), jnp.sin(ang)
        o_ref[0, base:base + half] = cs
        o_ref[0, base + half:base + 2 * half] = cs
        o_ref[0, base + 2 * half:base + 3 * half] = -sn
        o_ref[0, base + 3 * half:base + 4 * half] = sn


def _rope_tables(positions):
    B, S = positions.shape
    return pl.pallas_call(
        _rope_table_kernel, name="rope_tables", grid=(B,),
        in_specs=[pl.BlockSpec((1, 1, S), lambda b: (b, 0, 0))],
        out_specs=pl.BlockSpec((1, 192, S), lambda b: (b, 0, 0)),
        out_shape=jax.ShapeDtypeStruct((B, 192, S), F32),
        compiler_params=_params("parallel"))(positions.reshape(B, 1, S))


def _partner(n_heads, dim):
    r = np.arange(n_heads * dim)
    return (r // dim) * dim + (r % dim + dim // 2) % dim


def _layer_weight_layouts(w_in, mla_q_norm, mla_kv_norm, mla_w_uq, mla_w_ukv):
    L, D, _ = w_in.shape
    oB = MLA_COLS
    oC = oB + SWA_COLS
    oD = oC + DIFF_COLS
    Z = w_in.shape[2]
    wz = jnp.concatenate([w_in, jnp.zeros((L, D, 1), F32)], axis=2)

    kpe = MLA_Q_RANK + MLA_KV_RANK + np.arange(MLA_ROPE)
    pad96 = np.full(LANES - MLA_ROPE, Z)
    swa_k = oB + SWA_HEADS * HEAD_DIM + np.arange(SWA_KV_HEADS * HEAD_DIM)
    h, j, r = np.meshgrid(np.arange(DIFF_HEADS), np.arange(2), np.arange(DIFF_QK), indexing="ij")
    diff_src = (h * 2 * DIFF_QK + j * DIFF_QK + r).transpose(1, 0, 2).reshape(-1)
    diff_q = oC + diff_src
    diff_k = oC + DIFF_HEADS * 2 * DIFF_QK + diff_src
    nat_idx = np.concatenate([
        np.arange(MLA_Q_RANK + MLA_KV_RANK),
        kpe, pad96, kpe[_partner(1, MLA_ROPE)], pad96,
        swa_k, swa_k[_partner(SWA_KV_HEADS, HEAD_DIM)],
        diff_k, diff_k[_partner(2 * DIFF_HEADS, DIFF_QK)],
        oD + np.arange(HY_COLS)])
    w_nat = jnp.take(wz, jnp.asarray(nat_idx), axis=2).astype(BF16)

    swa_q = oB + np.arange(SWA_HEADS * HEAD_DIM)
    swa_v = oB + (SWA_HEADS + SWA_KV_HEADS) * HEAD_DIM + np.arange(SWA_KV_HEADS * HEAD_DIM)
    diff_v = oC + DIFF_HEADS * 4 * DIFF_QK + np.arange(DIFF_HEADS * DIFF_V)
    fm_idx = np.concatenate([
        swa_q, swa_q[_partner(SWA_HEADS, HEAD_DIM)], swa_v,
        diff_q, diff_q[_partner(2 * DIFF_HEADS, DIFF_QK)], diff_v])
    s_swa = HEAD_DIM ** -0.5 * LOG2E
    s_diff = DIFF_QK ** -0.5 * LOG2E
    fm_scale = np.concatenate([np.full(512, s_swa), np.ones(128), np.full(512, s_diff), np.ones(256)]).astype(np.float32)
    w_fm = (jnp.take(wz, jnp.asarray(fm_idx), axis=2) * fm_scale).transpose(0, 2, 1).astype(BF16)

    s_mla = (MLA_NOPE + MLA_ROPE) ** -0.5 * LOG2E
    uq = mla_w_uq * mla_q_norm[:, :, None] * s_mla
    hq = np.arange(MLA_HEADS)[:, None] * (MLA_NOPE + MLA_ROPE)
    qn_idx = (hq + np.arange(MLA_NOPE)[None]).reshape(-1)
    qp_idx = (hq + MLA_NOPE + np.arange(MLA_ROPE)[None]).reshape(-1)
    w_qn = jnp.take(uq, jnp.asarray(qn_idx), axis=2).transpose(0, 2, 1).astype(BF16)
    w_qpa = jnp.take(uq, jnp.asarray(qp_idx), axis=2).transpose(0, 2, 1).astype(BF16)
    w_qpb = jnp.take(uq, jnp.asarray(qp_idx[_partner(MLA_HEADS, MLA_ROPE)]), axis=2).transpose(0, 2, 1).astype(BF16)
    ukv = mla_w_ukv * mla_kv_norm[:, :, None]
    hk = np.arange(MLA_HEADS)[:, None] * (MLA_NOPE + MLA_V)
    kn_idx = (hk + np.arange(MLA_NOPE)[None]).reshape(-1)
    v_idx = (hk + MLA_NOPE + np.arange(MLA_V)[None]).reshape(-1)
    w_kn = jnp.take(ukv, jnp.asarray(kn_idx), axis=2).astype(BF16)
    w_v = jnp.take(ukv, jnp.asarray(v_idx), axis=2).transpose(0, 2, 1).astype(BF16)
    return w_nat, w_fm, w_qn, w_qpa, w_qpb, w_kn, w_v


def _natural_rope_table(tab_fm):
    t = tab_fm.transpose(0, 2, 1)
    cos64, sin64, cos32, sin32 = t[..., 0:64], t[..., 64:128], t[..., 128:160], t[..., 160:192]
    z = jnp.zeros(t.shape[:2] + (LANES - MLA_ROPE,), F32)
    return jnp.concatenate([cos32, z, sin32, z,
                            jnp.tile(cos64, (1, 1, 2)), jnp.tile(sin64, (1, 1, 2)),
                            jnp.tile(cos32, (1, 1, 8)), jnp.tile(sin32, (1, 1, 8))], axis=-1)


_SWA_KV, _DIFF_QK2 = SWA_KV_HEADS * HEAD_DIM, 2 * DIFF_HEADS * DIFF_QK
NAT_WIDTHS = (MLA_Q_RANK, MLA_KV_RANK, LANES, LANES, _SWA_KV, _SWA_KV, _DIFF_QK2, _DIFF_QK2, HY_COLS)
NAT_TABLE_WIDTHS = (LANES, LANES, _SWA_KV, _SWA_KV, _DIFF_QK2, _DIFF_QK2)
FM_WIDTHS = (SWA_HEADS * HEAD_DIM, SWA_HEADS * HEAD_DIM, _SWA_KV, _DIFF_QK2, _DIFF_QK2, DIFF_HEADS * DIFF_V)
FM_TABLE_WIDTHS = (HEAD_DIM, HEAD_DIM, DIFF_QK, DIFF_QK)


def _split(a, widths, axis):
    out, start = [], 0
    for w in widths:
        out.append(lax.slice_in_dim(a, start, start + w, axis=axis))
        start += w
    assert start == a.shape[axis], (start, a.shape)
    return out


def _rms(x):
    return x * lax.rsqrt(jnp.mean(x * x, axis=-1, keepdims=True) + EPS)


def _proj_kernel(x_ref, sc_ref, sh_ref, g_ref, wn_ref, wf_ref, wqn_ref, wqpa_ref, wqpb_ref, wkn_ref, wv_ref,
                 tn_ref, tf_ref,
                 kmla_ref, qn_ref, qp_ref, vmla_ref, kswa_ref, qswa_ref, vswa_ref, kdiff_ref, qdiff_ref, vdiff_ref, hy_ref):
    x = x_ref[0]
    t = x.shape[0]
    h = _rms(x) * g_ref[...] * (1.0 + sc_ref[0]) + sh_ref[0]
    hb = h.astype(BF16)
    cn = jnp.dot(hb, wn_ref[0], preferred_element_type=F32)
    cq, ckv, kpe_a, kpe_b, kswa_a, kswa_b, kdiff_a, kdiff_b, hy = _split(cn, NAT_WIDTHS, axis=1)
    cos_kpe, sin_kpe, cos_kswa, sin_kswa, cos_kdiff, sin_kdiff = _split(tn_ref[0], NAT_TABLE_WIDTHS, axis=1)
    cqn = _rms(cq).astype(BF16)
    ckvn = _rms(ckv).astype(BF16)
    kpe = (kpe_a * cos_kpe + kpe_b * sin_kpe).astype(BF16)
    kswa_ref[0] = (kswa_a * cos_kswa + kswa_b * sin_kswa).astype(BF16)
    kdiff_ref[0] = (kdiff_a * cos_kdiff + kdiff_b * sin_kdiff).astype(BF16)
    hy_ref[0] = hy
    kn = jnp.dot(ckvn, wkn_ref[0], preferred_element_type=F32).astype(BF16)
    for g, kn_g in enumerate(_split(kn, (LANES, LANES), axis=1)):
        kmla_ref[0, g] = jnp.concatenate([kn_g, kpe], axis=1)

    ft = _nt(wf_ref[0], hb)
    qswa_a, qswa_b, vswa, qdiff_a, qdiff_b, vdiff = _split(ft, FM_WIDTHS, axis=0)
    cos64, sin64, cos32, sin32 = _split(tf_ref[0], FM_TABLE_WIDTHS, axis=0)

    def rotate(a, b, cos, sin):
        heads = a.shape[0] // cos.shape[0]
        shape = (heads, cos.shape[0], t)
        return (a.reshape(shape) * cos[None] + b.reshape(shape) * sin[None]).reshape(a.shape).astype(BF16)

    qswa_ref[0] = rotate(qswa_a, qswa_b, cos64, sin64)
    vswa_ref[0] = _with_ones_rows(vswa, t)
    qdiff_ref[0] = rotate(qdiff_a, qdiff_b, cos32, sin32)
    vdiff_ref[0] = _with_ones_rows(vdiff, t)
    qn_ref[0] = _nt(wqn_ref[0], cqn).astype(BF16)
    qp_ref[0] = rotate(_nt(wqpa_ref[0], cqn), _nt(wqpb_ref[0], cqn), cos32, sin32)
    vmla_ref[0] = _with_ones_rows(_nt(wv_ref[0], ckvn), t)


def _project(x, sc1, sh1, ln_g, layer, weights, tab_nat, tab_fm, t=512):
    B, S, D = x.shape
    w_nat, w_fm, w_qn, w_qpa, w_qpb, w_kn, w_v = weights
    wspec = lambda a: pl.BlockSpec((1,) + a.shape[1:], lambda b, i: (layer,) + (0,) * (a.ndim - 1))
    nat = lambda n: pl.BlockSpec((1, t, n), lambda b, i: (b, i, 0))
    fm = lambda n: pl.BlockSpec((1, n, t), lambda b, i: (b, 0, i))
    outs = [
        (jax.ShapeDtypeStruct((B, 2, S, 256), BF16), pl.BlockSpec((1, 2, t, 256), lambda b, i: (b, 0, i, 0))),
        (jax.ShapeDtypeStruct((B, 256, S), BF16), fm(256)),
        (jax.ShapeDtypeStruct((B, 128, S), BF16), fm(128)),
        (jax.ShapeDtypeStruct((B, 4 * V_ROWS, S), BF16), fm(4 * V_ROWS)),
        (jax.ShapeDtypeStruct((B, S, 128), BF16), nat(128)),
        (jax.ShapeDtypeStruct((B, 256, S), BF16), fm(256)),
        (jax.ShapeDtypeStruct((B, 2 * V_ROWS, S), BF16), fm(2 * V_ROWS)),
        (jax.ShapeDtypeStruct((B, S, 256), BF16), nat(256)),
        (jax.ShapeDtypeStruct((B, 256, S), BF16), fm(256)),
        (jax.ShapeDtypeStruct((B, 4 * V_ROWS, S), BF16), fm(4 * V_ROWS)),
        (jax.ShapeDtypeStruct((B, S, HY_COLS), F32), nat(HY_COLS)),
    ]
    return pl.pallas_call(
        _proj_kernel, name="mixer_projection", grid=(B, S // t),
        in_specs=[nat(D),
                  pl.BlockSpec((1, 1, D), lambda b, i: (b, 0, 0)),
                  pl.BlockSpec((1, 1, D), lambda b, i: (b, 0, 0)),
                  pl.BlockSpec((1, D), lambda b, i: (0, 0)),
                  wspec(w_nat), wspec(w_fm), wspec(w_qn), wspec(w_qpa), wspec(w_qpb), wspec(w_kn), wspec(w_v),
                  nat(sum(NAT_TABLE_WIDTHS)), fm(sum(FM_TABLE_WIDTHS))],
        out_specs=[o[1] for o in outs], out_shape=[o[0] for o in outs],
        compiler_params=_params("parallel", "parallel"),
    )(x, sc1, sh1, ln_g[layer].reshape(1, D), w_nat, w_fm, w_qn, w_qpa, w_qpb, w_kn, w_v, tab_nat, tab_fm)


V_ROWS = 80


def _with_ones_rows(v, t):
    tail = (lax.broadcasted_iota(jnp.int32, (V_ROWS - 64, t), 0) == 0).astype(F32)
    parts = []
    for h in range(v.shape[0] // 64):
        parts += [v[h * 64:(h + 1) * 64], tail]
    return jnp.concatenate(parts, axis=0).astype(BF16)


def _flash_scratch(tk, tq):
    return ([pltpu.VMEM((256, tq), BF16)] + [pltpu.VMEM((tk, tq), F32)] * 4 + [pltpu.VMEM((tk, tq), BF16)] * 4
            + [pltpu.VMEM((1, tq), F32)] * 8 + [pltpu.VMEM((V_ROWS, tq), F32), pltpu.VMEM((1, tq), F32)])


def _flash_sweep(k_at, v_at, q_pad, n_blocks, scratch):
    s, p, al, top = scratch[0:4], scratch[4:8], scratch[8:12], scratch[12:16]
    acc_ref, m_ref = scratch[16], scratch[17]

    def scores(i, b):
        x = jnp.dot(k_at(i), q_pad, preferred_element_type=F32)
        s[b][...] = x
        top[b][...] = jnp.max(x, axis=0, keepdims=True)

    def softmax(src, dst):
        m_old = m_ref[...]
        m_new = jnp.maximum(m_old, top[src][...])
        al[dst][...] = jnp.exp2(m_old - m_new)
        p[dst][...] = jnp.exp2(s[src][...] - m_new).astype(BF16)
        m_ref[...] = m_new

    def values(i, b):
        acc_ref[...] = al[b][...] * acc_ref[...] + jnp.dot(v_at(i), p[b][...], preferred_element_type=F32)

    m_ref[...] = jnp.full(m_ref.shape, NEG_INF, F32)
    acc_ref[...] = jnp.zeros(acc_ref.shape, F32)
    scores(0, 2)
    scores(1, 3)
    softmax(2, 0)
    softmax(3, 1)
    scores(2, 0)
    scores(3, 1)

    def body(j, carry):
        i = 4 * j
        scores(i + 4, 2), scores(i + 5, 3)
        softmax(0, 2), softmax(1, 3)
        values(i, 0), values(i + 1, 1)
        scores(i + 6, 0), scores(i + 7, 1)
        softmax(2, 0), softmax(3, 1)
        values(i + 2, 2), values(i + 3, 3)
        return carry

    lax.fori_loop(0, n_blocks // 4 - 1, body, 0)
    n = n_blocks
    softmax(0, 2), softmax(1, 3)
    values(n - 4, 0), values(n - 3, 1), values(n - 2, 2), values(n - 1, 3)
    acc = acc_ref[...]
    return acc[0:64] / acc[64:65]


def _mla_kernel(k_ref, qn_ref, qp_ref, v_ref, o_ref, qpad_ref, *scratch, tk):
    slot = pl.program_id(1) % 2
    S = k_ref.shape[2]
    tq = qn_ref.shape[2]
    qn = qn_ref[0]
    zero = jnp.zeros_like(qn)
    qpad_ref[0:64, :] = jnp.where(slot == 0, qn, zero)
    qpad_ref[64:128, :] = jnp.where(slot == 1, qn, zero)
    qpad_ref[128:160, :] = qp_ref[0]
    qpad_ref[160:256, :] = jnp.zeros((96, tq), BF16)
    o_ref[0] = _flash_sweep(lambda i: k_ref[0, 0, pl.ds(pl.multiple_of(i * tk, tk), tk), :],
                            lambda i: v_ref[0, :, pl.ds(pl.multiple_of(i * tk, tk), tk)],
                            qpad_ref[...], S // tk, scratch)


def _mla_attention(k, qn, qp, v, tq=1024, tk=512):
    B, _, S, _ = k.shape
    tq, tk = min(tq, S), min(tk, S // 4)
    return pl.pallas_call(
        functools.partial(_mla_kernel, tk=tk), name="mla_attention", grid=(B, MLA_HEADS, S // tq),
        in_specs=[pl.BlockSpec((1, 1, S, 256), lambda b, h, i: (b, h // 2, 0, 0)),
                  pl.BlockSpec((1, 64, tq), lambda b, h, i: (b, h, i)),
                  pl.BlockSpec((1, 32, tq), lambda b, h, i: (b, h, i)),
                  pl.BlockSpec((1, V_ROWS, S), lambda b, h, i: (b, h, 0))],
        out_specs=pl.BlockSpec((1, 64, tq), lambda b, h, i: (b, h, i)),
        out_shape=jax.ShapeDtypeStruct((B, 256, S), F32),
        scratch_shapes=_flash_scratch(tk, tq),
        compiler_params=_params("parallel", "parallel", "arbitrary"))(k, qn, qp, v)


def _diff_kernel(lam_ref, g_ref, k_ref, q1_ref, q2_ref, v_ref, o_ref, qpad_ref, *scratch, tk, lambda_init):
    head = pl.program_id(1)
    S = k_ref.shape[1]
    maps = []
    for j, q_ref in enumerate((q1_ref, q2_ref)):
        q = q_ref[0]
        qpad_ref[...] = jnp.zeros(qpad_ref.shape, BF16)
        for s in range(DIFF_HEADS):
            r0 = j * 128 + s * DIFF_QK
            qpad_ref[r0:r0 + DIFF_QK, :] = jnp.where(head == s, q, jnp.zeros_like(q))
        maps.append(_flash_sweep(lambda i: k_ref[0, pl.ds(pl.multiple_of(i * tk, tk), tk), :],
                                 lambda i: v_ref[0, :, pl.ds(pl.multiple_of(i * tk, tk), tk)],
                                 qpad_ref[...], S // tk, scratch))
    lv = lam_ref[...]
    lam = (jnp.exp(jnp.sum(lv[0:1] * lv[1:2], axis=1, keepdims=True))
           - jnp.exp(jnp.sum(lv[2:3] * lv[3:4], axis=1, keepdims=True)) + lambda_init)
    o = maps[0] - lam * maps[1]
    o = o * lax.rsqrt(jnp.mean(o * o, axis=0, keepdims=True) + EPS)
    o_ref[0] = o * g_ref[...] * (1.0 - lambda_init)


def _diff_attention(k, q, v, lam_vecs, norm_g, lambda_init, tq=1024, tk=512):
    B, S, _ = k.shape
    tq, tk = min(tq, S), min(tk, S // 4)
    return pl.pallas_call(
        functools.partial(_diff_kernel, tk=tk, lambda_init=lambda_init), name="diff_attention",
        grid=(B, DIFF_HEADS, S // tq),
        in_specs=[pl.BlockSpec((4, DIFF_QK), lambda b, h, i: (0, 0)),
                  pl.BlockSpec((DIFF_V, 1), lambda b, h, i: (0, 0)),
                  pl.BlockSpec((1, S, 256), lambda b, h, i: (b, 0, 0)),
                  pl.BlockSpec((1, 32, tq), lambda b, h, i: (b, h, i)),
                  pl.BlockSpec((1, 32, tq), lambda b, h, i: (b, DIFF_HEADS + h, i)),
                  pl.BlockSpec((1, V_ROWS, S), lambda b, h, i: (b, h, 0))],
        out_specs=pl.BlockSpec((1, 64, tq), lambda b, h, i: (b, h, i)),
        out_shape=jax.ShapeDtypeStruct((B, 256, S), F32),
        scratch_shapes=_flash_scratch(tk, tq),
        compiler_params=_params("parallel", "parallel", "arbitrary"))(lam_vecs, norm_g.reshape(DIFF_V, 1), k, q, q, v)


def _swa_kernel(sink_ref, q_ref, k_prev, k_main, k_next, v_prev, v_main, v_next, o_ref, *, S):
    qi = pl.program_id(1)
    tq = q_ref.shape[2]
    k = jnp.concatenate([k_prev[0], k_main[0], k_next[0]], axis=0)
    v = jnp.concatenate([v_prev[0], v_main[0], v_next[0]], axis=1)
    shape = (tq + 2 * SWA_WINDOW, tq)
    kpos = qi * tq - SWA_WINDOW + lax.broadcasted_iota(jnp.int32, shape, 0)
    qpos = qi * tq + lax.broadcasted_iota(jnp.int32, shape, 1)
    valid = (jnp.abs(kpos - qpos) <= SWA_WINDOW) & (kpos >= 0) & (kpos < S)
    for head in range(SWA_HEADS):
        kv = head // (SWA_HEADS // SWA_KV_HEADS)
        q = q_ref[0, head * HEAD_DIM:(head + 1) * HEAD_DIM, :]
        zero = jnp.zeros_like(q)
        q_pad = jnp.concatenate([q, zero] if kv == 0 else [zero, q], axis=0)
        s = jnp.where(valid, jnp.dot(k, q_pad, preferred_element_type=F32), NEG_INF)
        sink = sink_ref[head] * LOG2E
        m = jnp.maximum(jnp.max(s, axis=0, keepdims=True), sink)
        e = jnp.exp2(s - m).astype(BF16)
        acc = jnp.dot(v[kv * V_ROWS:(kv + 1) * V_ROWS], e, preferred_element_type=F32)
        o_ref[0, head * HEAD_DIM:(head + 1) * HEAD_DIM, :] = acc[0:HEAD_DIM] / (acc[HEAD_DIM:HEAD_DIM + 1] + jnp.exp2(sink - m))


def _swa_attention(k, q, v, sink, tq=512):
    B, S, _ = k.shape
    W = SWA_WINDOW
    tq = min(tq, S)
    r = tq // W
    nb = S // W
    prev = lambda i: jnp.maximum(i * r - 1, 0)
    nxt = lambda i: jnp.minimum((i + 1) * r, nb - 1)
    return pl.pallas_call(
        functools.partial(_swa_kernel, S=S), name="swa_attention", grid=(B, S // tq),
        in_specs=[pl.BlockSpec(memory_space=pltpu.SMEM),
                  pl.BlockSpec((1, 256, tq), lambda b, i: (b, 0, i)),
                  pl.BlockSpec((1, W, 128), lambda b, i: (b, prev(i), 0)),
                  pl.BlockSpec((1, tq, 128), lambda b, i: (b, i, 0)),
                  pl.BlockSpec((1, W, 128), lambda b, i: (b, nxt(i), 0)),
                  pl.BlockSpec((1, 2 * V_ROWS, W), lambda b, i: (b, 0, prev(i))),
                  pl.BlockSpec((1, 2 * V_ROWS, tq), lambda b, i: (b, 0, i)),
                  pl.BlockSpec((1, 2 * V_ROWS, W), lambda b, i: (b, 0, nxt(i)))],
        out_specs=pl.BlockSpec((1, 256, tq), lambda b, i: (b, 0, i)),
        out_shape=jax.ShapeDtypeStruct((B, 256, S), F32),
        compiler_params=_params("parallel", "parallel"))(sink, q, k, k, k, v, v, v)


def _hyena_constants(S):
    N = 2 * S
    P2 = DFT_P2
    P1 = N // P2
    t = np.linspace(0.0, 1.0, S)
    w = 2.0 * math.pi * np.arange(S) / S
    f = np.linspace(1e-4, HY_BANDS - 1, HY_BANDS)
    z = np.concatenate([t[:, None], np.cos(w[:, None] * f), -np.sin(w[:, None] * f), np.zeros((S, 40 - HY_EMB))], axis=1)
    decay = np.linspace(HY_MIN_DECAY, HY_MAX_DECAY, HY_CH)
    window = np.exp(-t[:, None] * np.abs(decay))
    k1 = np.arange(P1)
    n1 = np.arange(P1 // 2)
    a1 = 2.0 * math.pi * np.outer(k1, n1) / P1
    m_fwd = np.concatenate([np.cos(a1), -np.sin(a1)], axis=0)
    m_inv = np.concatenate([np.cos(a1.T), -np.sin(a1.T)], axis=1)
    a2 = 2.0 * math.pi * np.outer(np.arange(P2), np.arange(P2)) / P2
    c2, s2 = np.cos(a2), np.sin(a2)
    g = np.block([[c2, s2], [-s2, c2]])
    at = 2.0 * math.pi * np.outer(k1, np.arange(P2)) / N
    eye = np.eye(DFT_COLS_PER_STEP)
    return dict(z=jnp.asarray(z, F32), window=jnp.asarray(window, F32),
                m_fwd=jnp.asarray(np.kron(m_fwd, eye), BF16), m_inv=jnp.asarray(np.kron(m_inv, eye), BF16),
                g=jnp.asarray(g, BF16), gt=jnp.asarray(g.T, BF16),
                tw_r=jnp.asarray(np.cos(at)[:, :, None], F32), tw_i=jnp.asarray(-np.sin(at)[:, :, None], F32))


def _filter_kernel(z_ref, win_ref, w1_ref, b1_ref, w2_ref, b2_ref, w3_ref, b3_ref, w4_ref, f_ref, o_ref, st_ref):
    dot = functools.partial(jnp.dot, preferred_element_type=F32, precision=HIGHEST)
    fr = f_ref[0]
    h = jnp.sin(fr[0:1] * (dot(z_ref[...], w1_ref[0]) + b1_ref[0]))
    h = jnp.sin(fr[1:2] * (dot(h, w2_ref[0]) + b2_ref[0]))
    h = jnp.sin(fr[2:3] * (dot(h, w3_ref[0]) + b3_ref[0]))
    out = dot(h, w4_ref[0])
    win = win_ref[...]
    parts = [out[:, q * HY_CH:(q + 1) * HY_CH] * win for q in range(4)]
    for q in range(4):
        o_ref[0, q] = parts[q]
    filt = jnp.concatenate(parts, axis=1)
    ssq = jnp.sum(filt * filt, axis=0, keepdims=True)

    @pl.when(pl.program_id(1) == 0)
    def _():
        cross = filt[0:1, 0:512] * filt[0:1, 512:1024]
        st_ref[0] = jnp.concatenate([ssq, jnp.concatenate([cross, jnp.zeros_like(cross)], axis=1),
                                     jnp.zeros((6, 1024), F32)], axis=0)

    @pl.when(pl.program_id(1) != 0)
    def _():
        st_ref[0, 0:1] = st_ref[0, 0:1] + ssq


def _hyena_filters(consts, hy_w1, hy_b1, hy_w2, hy_b2, hy_w3, hy_b3, hy_w4, hy_sin_freq, S):
    L = hy_w1.shape[0]
    ts = min(512, S)
    w1 = jnp.concatenate([hy_w1, jnp.zeros((L, 40 - HY_EMB, HY_FFN), F32)], axis=1)
    lw = lambda a: pl.BlockSpec((1,) + a.shape[1:], lambda l, i: (l,) + (0,) * (a.ndim - 1))
    b1, b2, b3 = (b.reshape(L, 1, HY_FFN) for b in (hy_b1, hy_b2, hy_b3))
    return pl.pallas_call(
        _filter_kernel, name="hyena_filters", grid=(L, S // ts),
        in_specs=[pl.BlockSpec((ts, 40), lambda l, i: (i, 0)), pl.BlockSpec((ts, HY_CH), lambda l, i: (i, 0)),
                  lw(w1), lw(b1), lw(hy_w2), lw(b2), lw(hy_w3), lw(b3), lw(hy_w4), lw(hy_sin_freq)],
        out_specs=[pl.BlockSpec((1, 4, ts, HY_CH), lambda l, i: (l, 0, i, 0)),
                   pl.BlockSpec((1, 8, 1024), lambda l, i: (l, 0, 0))],
        out_shape=[jax.ShapeDtypeStruct((L, 4, S, HY_CH), F32), jax.ShapeDtypeStruct((L, 8, 1024), F32)],
        compiler_params=_params("parallel", "arbitrary"),
    )(consts["z"], consts["window"], w1, b1, hy_w2, b2, hy_w3, b3, hy_w4, hy_sin_freq)


def _dft_rows_kernel(m_ref, x_ref, o_ref):
    x = x_ref[0]
    x = x.reshape(x.shape[0] * x.shape[1], x.shape[2]).astype(BF16)
    a = jnp.dot(m_ref[...], x, preferred_element_type=F32)
    o_ref[0] = a.reshape(o_ref.shape[1:]).astype(o_ref.dtype)


def _dft_rows(consts, x):
    nb, p1h, p2, c = x.shape
    tj = DFT_COLS_PER_STEP
    return pl.pallas_call(
        _dft_rows_kernel, name="hyena_dft_rows", grid=(nb, p2 // tj),
        in_specs=[pl.BlockSpec((4 * p1h * tj, p1h * tj), lambda b, j: (0, 0)),
                  pl.BlockSpec((1, p1h, tj, c), lambda b, j: (b, 0, j, 0))],
        out_specs=pl.BlockSpec((1, 2, 2 * p1h, tj, c), lambda b, j: (b, 0, 0, j, 0)),
        out_shape=jax.ShapeDtypeStruct((nb, 2, 2 * p1h, p2, c), BF16),
        compiler_params=_params("parallel", "parallel"))(consts["m_fwd"], x)


def _twiddled_spectrum(a_ref, kk, tr, ti, g):
    ar, ai = a_ref[0, 0, kk].astype(F32), a_ref[0, 1, kk].astype(F32)
    pre = jnp.concatenate([ar * tr - ai * ti, ar * ti + ai * tr], axis=0).astype(BF16)
    return jnp.dot(g, pre, preferred_element_type=F32)


def _filter_spectrum_kernel(g_ref, twr_ref, twi_ref, st_ref, af_ref, ab_ref, h_ref, *, order, n_total):
    P2 = DFT_P2
    kb = af_ref.shape[2]
    st = st_ref[0]
    c0 = order * HY_CH
    ssq = st[0:1, c0:c0 + HY_CH] + st[0:1, 512 + c0:512 + c0 + HY_CH] + 2.0 * st[1:2, c0:c0 + HY_CH]
    scale = lax.rsqrt(ssq + EPS) * (1.0 / n_total)
    g = g_ref[...]
    for kk in range(kb):
        tr, ti = twr_ref[kk], twi_ref[kk]
        xf = _twiddled_spectrum(af_ref, kk, tr, ti, g)
        xb = _twiddled_spectrum(ab_ref, kk, tr, ti, g)
        h_ref[0, 0, kk] = ((xf[:P2] + xb[:P2]) * scale).astype(BF16)
        h_ref[0, 1, kk] = ((xf[P2:] - xb[P2:]) * scale).astype(BF16)


def _filter_spectra(consts, filters, stats, S, kb=8):
    L = filters.shape[0]
    P2, C = DFT_P2, HY_CH
    P1 = 2 * S // P2
    kb = min(kb, P1)
    a = _dft_rows(consts, filters.reshape(L * 4, P1 // 2, P2, C))
    outs = []
    for order in range(HY_ORDER):
        outs.append(pl.pallas_call(
            functools.partial(_filter_spectrum_kernel, order=order, n_total=2 * S),
            name=f"hyena_filter_spectrum_{order}", grid=(L, P1 // kb),
            in_specs=[pl.BlockSpec((2 * P2, 2 * P2), lambda l, i: (0, 0)),
                      pl.BlockSpec((kb, P2, 1), lambda l, i: (i, 0, 0)),
                      pl.BlockSpec((kb, P2, 1), lambda l, i: (i, 0, 0)),
                      pl.BlockSpec((1, 8, 1024), lambda l, i: (l, 0, 0)),
                      pl.BlockSpec((1, 2, kb, P2, C), lambda l, i: (l * 4 + order, 0, i, 0, 0)),
                      pl.BlockSpec((1, 2, kb, P2, C), lambda l, i: (l * 4 + 2 + order, 0, i, 0, 0))],
            out_specs=pl.BlockSpec((1, 2, kb, P2, C), lambda l, i: (l, 0, i, 0, 0)),
            out_shape=jax.ShapeDtypeStruct((L, 2, P1, P2, C), BF16),
            compiler_params=_params("parallel", "parallel"),
        )(consts["g"], consts["tw_r"], consts["tw_i"], stats, a, a))
    return outs


def _spectral_filter_kernel(g_ref, gt_ref, twr_ref, twi_ref, h_ref, a_ref, o_ref):
    P2 = DFT_P2
    g, gt = g_ref[...], gt_ref[...]
    for kk in range(a_ref.shape[2]):
        tr, ti = twr_ref[kk], twi_ref[kk]
        x = _twiddled_spectrum(a_ref, kk, tr, ti, g)
        xr, xi = x[:P2], x[P2:]
        hr, hi = h_ref[0, 0, kk].astype(F32), h_ref[0, 1, kk].astype(F32)
        y = jnp.concatenate([xr * hr - xi * hi, xr * hi + xi * hr], axis=0).astype(BF16)
        bm = jnp.dot(gt, y, preferred_element_type=F32)
        br, bi = bm[:P2], bm[P2:]
        o_ref[0, 0, kk] = (br * tr + bi * ti).astype(BF16)
        o_ref[0, 1, kk] = (bi * tr - br * ti).astype(BF16)


def _spectral_filter(consts, a, h, layer, kb=32):
    B, _, P1, P2, C = a.shape
    kb = min(kb, P1)
    blk = lambda f: pl.BlockSpec((1, 2, kb, P2, C), f)
    return pl.pallas_call(
        _spectral_filter_kernel, name="hyena_spectral_filter", grid=(B, P1 // kb),
        in_specs=[pl.BlockSpec((2 * P2, 2 * P2), lambda b, i: (0, 0)),
                  pl.BlockSpec((2 * P2, 2 * P2), lambda b, i: (0, 0)),
                  pl.BlockSpec((kb, P2, 1), lambda b, i: (i, 0, 0)),
                  pl.BlockSpec((kb, P2, 1), lambda b, i: (i, 0, 0)),
                  blk(lambda b, i: (layer, 0, i, 0, 0)), blk(lambda b, i: (b, 0, i, 0, 0))],
        out_specs=blk(lambda b, i: (b, 0, i, 0, 0)),
        out_shape=jax.ShapeDtypeStruct(a.shape, BF16),
        compiler_params=_params("parallel", "parallel"),
    )(consts["g"], consts["gt"], consts["tw_r"], consts["tw_i"], h, a)


def _idft_rows_kernel(m_ref, b_ref, u_ref, gate_ref, bias_ref, o_ref):
    bm = b_ref[0]
    bm = bm.reshape(bm.shape[0] * bm.shape[1] * bm.shape[2], bm.shape[3]).astype(BF16)
    y = jnp.dot(m_ref[...], bm, preferred_element_type=F32)
    o_ref[0] = gate_ref[0] * (y.reshape(o_ref.shape[1:]) + bias_ref[...] * u_ref[0])


def _idft_rows_gated(consts, bm, u, gate, bias):
    B, p1h, p2, c = u.shape
    tj = DFT_COLS_PER_STEP
    row = lambda: pl.BlockSpec((1, p1h, tj, c), lambda b, j: (b, 0, j, 0))
    return pl.pallas_call(
        _idft_rows_kernel, name="hyena_idft_rows", grid=(B, p2 // tj),
        in_specs=[pl.BlockSpec((p1h * tj, 4 * p1h * tj), lambda b, j: (0, 0)),
                  pl.BlockSpec((1, 2, 2 * p1h, tj, c), lambda b, j: (b, 0, 0, j, 0)),
                  row(), row(), pl.BlockSpec((1, c), lambda b, j: (0, 0))],
        out_specs=row(), out_shape=jax.ShapeDtypeStruct(u.shape, F32),
        compiler_params=_params("parallel", "parallel"),
    )(consts["m_inv"], bm, u, gate, bias.reshape(1, c))


def _short_conv_kernel(x_ref, p_ref, n_ref, w_ref, b_ref, v_ref, x1_ref, x2_ref):
    i = pl.program_id(1)
    x = x_ref[0]
    t = x.shape[0]
    row = lax.broadcasted_iota(jnp.int32, x.shape, 0)
    prev_row = jnp.where(i == 0, 0.0, p_ref[0, 7:8, :])
    next_row = jnp.where(i == pl.num_programs(1) - 1, 0.0, n_ref[0, 0:1, :])
    before = jnp.where(row == 0, prev_row, pltpu.roll(x, 1, axis=0))
    after = jnp.where(row == t - 1, next_row, pltpu.roll(x, t - 1, axis=0))
    w = w_ref[...]
    u = before * w[0:1] + x * w[1:2] + after * w[2:3] + b_ref[...]
    v_ref[0] = u[:, 0:HY_CH]
    x1_ref[0] = u[:, HY_CH:2 * HY_CH]
    x2_ref[0] = u[:, 2 * HY_CH:3 * HY_CH]


def _short_conv(cols, conv_w, conv_b, t=512):
    B, S, W = cols.shape
    t = min(t, S)
    r = t // 8
    out = pl.BlockSpec((1, t, HY_CH), lambda b, i: (b, i, 0))
    return pl.pallas_call(
        _short_conv_kernel, name="hyena_short_conv", grid=(B, S // t),
        in_specs=[pl.BlockSpec((1, t, W), lambda b, i: (b, i, 0)),
                  pl.BlockSpec((1, 8, W), lambda b, i: (b, jnp.maximum(i * r - 1, 0), 0)),
                  pl.BlockSpec((1, 8, W), lambda b, i: (b, jnp.minimum((i + 1) * r, S // 8 - 1), 0)),
                  pl.BlockSpec((HY_SHORT, W), lambda b, i: (0, 0)), pl.BlockSpec((1, W), lambda b, i: (0, 0))],
        out_specs=[out, out, out], out_shape=[jax.ShapeDtypeStruct((B, S, HY_CH), F32)] * 3,
        compiler_params=_params("parallel", "parallel"))(cols, cols, cols, conv_w, conv_b.reshape(1, W))


def _hyena_mixer(consts, cols, conv_w, conv_b, spectra, bias, layer):
    B, S, _ = cols.shape
    P2, C = DFT_P2, HY_CH
    P1 = 2 * S // P2
    v, x1, x2 = _short_conv(cols, conv_w, conv_b)
    rows = lambda a: a.reshape(B, P1 // 2, P2, C)
    z = rows(v)
    for order, gate in enumerate((x1, x2)):
        bm = _spectral_filter(consts, _dft_rows(consts, z), spectra[order], layer)
        z = _idft_rows_gated(consts, bm, z, rows(gate), bias[order])
    return z.reshape(B, S, C)


def _route(lt):
    t = lt.shape[1]
    row = lax.broadcasted_iota(jnp.int32, (8, t), 0)
    lg = jnp.where(row < N_GROUPS, lt[0:8], NEG_INF)
    mg = jnp.max(lg, axis=0, keepdims=True)
    p_top = 1.0 / jnp.sum(jnp.exp(lg - mg), axis=0, keepdims=True)
    g_sel = jnp.min(jnp.where(lg == mg, row, 8), axis=0, keepdims=True)
    le = jnp.zeros((8, t), F32)
    for g in range(N_GROUPS):
        le = jnp.where(g_sel == g, lt[8 + 8 * g:16 + 8 * g], le)
    v1 = jnp.max(le, axis=0, keepdims=True)
    i1 = jnp.min(jnp.where(le == v1, row, 8), axis=0, keepdims=True)
    rest = jnp.where(row == i1, NEG_INF, le)
    v2 = jnp.max(rest, axis=0, keepdims=True)
    i2 = jnp.min(jnp.where(rest == v2, row, 8), axis=0, keepdims=True)
    e = jnp.exp(v2 - v1)
    w1 = 1.0 / (1.0 + e)
    base = g_sel * EXPERTS_PER_GROUP
    idx = jnp.where(row == 0, base + i1, jnp.where(row == 1, base + i2, 0))
    gate = jnp.where(row == 0, p_top * w1, jnp.where(row == 1, p_top * (e * w1), 0.0))
    return idx, gate


def _post_kernel(x_ref, g1_ref, sc_ref, sh_ref, ln_ref, a_ref, b_ref, c_ref, d_ref, wo_ref, wr_ref, br_ref,
                 xo_ref, h_ref, idx_ref, gate_ref):
    wo = wo_ref[0]
    y = (_tn(a_ref[0].astype(BF16), wo[0:256]) + _tn(b_ref[0].astype(BF16), wo[256:512])
         + _tn(c_ref[0].astype(BF16), wo[512:768])
         + jnp.dot(d_ref[0].astype(BF16), wo[768:1024], preferred_element_type=F32))
    x = x_ref[0] + g1_ref[0] * y
    xo_ref[0] = x
    h = _rms(x) * ln_ref[...] * (1.0 + sc_ref[0]) + sh_ref[0]
    h_ref[0] = h
    lt = lax.dot_general(wr_ref[0], h, (((1,), (1,)), ((), ())), preferred_element_type=F32, precision=HIGHEST) + br_ref[0]
    idx, gate = _route(lt)
    idx_ref[...] = idx
    gate_ref[...] = gate


def _post_mixer(x, g1, sc2, sh2, ln_g, layer, mix_a, mix_b, mix_c, mix_d, w_out, w_router, b_router, t=512):
    B, S, D = x.shape
    t = min(t, S)
    nt = S // t
    nat = lambda n: pl.BlockSpec((1, t, n), lambda b, i: (b, i, 0))
    fm = pl.BlockSpec((1, 256, t), lambda b, i: (b, 0, i))
    mod = pl.BlockSpec((1, 1, D), lambda b, i: (b, 0, 0))
    lw = lambda a: pl.BlockSpec((1,) + a.shape[1:], lambda b, i: (layer,) + (0,) * (a.ndim - 1))
    tok = pl.BlockSpec((8, t), lambda b, i: (0, b * nt + i))
    return pl.pallas_call(
        _post_kernel, name="mixer_output_router", grid=(B, nt),
        in_specs=[nat(D), mod, mod, mod, pl.BlockSpec((1, D), lambda b, i: (0, 0)), fm, fm, fm, nat(256),
                  lw(w_out), lw(w_router), lw(b_router)],
        out_specs=[nat(D), nat(D), tok, tok],
        out_shape=[jax.ShapeDtypeStruct((B, S, D), F32), jax.ShapeDtypeStruct((B, S, D), F32),
                   jax.ShapeDtypeStruct((8, B * S), jnp.int32), jax.ShapeDtypeStruct((8, B * S), F32)],
        compiler_params=_params("parallel", "parallel"),
    )(x, g1, sc2, sh2, ln_g[layer].reshape(1, D), mix_a, mix_b, mix_c, mix_d, w_out, w_router, b_router)


def _start_row_gather(src_hbm, dst_ref, sem, row_of, priorities=(0, 1)):
    n = dst_ref.shape[0]
    for i in range(n):
        r = (i % 8) * (n // 8) + i // 8
        pltpu.make_async_copy(src_hbm.at[pl.ds(row_of(r), 1)], dst_ref.at[pl.ds(r, 1)], sem).start(
            priority=priorities[i % len(priorities)])


def _wait_row_gather(src_hbm, dst_ref, sem):
    pltpu.make_async_copy(src_hbm.at[pl.ds(0, dst_ref.shape[0])], dst_ref, sem).wait()


def _expert_kernel(be_ref, tok_ref, h_hbm, wg_ref, wu_ref, wd_ref, o_ref, x_buf, sem):
    i = pl.program_id(0)
    last = pl.num_programs(0) - 1
    slot = i % 2

    @pl.when(i == 0)
    def _():
        _start_row_gather(h_hbm, x_buf.at[0], sem.at[0], lambda r: tok_ref[r])

    nxt = jnp.minimum(i + 1, last) * MOE_ROWS
    _start_row_gather(h_hbm, x_buf.at[1 - slot], sem.at[1 - slot], lambda r: tok_ref[nxt + r])
    _wait_row_gather(h_hbm, x_buf.at[slot], sem.at[slot])

    @pl.when(i == last)
    def _():
        _wait_row_gather(h_hbm, x_buf.at[1 - slot], sem.at[1 - slot])

    x = x_buf[slot].astype(BF16)
    g = jnp.dot(x, wg_ref[0, 0].astype(BF16), preferred_element_type=F32)
    u = jnp.dot(x, wu_ref[0, 0].astype(BF16), preferred_element_type=F32)
    a = (g * jax.nn.sigmoid(g) * u).astype(BF16)
    o_ref[...] = jnp.dot(a, wd_ref[0, 0].astype(BF16), preferred_element_type=F32)


def _expert_ffn(h, slot_tok, block_expert, w_gate, w_up, w_down, layer):
    D = h.shape[1]
    P = slot_tok.shape[0]
    wmap = lambda i, be, tok: (layer, be[i], 0, 0)
    return pl.pallas_call(
        _expert_kernel, name="expert_ffn",
        grid_spec=pltpu.PrefetchScalarGridSpec(
            num_scalar_prefetch=2, grid=(P // MOE_ROWS,),
            in_specs=[pl.BlockSpec(memory_space=pl.ANY),
                      pl.BlockSpec((1, 1, D, D_EXPERT), wmap), pl.BlockSpec((1, 1, D, D_EXPERT), wmap),
                      pl.BlockSpec((1, 1, D_EXPERT, D), wmap)],
            out_specs=pl.BlockSpec((MOE_ROWS, D), lambda i, be, tok: (i, 0)),
            scratch_shapes=[pltpu.VMEM((2, MOE_ROWS, D), F32), pltpu.SemaphoreType.DMA((2,))]),
        out_shape=jax.ShapeDtypeStruct((P, D), F32),
        compiler_params=_params("arbitrary"))(block_expert, slot_tok, h, w_gate, w_up, w_down)


def _dispatch_plan(idx, T):
    e = idx[0:TOP_K]
    onehot = (e.reshape(-1)[:, None] == jnp.arange(N_EXPERTS)[None]).astype(jnp.int32)
    before = jnp.cumsum(onehot, axis=0) - onehot
    rank = jnp.sum(before * onehot, axis=1)
    counts = jnp.sum(onehot, axis=0)
    padded = (counts + MOE_ROWS - 1) // MOE_ROWS * MOE_ROWS
    pad_end = jnp.cumsum(padded)
    dest = ((pad_end - padded)[e.reshape(-1)] + rank).reshape(TOP_K, T)
    n_blocks = (TOP_K * T + N_EXPERTS * (MOE_ROWS - 1) + MOE_ROWS - 1) // MOE_ROWS
    tok = jnp.tile(jnp.arange(T, dtype=jnp.int32), TOP_K)
    slot_tok = (jnp.arange(n_blocks * MOE_ROWS, dtype=jnp.int32) % T).at[dest.reshape(-1)].set(tok)
    starts = jnp.arange(n_blocks, dtype=jnp.int32) * MOE_ROWS
    block_expert = jnp.minimum(jnp.sum((pad_end[None, :] <= starts[:, None]).astype(jnp.int32), axis=1), N_EXPERTS - 1)
    return dest, slot_tok, block_expert.astype(jnp.int32)


def _combine_kernel(dest_ref, x_ref, g2_ref, gate_ref, fg_ref, y_hbm, o_ref, y_buf, sem, *, final, n_tokens):
    t = x_ref.shape[1]
    step = pl.program_id(0) * pl.num_programs(1) + pl.program_id(1)
    last = pl.num_programs(0) * pl.num_programs(1) - 1
    slot = step % 2

    def start(tile, s):
        for k in range(TOP_K):
            _start_row_gather(y_hbm, y_buf.at[s, k], sem.at[s, k], lambda r, k=k: dest_ref[k * n_tokens + tile * t + r])

    def wait(s):
        for k in range(TOP_K):
            _wait_row_gather(y_hbm, y_buf.at[s, k], sem.at[s, k])

    @pl.when(step == 0)
    def _():
        start(0, 0)

    start(jnp.minimum(step + 1, last), 1 - slot)
    wait(slot)

    @pl.when(step == last)
    def _():
        wait(1 - slot)

    gt = gate_ref[...]
    x = x_ref[0] + g2_ref[0] * (gt[:, 0:1] * y_buf[slot, 0] + gt[:, 1:2] * y_buf[slot, 1])
    if final:
        x = _rms(x) * fg_ref[...]
    o_ref[0] = x


def _combine(x, g2, yb, dest, gate_t, final_g, final, t=256):
    B, S, D = x.shape
    t = min(t, S)
    nt = S // t
    nat = pl.BlockSpec((1, t, D), lambda b, i, d: (b, i, 0))
    return pl.pallas_call(
        functools.partial(_combine_kernel, final=final, n_tokens=B * S), name="moe_combine",
        grid_spec=pltpu.PrefetchScalarGridSpec(
            num_scalar_prefetch=1, grid=(B, nt),
            in_specs=[nat, pl.BlockSpec((1, 1, D), lambda b, i, d: (b, 0, 0)),
                      pl.BlockSpec((t, 8), lambda b, i, d: (b * nt + i, 0)),
                      pl.BlockSpec((1, D), lambda b, i, d: (0, 0)),
                      pl.BlockSpec(memory_space=pl.ANY)],
            out_specs=nat,
            scratch_shapes=[pltpu.VMEM((2, TOP_K, t, D), F32), pltpu.SemaphoreType.DMA((2, TOP_K))]),
        out_shape=jax.ShapeDtypeStruct((B, S, D), F32),
        compiler_params=_params("arbitrary", "arbitrary"))(dest.reshape(-1), x, g2, gate_t, final_g.reshape(1, D), yb)


def _moe(x, h, idx, gate, g2, w_gate, w_up, w_down, layer, final_g, final):
    B, S, D = x.shape
    T = B * S
    dest, slot_tok, block_expert = _dispatch_plan(idx, T)
    yb = _expert_ffn(h.reshape(T, D), slot_tok, block_expert, w_gate, w_up, w_down, layer)
    return _combine(x, g2, yb, dest, gate.T, final_g, final)


def kernel(x, c, positions, ln1_g, ln2_g, ada_w, ada_b, w_in, mla_q_norm, mla_kv_norm, mla_w_uq, mla_w_ukv, swa_sink, diff_lambda, diff_norm_g, hy_conv_w, hy_conv_b, hy_w1, hy_b1, hy_w2, hy_b2, hy_w3, hy_b3, hy_w4, hy_sin_freq, hy_bias, w_out, router_g_w, router_g_b, router_e_w, router_e_b, moe_w_gate, moe_w_up, moe_w_down, final_g):
    B, S, D = x.shape
    L = w_in.shape[0]
    mod = _ada_modulation(c, ada_w, ada_b)[:, :B].reshape(L, B, 6, 1, D)
    tab_fm = _rope_tables(positions)
    tab_nat = _natural_rope_table(tab_fm)
    weights = _layer_weight_layouts(w_in, mla_q_norm, mla_kv_norm, mla_w_uq, mla_w_ukv)
    w_out_b = w_out.astype(BF16)
    w_router = jnp.concatenate([router_g_w.transpose(0, 2, 1), jnp.zeros((L, 8 - N_GROUPS, D), F32),
                                router_e_w.transpose(0, 2, 1)], axis=1)
    b_router = jnp.concatenate([router_g_b, jnp.zeros((L, 8 - N_GROUPS), F32), router_e_b], axis=1)[:, :, None]
    consts = _hyena_constants(S)
    filters, stats = _hyena_filters(consts, hy_w1, hy_b1, hy_w2, hy_b2, hy_w3, hy_b3, hy_w4, hy_sin_freq, S)
    spectra = _filter_spectra(consts, filters, stats, S)
    for l in range(L):
        sh1, sc1, g1, sh2, sc2, g2 = (mod[l, :, i] for i in range(6))
        kmla, qn, qp, vmla, kswa, qswa, vswa, kdiff, qdiff, vdiff, hy = _project(
            x, sc1, sh1, ln1_g, l, weights, tab_nat, tab_fm)
        mix_a = _mla_attention(kmla, qn, qp, vmla)
        mix_b = _swa_attention(kswa, qswa, vswa, swa_sink[l])
        lambda_init = 0.8 - 0.6 * math.exp(-0.3 * l)
        mix_c = _diff_attention(kdiff, qdiff, vdiff, diff_lambda[l], diff_norm_g[l], lambda_init)
        mix_d = _hyena_mixer(consts, hy, hy_conv_w[l], hy_conv_b[l], spectra, hy_bias[l], l)
        x, h, idx, gate = _post_mixer(x, g1, sc2, sh2, ln2_g, l, mix_a, mix_b, mix_c, mix_d, w_out_b, w_router, b_router)
        x = _moe(x, h, idx, gate, g2, moe_w_gate, moe_w_up, moe_w_down, l, final_g, final=(l == L - 1))
    return x
```

```python
import functools
import math

import numpy as np
import jax
import jax.numpy as jnp
from jax import lax
from jax.experimental import pallas as pl
from jax.experimental.pallas import tpu as pltpu

F32 = jnp.float32
BF16 = jnp.bfloat16
HIGHEST = lax.Precision.HIGHEST

HEAD_DIM = 64
ROPE_THETA = 10000.0
EPS = 1e-6
NEG_INF = -1e30
LOG2E = 1.4426950408889634
MLA_HEADS, MLA_Q_RANK, MLA_KV_RANK, MLA_NOPE, MLA_ROPE, MLA_V = 4, 256, 128, 64, 32, 64
SWA_HEADS, SWA_KV_HEADS, SWA_WINDOW = 4, 2, 128
DIFF_HEADS, DIFF_QK, DIFF_V = 4, 32, 64
HY_CH, HY_ORDER, HY_DIRS, HY_SHORT, HY_EMB, HY_FFN = 256, 2, 2, 3, 33, 64
HY_BANDS = (HY_EMB - 1) // 2
HY_MIN_DECAY = math.log(1e-2) / 1.5
HY_MAX_DECAY = math.log(1e-2) / 0.3
MLA_COLS = MLA_Q_RANK + MLA_KV_RANK + MLA_ROPE
SWA_COLS = (SWA_HEADS + 2 * SWA_KV_HEADS) * HEAD_DIM
DIFF_COLS = DIFF_HEADS * (4 * DIFF_QK + DIFF_V)
HY_COLS = (HY_ORDER + 1) * HY_CH
N_GROUPS, EXPERTS_PER_GROUP, TOP_K, D_EXPERT = 4, 8, 2, 256
N_EXPERTS = N_GROUPS * EXPERTS_PER_GROUP

V7X_VMEM_LIMIT_BYTES = 56 * 1024 * 1024
LANES = 128
DFT_P2 = 128
DFT_COLS_PER_STEP = 8
MOE_ROWS = 512


def _params(*sem):
    return pltpu.CompilerParams(dimension_semantics=sem, vmem_limit_bytes=V7X_VMEM_LIMIT_BYTES)


def _nt(a, b):
    return lax.dot_general(a, b, (((1,), (1,)), ((), ())), preferred_element_type=F32)


def _tn(a, b):
    return lax.dot_general(a, b, (((0,), (0,)), ((), ())), preferred_element_type=F32)


def _ada_kernel(c_ref, w_ref, b_ref, o_ref):
    c = c_ref[...]
    act = c * jax.nn.sigmoid(c)
    o_ref[0] = jnp.dot(act, w_ref[0], preferred_element_type=F32, precision=HIGHEST) + b_ref[0]


def _ada_modulation(c, ada_w, ada_b):
    L, D, N = ada_w.shape
    B = c.shape[0]
    c8 = jnp.zeros((8, D), F32).at[:B].set(c)
    tn = 1024
    return pl.pallas_call(
        _ada_kernel, name="ada_modulation", grid=(L, N // tn),
        in_specs=[pl.BlockSpec((8, D), lambda l, j: (0, 0)),
                  pl.BlockSpec((1, D, tn), lambda l, j: (l, 0, j)),
                  pl.BlockSpec((1, 1, tn), lambda l, j: (l, 0, j))],
        out_specs=pl.BlockSpec((1, 8, tn), lambda l, j: (l, 0, j)),
        out_shape=jax.ShapeDtypeStruct((L, 8, N), F32),
        compiler_params=_params("parallel", "parallel"))(c8, ada_w, ada_b.reshape(L, 1, N))


def _rope_table_kernel(p_ref, o_ref):
    pos = p_ref[0].astype(F32)
    for half, base in ((32, 0), (16, 128)):
        j = lax.broadcasted_iota(jnp.int32, (half, 1), 0).astype(F32)
        inv = jnp.exp(-math.log(ROPE_THETA) * j / half)
        ang = inv * pos
        cs, sn = jnp.cos(ang), jnp.sin(ang)
        o_ref[0, base:base + half] = cs
        o_ref[0, base + half:base + 2 * half] = cs
        o_ref[0, base + 2 * half:base + 3 * half] = -sn
        o_ref[0, base + 3 * half:base + 4 * half] = sn


def _rope_tables(positions):
    B, S = positions.shape
    return pl.pallas_call(
        _rope_table_kernel, name="rope_tables", grid=(B,),
        in_specs=[pl.BlockSpec((1, 1, S), lambda b: (b, 0, 0))],
        out_specs=pl.BlockSpec((1, 192, S), lambda b: (b, 0, 0)),
        out_shape=jax.ShapeDtypeStruct((B, 192, S), F32),
        compiler_params=_params("parallel"))(positions.reshape(B, 1, S))


def _partner(n_heads, dim):
    r = np.arange(n_heads * dim)
    return (r // dim) * dim + (r % dim + dim // 2) % dim


def _layer_weight_layouts(w_in, mla_q_norm, mla_kv_norm, mla_w_uq, mla_w_ukv):
    L, D, _ = w_in.shape
    oB = MLA_COLS
    oC = oB + SWA_COLS
    oD = oC + DIFF_COLS
    Z = w_in.shape[2]
    wz = jnp.concatenate([w_in, jnp.zeros((L, D, 1), F32)], axis=2)

    kpe = MLA_Q_RANK + MLA_KV_RANK + np.arange(MLA_ROPE)
    pad96 = np.full(LANES - MLA_ROPE, Z)
    swa_k = oB + SWA_HEADS * HEAD_DIM + np.arange(SWA_KV_HEADS * HEAD_DIM)
    h, j, r = np.meshgrid(np.arange(DIFF_HEADS), np.arange(2), np.arange(DIFF_QK), indexing="ij")
    diff_src = (h * 2 * DIFF_QK + j * DIFF_QK + r).transpose(1, 0, 2).reshape(-1)
    diff_q = oC + diff_src
    diff_k = oC + DIFF_HEADS * 2 * DIFF_QK + diff_src
    nat_idx = np.concatenate([
        np.arange(MLA_Q_RANK + MLA_KV_RANK),
        kpe, pad96, kpe[_partner(1, MLA_ROPE)], pad96,
        swa_k, swa_k[_partner(SWA_KV_HEADS, HEAD_DIM)],
        diff_k, diff_k[_partner(2 * DIFF_HEADS, DIFF_QK)],
        oD + np.arange(HY_COLS)])
    w_nat = jnp.take(wz, jnp.asarray(nat_idx), axis=2).astype(BF16)

    swa_q = oB + np.arange(SWA_HEADS * HEAD_DIM)
    swa_v = oB + (SWA_HEADS + SWA_KV_HEADS) * HEAD_DIM + np.arange(SWA_KV_HEADS * HEAD_DIM)
    diff_v = oC + DIFF_HEADS * 4 * DIFF_QK + np.arange(DIFF_HEADS * DIFF_V)
    fm_idx = np.concatenate([
        swa_q, swa_q[_partner(SWA_HEADS, HEAD_DIM)], swa_v,
        diff_q, diff_q[_partner(2 * DIFF_HEADS, DIFF_QK)], diff_v])
    s_swa = HEAD_DIM ** -0.5 * LOG2E
    s_diff = DIFF_QK ** -0.5 * LOG2E
    fm_scale = np.concatenate([np.full(512, s_swa), np.ones(128), np.full(512, s_diff), np.ones(256)]).astype(np.float32)
    w_fm = (jnp.take(wz, jnp.asarray(fm_idx), axis=2) * fm_scale).transpose(0, 2, 1).astype(BF16)

    s_mla = (MLA_NOPE + MLA_ROPE) ** -0.5 * LOG2E
    uq = mla_w_uq * mla_q_norm[:, :, None] * s_mla
    hq = np.arange(MLA_HEADS)[:, None] * (MLA_NOPE + MLA_ROPE)
    qn_idx = (hq + np.arange(MLA_NOPE)[None]).reshape(-1)
    qp_idx = (hq + MLA_NOPE + np.arange(MLA_ROPE)[None]).reshape(-1)
    w_qn = jnp.take(uq, jnp.asarray(qn_idx), axis=2).transpose(0, 2, 1).astype(BF16)
    w_qpa = jnp.take(uq, jnp.asarray(qp_idx), axis=2).transpose(0, 2, 1).astype(BF16)
    w_qpb = jnp.take(uq, jnp.asarray(qp_idx[_partner(MLA_HEADS, MLA_ROPE)]), axis=2).transpose(0, 2, 1).astype(BF16)
    ukv = mla_w_ukv * mla_kv_norm[:, :, None]
    hk = np.arange(MLA_HEADS)[:, None] * (MLA_NOPE + MLA_V)
    kn_idx = (hk + np.arange(MLA_NOPE)[None]).reshape(-1)
    v_idx = (hk + MLA_NOPE + np.arange(MLA_V)[None]).reshape(-1)
    w_kn = jnp.take(ukv, jnp.asarray(kn_idx), axis=2).astype(BF16)
    w_v = jnp.take(ukv, jnp.asarray(v_idx), axis=2).transpose(0, 2, 1).astype(BF16)
    return w_nat, w_fm, w_qn, w_qpa, w_qpb, w_kn, w_v


def _natural_rope_table(tab_fm):
    t = tab_fm.transpose(0, 2, 1)
    cos64, sin64, cos32, sin32 = t[..., 0:64], t[..., 64:128], t[..., 128:160], t[..., 160:192]
    z = jnp.zeros(t.shape[:2] + (LANES - MLA_ROPE,), F32)
    return jnp.concatenate([cos32, z, sin32, z,
                            jnp.tile(cos64, (1, 1, 2)), jnp.tile(sin64, (1, 1, 2)),
                            jnp.tile(cos32, (1, 1, 8)), jnp.tile(sin32, (1, 1, 8))], axis=-1)


_SWA_KV, _DIFF_QK2 = SWA_KV_HEADS * HEAD_DIM, 2 * DIFF_HEADS * DIFF_QK
NAT_WIDTHS = (MLA_Q_RANK, MLA_KV_RANK, LANES, LANES, _SWA_KV, _SWA_KV, _DIFF_QK2, _DIFF_QK2, HY_COLS)
NAT_TABLE_WIDTHS = (LANES, LANES, _SWA_KV, _SWA_KV, _DIFF_QK2, _DIFF_QK2)
FM_WIDTHS = (SWA_HEADS * HEAD_DIM, SWA_HEADS * HEAD_DIM, _SWA_KV, _DIFF_QK2, _DIFF_QK2, DIFF_HEADS * DIFF_V)
FM_TABLE_WIDTHS = (HEAD_DIM, HEAD_DIM, DIFF_QK, DIFF_QK)


def _split(a, widths, axis):
    out, start = [], 0
    for w in widths:
        out.append(lax.slice_in_dim(a, start, start + w, axis=axis))
        start += w
    assert start == a.shape[axis], (start, a.shape)
    return out


def _rms(x):
    return x * lax.rsqrt(jnp.mean(x * x, axis=-1, keepdims=True) + EPS)


def _proj_kernel(x_ref, sc_ref, sh_ref, g_ref, wn_ref, wf_ref, wqn_ref, wqpa_ref, wqpb_ref, wkn_ref, wv_ref,
                 tn_ref, tf_ref,
                 kmla_ref, qn_ref, qp_ref, vmla_ref, kswa_ref, qswa_ref, vswa_ref, kdiff_ref, qdiff_ref, vdiff_ref, hy_ref):
    x = x_ref[0]
    t = x.shape[0]
    h = _rms(x) * g_ref[...] * (1.0 + sc_ref[0]) + sh_ref[0]
    hb = h.astype(BF16)
    cn = jnp.dot(hb, wn_ref[0], preferred_element_type=F32)
    cq, ckv, kpe_a, kpe_b, kswa_a, kswa_b, kdiff_a, kdiff_b, hy = _split(cn, NAT_WIDTHS, axis=1)
    cos_kpe, sin_kpe, cos_kswa, sin_kswa, cos_kdiff, sin_kdiff = _split(tn_ref[0], NAT_TABLE_WIDTHS, axis=1)
    cqn = _rms(cq).astype(BF16)
    ckvn = _rms(ckv).astype(BF16)
    kpe = (kpe_a * cos_kpe + kpe_b * sin_kpe).astype(BF16)
    kswa_ref[0] = (kswa_a * cos_kswa + kswa_b * sin_kswa).astype(BF16)
    kdiff_ref[0] = (kdiff_a * cos_kdiff + kdiff_b * sin_kdiff).astype(BF16)
    hy_ref[0] = hy
    kn = jnp.dot(ckvn, wkn_ref[0], preferred_element_type=F32).astype(BF16)
    for g, kn_g in enumerate(_split(kn, (LANES, LANES), axis=1)):
        kmla_ref[0, g] = jnp.concatenate([kn_g, kpe], axis=1)

    ft = _nt(wf_ref[0], hb)
    qswa_a, qswa_b, vswa, qdiff_a, qdiff_b, vdiff = _split(ft, FM_WIDTHS, axis=0)
    cos64, sin64, cos32, sin32 = _split(tf_ref[0], FM_TABLE_WIDTHS, axis=0)

    def rotate(a, b, cos, sin):
        heads = a.shape[0] // cos.shape[0]
        shape = (heads, cos.shape[0], t)
        return (a.reshape(shape) * cos[None] + b.reshape(shape) * sin[None]).reshape(a.shape).astype(BF16)

    qswa_ref[0] = rotate(qswa_a, qswa_b, cos64, sin64)
    vswa_ref[0] = _with_ones_rows(vswa, t)
    qdiff_ref[0] = rotate(qdiff_a, qdiff_b, cos32, sin32)
    vdiff_ref[0] = _with_ones_rows(vdiff, t)
    qn_ref[0] = _nt(wqn_ref[0], cqn).astype(BF16)
    qp_ref[0] = rotate(_nt(wqpa_ref[0], cqn), _nt(wqpb_ref[0], cqn), cos32, sin32)
    vmla_ref[0] = _with_ones_rows(_nt(wv_ref[0], ckvn), t)


def _project(x, sc1, sh1, ln_g, layer, weights, tab_nat, tab_fm, t=512):
    B, S, D = x.shape
    w_nat, w_fm, w_qn, w_qpa, w_qpb, w_kn, w_v = weights
    wspec = lambda a: pl.BlockSpec((1,) + a.shape[1:], lambda b, i: (layer,) + (0,) * (a.ndim - 1))
    nat = lambda n: pl.BlockSpec((1, t, n), lambda b, i: (b, i, 0))
    fm = lambda n: pl.BlockSpec((1, n, t), lambda b, i: (b, 0, i))
    outs = [
        (jax.ShapeDtypeStruct((B, 2, S, 256), BF16), pl.BlockSpec((1, 2, t, 256), lambda b, i: (b, 0, i, 0))),
        (jax.ShapeDtypeStruct((B, 256, S), BF16), fm(256)),
        (jax.ShapeDtypeStruct((B, 128, S), BF16), fm(128)),
        (jax.ShapeDtypeStruct((B, 4 * V_ROWS, S), BF16), fm(4 * V_ROWS)),
        (jax.ShapeDtypeStruct((B, S, 128), BF16), nat(128)),
        (jax.ShapeDtypeStruct((B, 256, S), BF16), fm(256)),
        (jax.ShapeDtypeStruct((B, 2 * V_ROWS, S), BF16), fm(2 * V_ROWS)),
        (jax.ShapeDtypeStruct((B, S, 256), BF16), nat(256)),
        (jax.ShapeDtypeStruct((B, 256, S), BF16), fm(256)),
        (jax.ShapeDtypeStruct((B, 4 * V_ROWS, S), BF16), fm(4 * V_ROWS)),
        (jax.ShapeDtypeStruct((B, S, HY_COLS), F32), nat(HY_COLS)),
    ]
    return pl.pallas_call(
        _proj_kernel, name="mixer_projection", grid=(B, S // t),
        in_specs=[nat(D),
                  pl.BlockSpec((1, 1, D), lambda b, i: (b, 0, 0)),
                  pl.BlockSpec((1, 1, D), lambda b, i: (b, 0, 0)),
                  pl.BlockSpec((1, D), lambda b, i: (0, 0)),
                  wspec(w_nat), wspec(w_fm), wspec(w_qn), wspec(w_qpa), wspec(w_qpb), wspec(w_kn), wspec(w_v),
                  nat(sum(NAT_TABLE_WIDTHS)), fm(sum(FM_TABLE_WIDTHS))],
        out_specs=[o[1] for o in outs], out_shape=[o[0] for o in outs],
        compiler_params=_params("parallel", "parallel"),
    )(x, sc1, sh1, ln_g[layer].reshape(1, D), w_nat, w_fm, w_qn, w_qpa, w_qpb, w_kn, w_v, tab_nat, tab_fm)


V_ROWS = 80


def _with_ones_rows(v, t):
    tail = (lax.broadcasted_iota(jnp.int32, (V_ROWS - 64, t), 0) == 0).astype(F32)
    parts = []
    for h in range(v.shape[0] // 64):
        parts += [v[h * 64:(h + 1) * 64], tail]
    return jnp.concatenate(parts, axis=0).astype(BF16)


def _flash_scratch(tk, tq):
    return ([pltpu.VMEM((256, tq), BF16)] + [pltpu.VMEM((tk, tq), F32)] * 4 + [pltpu.VMEM((tk, tq), BF16)] * 4
            + [pltpu.VMEM((1, tq), F32)] * 8 + [pltpu.VMEM((V_ROWS, tq), F32), pltpu.VMEM((1, tq), F32)])


def _flash_sweep(k_at, v_at, q_pad, n_blocks, scratch):
    s, p, al, top = scratch[0:4], scratch[4:8], scratch[8:12], scratch[12:16]
    acc_ref, m_ref = scratch[16], scratch[17]

    def scores(i, b):
        x = jnp.dot(k_at(i), q_pad, preferred_element_type=F32)
        s[b][...] = x
        top[b][...] = jnp.max(x, axis=0, keepdims=True)

    def softmax(src, dst):
        m_old = m_ref[...]
        m_new = jnp.maximum(m_old, top[src][...])
        al[dst][...] = jnp.exp2(m_old - m_new)
        p[dst][...] = jnp.exp2(s[src][...] - m_new).astype(BF16)
        m_ref[...] = m_new

    def values(i, b):
        acc_ref[...] = al[b][...] * acc_ref[...] + jnp.dot(v_at(i), p[b][...], preferred_element_type=F32)

    m_ref[...] = jnp.full(m_ref.shape, NEG_INF, F32)
    acc_ref[...] = jnp.zeros(acc_ref.shape, F32)
    scores(0, 2)
    scores(1, 3)
    softmax(2, 0)
    softmax(3, 1)
    scores(2, 0)
    scores(3, 1)

    def body(j, carry):
        i = 4 * j
        scores(i + 4, 2), scores(i + 5, 3)
        softmax(0, 2), softmax(1, 3)
        values(i, 0), values(i + 1, 1)
        scores(i + 6, 0), scores(i + 7, 1)
        softmax(2, 0), softmax(3, 1)
        values(i + 2, 2), values(i + 3, 3)
        return carry

    lax.fori_loop(0, n_blocks // 4 - 1, body, 0)
    n = n_blocks
    softmax(0, 2), softmax(1, 3)
    values(n - 4, 0), values(n - 3, 1), values(n - 2, 2), values(n - 1, 3)
    acc = acc_ref[...]
    return acc[0:64] / acc[64:65]


def _mla_kernel(k_ref, qn_ref, qp_ref, v_ref, o_ref, qpad_ref, *scratch, tk):
    slot = pl.program_id(1) % 2
    S = k_ref.shape[2]
    tq = qn_ref.shape[2]
    qn = qn_ref[0]
    zero = jnp.zeros_like(qn)
    qpad_ref[0:64, :] = jnp.where(slot == 0, qn, zero)
    qpad_ref[64:128, :] = jnp.where(slot == 1, qn, zero)
    qpad_ref[128:160, :] = qp_ref[0]
    qpad_ref[160:256, :] = jnp.zeros((96, tq), BF16)
    o_ref[0] = _flash_sweep(lambda i: k_ref[0, 0, pl.ds(pl.multiple_of(i * tk, tk), tk), :],
                            lambda i: v_ref[0, :, pl.ds(pl.multiple_of(i * tk, tk), tk)],
                            qpad_ref[...], S // tk, scratch)


def _mla_attention(k, qn, qp, v, tq=1024, tk=512):
    B, _, S, _ = k.shape
    tq, tk = min(tq, S), min(tk, S // 4)
    return pl.pallas_call(
        functools.partial(_mla_kernel, tk=tk), name="mla_attention", grid=(B, MLA_HEADS, S // tq),
        in_specs=[pl.BlockSpec((1, 1, S, 256), lambda b, h, i: (b, h // 2, 0, 0)),
                  pl.BlockSpec((1, 64, tq), lambda b, h, i: (b, h, i)),
                  pl.BlockSpec((1, 32, tq), lambda b, h, i: (b, h, i)),
                  pl.BlockSpec((1, V_ROWS, S), lambda b, h, i: (b, h, 0))],
        out_specs=pl.BlockSpec((1, 64, tq), lambda b, h, i: (b, h, i)),
        out_shape=jax.ShapeDtypeStruct((B, 256, S), F32),
        scratch_shapes=_flash_scratch(tk, tq),
        compiler_params=_params("parallel", "parallel", "arbitrary"))(k, qn, qp, v)


def _diff_kernel(lam_ref, g_ref, k_ref, q1_ref, q2_ref, v_ref, o_ref, qpad_ref, *scratch, tk, lambda_init):
    head = pl.program_id(1)
    S = k_ref.shape[1]
    maps = []
    for j, q_ref in enumerate((q1_ref, q2_ref)):
        q = q_ref[0]
        qpad_ref[...] = jnp.zeros(qpad_ref.shape, BF16)
        for s in range(DIFF_HEADS):
            r0 = j * 128 + s * DIFF_QK
            qpad_ref[r0:r0 + DIFF_QK, :] = jnp.where(head == s, q, jnp.zeros_like(q))
        maps.append(_flash_sweep(lambda i: k_ref[0, pl.ds(pl.multiple_of(i * tk, tk), tk), :],
                                 lambda i: v_ref[0, :, pl.ds(pl.multiple_of(i * tk, tk), tk)],
                                 qpad_ref[...], S // tk, scratch))
    lv = lam_ref[...]
    lam = (jnp.exp(jnp.sum(lv[0:1] * lv[1:2], axis=1, keepdims=True))
           - jnp.exp(jnp.sum(lv[2:3] * lv[3:4], axis=1, keepdims=True)) + lambda_init)
    o = maps[0] - lam * maps[1]
    o = o * lax.rsqrt(jnp.mean(o * o, axis=0, keepdims=True) + EPS)
    o_ref[0] = o * g_ref[...] * (1.0 - lambda_init)


def _diff_attention(k, q, v, lam_vecs, norm_g, lambda_init, tq=1024, tk=512):
    B, S, _ = k.shape
    tq, tk = min(tq, S), min(tk, S // 4)
    return pl.pallas_call(
        functools.partial(_diff_kernel, tk=tk, lambda_init=lambda_init), name="diff_attention",
        grid=(B, DIFF_HEADS, S // tq),
        in_specs=[pl.BlockSpec((4, DIFF_QK), lambda b, h, i: (0, 0)),
                  pl.BlockSpec((DIFF_V, 1), lambda b, h, i: (0, 0)),
                  pl.BlockSpec((1, S, 256), lambda b, h, i: (b, 0, 0)),
                  pl.BlockSpec((1, 32, tq), lambda b, h, i: (b, h, i)),
                  pl.BlockSpec((1, 32, tq), lambda b, h, i: (b, DIFF_HEADS + h, i)),
                  pl.BlockSpec((1, V_ROWS, S), lambda b, h, i: (b, h, 0))],
        out_specs=pl.BlockSpec((1, 64, tq), lambda b, h, i: (b, h, i)),
        out_shape=jax.ShapeDtypeStruct((B, 256, S), F32),
        scratch_shapes=_flash_scratch(tk, tq),
        compiler_params=_params("parallel", "parallel", "arbitrary"))(lam_vecs, norm_g.reshape(DIFF_V, 1), k, q, q, v)


def _swa_kernel(sink_ref, q_ref, k_prev, k_main, k_next, v_prev, v_main, v_next, o_ref, *, S):
    qi = pl.program_id(1)
    tq = q_ref.shape[2]
    k = jnp.concatenate([k_prev[0], k_main[0], k_next[0]], axis=0)
    v = jnp.concatenate([v_prev[0], v_main[0], v_next[0]], axis=1)
    shape = (tq + 2 * SWA_WINDOW, tq)
    kpos = qi * tq - SWA_WINDOW + lax.broadcasted_iota(jnp.int32, shape, 0)
    qpos = qi * tq + lax.broadcasted_iota(jnp.int32, shape, 1)
    valid = (jnp.abs(kpos - qpos) <= SWA_WINDOW) & (kpos >= 0) & (kpos < S)
    for head in range(SWA_HEADS):
        kv = head // (SWA_HEADS // SWA_KV_HEADS)
        q = q_ref[0, head * HEAD_DIM:(head + 1) * HEAD_DIM, :]
        zero = jnp.zeros_like(q)
        q_pad = jnp.concatenate([q, zero] if kv == 0 else [zero, q], axis=0)
        s = jnp.where(valid, jnp.dot(k, q_pad, preferred_element_type=F32), NEG_INF)
        sink = sink_ref[head] * LOG2E
        m = jnp.maximum(jnp.max(s, axis=0, keepdims=True), sink)
        e = jnp.exp2(s - m).astype(BF16)
        acc = jnp.dot(v[kv * V_ROWS:(kv + 1) * V_ROWS], e, preferred_element_type=F32)
        o_ref[0, head * HEAD_DIM:(head + 1) * HEAD_DIM, :] = acc[0:HEAD_DIM] / (acc[HEAD_DIM:HEAD_DIM + 1] + jnp.exp2(sink - m))


def _swa_attention(k, q, v, sink, tq=512):
    B, S, _ = k.shape
    W = SWA_WINDOW
    tq = min(tq, S)
    r = tq // W
    nb = S // W
    prev = lambda i: jnp.maximum(i * r - 1, 0)
    nxt = lambda i: jnp.minimum((i + 1) * r, nb - 1)
    return pl.pallas_call(
        functools.partial(_swa_kernel, S=S), name="swa_attention", grid=(B, S // tq),
        in_specs=[pl.BlockSpec(memory_space=pltpu.SMEM),
                  pl.BlockSpec((1, 256, tq), lambda b, i: (b, 0, i)),
                  pl.BlockSpec((1, W, 128), lambda b, i: (b, prev(i), 0)),
                  pl.BlockSpec((1, tq, 128), lambda b, i: (b, i, 0)),
                  pl.BlockSpec((1, W, 128), lambda b, i: (b, nxt(i), 0)),
                  pl.BlockSpec((1, 2 * V_ROWS, W), lambda b, i: (b, 0, prev(i))),
                  pl.BlockSpec((1, 2 * V_ROWS, tq), lambda b, i: (b, 0, i)),
                  pl.BlockSpec((1, 2 * V_ROWS, W), lambda b, i: (b, 0, nxt(i)))],
        out_specs=pl.BlockSpec((1, 256, tq), lambda b, i: (b, 0, i)),
        out_shape=jax.ShapeDtypeStruct((B, 256, S), F32),
        compiler_params=_params("parallel", "parallel"))(sink, q, k, k, k, v, v, v)


def _hyena_constants(S):
    N = 2 * S
    P2 = DFT_P2
    P1 = N // P2
    t = np.linspace(0.0, 1.0, S)
    w = 2.0 * math.pi * np.arange(S) / S
    f = np.linspace(1e-4, HY_BANDS - 1, HY_BANDS)
    z = np.concatenate([t[:, None], np.cos(w[:, None] * f), -np.sin(w[:, None] * f), np.zeros((S, 40 - HY_EMB))], axis=1)
    decay = np.linspace(HY_MIN_DECAY, HY_MAX_DECAY, HY_CH)
    window = np.exp(-t[:, None] * np.abs(decay))
    k1 = np.arange(P1)
    n1 = np.arange(P1 // 2)
    a1 = 2.0 * math.pi * np.outer(k1, n1) / P1
    m_fwd = np.concatenate([np.cos(a1), -np.sin(a1)], axis=0)
    m_inv = np.concatenate([np.cos(a1.T), -np.sin(a1.T)], axis=1)
    a2 = 2.0 * math.pi * np.outer(np.arange(P2), np.arange(P2)) / P2
    c2, s2 = np.cos(a2), np.sin(a2)
    g = np.block([[c2, s2], [-s2, c2]])
    at = 2.0 * math.pi * np.outer(k1, np.arange(P2)) / N
    eye = np.eye(DFT_COLS_PER_STEP)
    return dict(z=jnp.asarray(z, F32), window=jnp.asarray(window, F32),
                m_fwd=jnp.asarray(np.kron(m_fwd, eye), BF16), m_inv=jnp.asarray(np.kron(m_inv, eye), BF16),
                g=jnp.asarray(g, BF16), gt=jnp.asarray(g.T, BF16),
                tw_r=jnp.asarray(np.cos(at)[:, :, None], F32), tw_i=jnp.asarray(-np.sin(at)[:, :, None], F32))


def _filter_kernel(z_ref, win_ref, w1_ref, b1_ref, w2_ref, b2_ref, w3_ref, b3_ref, w4_ref, f_ref, o_ref, st_ref):
    dot = functools.partial(jnp.dot, preferred_element_type=F32, precision=HIGHEST)
    fr = f_ref[0]
    h = jnp.sin(fr[0:1] * (dot(z_ref[...], w1_ref[0]) + b1_ref[0]))
    h = jnp.sin(fr[1:2] * (dot(h, w2_ref[0]) + b2_ref[0]))
    h = jnp.sin(fr[2:3] * (dot(h, w3_ref[0]) + b3_ref[0]))
    out = dot(h, w4_ref[0])
    win = win_ref[...]
    parts = [out[:, q * HY_CH:(q + 1) * HY_CH] * win for q in range(4)]
    for q in range(4):
        o_ref[0, q] = parts[q]
    filt = jnp.concatenate(parts, axis=1)
    ssq = jnp.sum(filt * filt, axis=0, keepdims=True)

    @pl.when(pl.program_id(1) == 0)
    def _():
        cross = filt[0:1, 0:512] * filt[0:1, 512:1024]
        st_ref[0] = jnp.concatenate([ssq, jnp.concatenate([cross, jnp.zeros_like(cross)], axis=1),
                                     jnp.zeros((6, 1024), F32)], axis=0)

    @pl.when(pl.program_id(1) != 0)
    def _():
        st_ref[0, 0:1] = st_ref[0, 0:1] + ssq


def _hyena_filters(consts, hy_w1, hy_b1, hy_w2, hy_b2, hy_w3, hy_b3, hy_w4, hy_sin_freq, S):
    L = hy_w1.shape[0]
    ts = min(512, S)
    w1 = jnp.concatenate([hy_w1, jnp.zeros((L, 40 - HY_EMB, HY_FFN), F32)], axis=1)
    lw = lambda a: pl.BlockSpec((1,) + a.shape[1:], lambda l, i: (l,) + (0,) * (a.ndim - 1))
    b1, b2, b3 = (b.reshape(L, 1, HY_FFN) for b in (hy_b1, hy_b2, hy_b3))
    return pl.pallas_call(
        _filter_kernel, name="hyena_filters", grid=(L, S // ts),
        in_specs=[pl.BlockSpec((ts, 40), lambda l, i: (i, 0)), pl.BlockSpec((ts, HY_CH), lambda l, i: (i, 0)),
                  lw(w1), lw(b1), lw(hy_w2), lw(b2), lw(hy_w3), lw(b3), lw(hy_w4), lw(hy_sin_freq)],
        out_specs=[pl.BlockSpec((1, 4, ts, HY_CH), lambda l, i: (l, 0, i, 0)),
                   pl.BlockSpec((1, 8, 1024), lambda l, i: (l, 0, 0))],
        out_shape=[jax.ShapeDtypeStruct((L, 4, S, HY_CH), F32), jax.ShapeDtypeStruct((L, 8, 1024), F32)],
        compiler_params=_params("parallel", "arbitrary"),
    )(consts["z"], consts["window"], w1, b1, hy_w2, b2, hy_w3, b3, hy_w4, hy_sin_freq)


def _dft_rows_kernel(m_ref, x_ref, o_ref):
    x = x_ref[0]
    x = x.reshape(x.shape[0] * x.shape[1], x.shape[2]).astype(BF16)
    a = jnp.dot(m_ref[...], x, preferred_element_type=F32)
    o_ref[0] = a.reshape(o_ref.shape[1:]).astype(o_ref.dtype)


def _dft_rows(consts, x):
    nb, p1h, p2, c = x.shape
    tj = DFT_COLS_PER_STEP
    return pl.pallas_call(
        _dft_rows_kernel, name="hyena_dft_rows", grid=(nb, p2 // tj),
        in_specs=[pl.BlockSpec((4 * p1h * tj, p1h * tj), lambda b, j: (0, 0)),
                  pl.BlockSpec((1, p1h, tj, c), lambda b, j: (b, 0, j, 0))],
        out_specs=pl.BlockSpec((1, 2, 2 * p1h, tj, c), lambda b, j: (b, 0, 0, j, 0)),
        out_shape=jax.ShapeDtypeStruct((nb, 2, 2 * p1h, p2, c), BF16),
        compiler_params=_params("parallel", "parallel"))(consts["m_fwd"], x)


def _twiddled_spectrum(a_ref, kk, tr, ti, g):
    ar, ai = a_ref[0, 0, kk].astype(F32), a_ref[0, 1, kk].astype(F32)
    pre = jnp.concatenate([ar * tr - ai * ti, ar * ti + ai * tr], axis=0).astype(BF16)
    return jnp.dot(g, pre, preferred_element_type=F32)


def _filter_spectrum_kernel(g_ref, twr_ref, twi_ref, st_ref, af_ref, ab_ref, h_ref, *, order, n_total):
    P2 = DFT_P2
    kb = af_ref.shape[2]
    st = st_ref[0]
    c0 = order * HY_CH
    ssq = st[0:1, c0:c0 + HY_CH] + st[0:1, 512 + c0:512 + c0 + HY_CH] + 2.0 * st[1:2, c0:c0 + HY_CH]
    scale = lax.rsqrt(ssq + EPS) * (1.0 / n_total)
    g = g_ref[...]
    for kk in range(kb):
        tr, ti = twr_ref[kk], twi_ref[kk]
        xf = _twiddled_spectrum(af_ref, kk, tr, ti, g)
        xb = _twiddled_spectrum(ab_ref, kk, tr, ti, g)
        h_ref[0, 0, kk] = ((xf[:P2] + xb[:P2]) * scale).astype(BF16)
        h_ref[0, 1, kk] = ((xf[P2:] - xb[P2:]) * scale).astype(BF16)


def _filter_spectra(consts, filters, stats, S, kb=8):
    L = filters.shape[0]
    P2, C = DFT_P2, HY_CH
    P1 = 2 * S // P2
    kb = min(kb, P1)
    a = _dft_rows(consts, filters.reshape(L * 4, P1 // 2, P2, C))
    outs = []
    for order in range(HY_ORDER):
        outs.append(pl.pallas_call(
            functools.partial(_filter_spectrum_kernel, order=order, n_total=2 * S),
            name=f"hyena_filter_spectrum_{order}", grid=(L, P1 // kb),
            in_specs=[pl.BlockSpec((2 * P2, 2 * P2), lambda l, i: (0, 0)),
                      pl.BlockSpec((kb, P2, 1), lambda l, i: (i, 0, 0)),
                      pl.BlockSpec((kb, P2, 1), lambda l, i: (i, 0, 0)),
                      pl.BlockSpec((1, 8, 1024), lambda l, i: (l, 0, 0)),
                      pl.BlockSpec((1, 2, kb, P2, C), lambda l, i: (l * 4 + order, 0, i, 0, 0)),
                      pl.BlockSpec((1, 2, kb, P2, C), lambda l, i: (l * 4 + 2 + order, 0, i, 0, 0))],
            out_specs=pl.BlockSpec((1, 2, kb, P2, C), lambda l, i: (l, 0, i, 0, 0)),
            out_shape=jax.ShapeDtypeStruct((L, 2, P1, P2, C), BF16),
            compiler_params=_params("parallel", "parallel"),
        )(consts["g"], consts["tw_r"], consts["tw_i"], stats, a, a))
    return outs


def _spectral_filter_kernel(g_ref, gt_ref, twr_ref, twi_ref, h_ref, a_ref, o_ref):
    P2 = DFT_P2
    g, gt = g_ref[...], gt_ref[...]
    for kk in range(a_ref.shape[2]):
        tr, ti = twr_ref[kk], twi_ref[kk]
        x = _twiddled_spectrum(a_ref, kk, tr, ti, g)
        xr, xi = x[:P2], x[P2:]
        hr, hi = h_ref[0, 0, kk].astype(F32), h_ref[0, 1, kk].astype(F32)
        y = jnp.concatenate([xr * hr - xi * hi, xr * hi + xi * hr], axis=0).astype(BF16)
        bm = jnp.dot(gt, y, preferred_element_type=F32)
        br, bi = bm[:P2], bm[P2:]
        o_ref[0, 0, kk] = (br * tr + bi * ti).astype(BF16)
        o_ref[0, 1, kk] = (bi * tr - br * ti).astype(BF16)


def _spectral_filter(consts, a, h, layer, kb=32):
    B, _, P1, P2, C = a.shape
    kb = min(kb, P1)
    blk = lambda f: pl.BlockSpec((1, 2, kb, P2, C), f)
    return pl.pallas_call(
        _spectral_filter_kernel, name="hyena_spectral_filter", grid=(B, P1 // kb),
        in_specs=[pl.BlockSpec((2 * P2, 2 * P2), lambda b, i: (0, 0)),
                  pl.BlockSpec((2 * P2, 2 * P2), lambda b, i: (0, 0)),
                  pl.BlockSpec((kb, P2, 1), lambda b, i: (i, 0, 0)),
                  pl.BlockSpec((kb, P2, 1), lambda b, i: (i, 0, 0)),
                  blk(lambda b, i: (layer, 0, i, 0, 0)), blk(lambda b, i: (b, 0, i, 0, 0))],
        out_specs=blk(lambda b, i: (b, 0, i, 0, 0)),
        out_shape=jax.ShapeDtypeStruct(a.shape, BF16),
        compiler_params=_params("parallel", "parallel"),
    )(consts["g"], consts["gt"], consts["tw_r"], consts["tw_i"], h, a)


def _idft_rows_kernel(m_ref, b_ref, u_ref, gate_ref, bias_ref, o_ref):
    bm = b_ref[0]
    bm = bm.reshape(bm.shape[0] * bm.shape[1] * bm.shape[2], bm.shape[3]).astype(BF16)
    y = jnp.dot(m_ref[...], bm, preferred_element_type=F32)
    o_ref[0] = gate_ref[0] * (y.reshape(o_ref.shape[1:]) + bias_ref[...] * u_ref[0])


def _idft_rows_gated(consts, bm, u, gate, bias):
    B, p1h, p2, c = u.shape
    tj = DFT_COLS_PER_STEP
    row = lambda: pl.BlockSpec((1, p1h, tj, c), lambda b, j: (b, 0, j, 0))
    return pl.pallas_call(
        _idft_rows_kernel, name="hyena_idft_rows", grid=(B, p2 // tj),
        in_specs=[pl.BlockSpec((p1h * tj, 4 * p1h * tj), lambda b, j: (0, 0)),
                  pl.BlockSpec((1, 2, 2 * p1h, tj, c), lambda b, j: (b, 0, 0, j, 0)),
                  row(), row(), pl.BlockSpec((1, c), lambda b, j: (0, 0))],
        out_specs=row(), out_shape=jax.ShapeDtypeStruct(u.shape, F32),
        compiler_params=_params("parallel", "parallel"),
    )(consts["m_inv"], bm, u, gate, bias.reshape(1, c))


def _short_conv_kernel(x_ref, p_ref, n_ref, w_ref, b_ref, v_ref, x1_ref, x2_ref):
    i = pl.program_id(1)
    x = x_ref[0]
    t = x.shape[0]
    row = lax.broadcasted_iota(jnp.int32, x.shape, 0)
    prev_row = jnp.where(i == 0, 0.0, p_ref[0, 7:8, :])
    next_row = jnp.where(i == pl.num_programs(1) - 1, 0.0, n_ref[0, 0:1, :])
    before = jnp.where(row == 0, prev_row, pltpu.roll(x, 1, axis=0))
    after = jnp.where(row == t - 1, next_row, pltpu.roll(x, t - 1, axis=0))
    w = w_ref[...]
    u = before * w[0:1] + x * w[1:2] + after * w[2:3] + b_ref[...]
    v_ref[0] = u[:, 0:HY_CH]
    x1_ref[0] = u[:, HY_CH:2 * HY_CH]
    x2_ref[0] = u[:, 2 * HY_CH:3 * HY_CH]


def _short_conv(cols, conv_w, conv_b, t=512):
    B, S, W = cols.shape
    t = min(t, S)
    r = t // 8
    out = pl.BlockSpec((1, t, HY_CH), lambda b, i: (b, i, 0))
    return pl.pallas_call(
        _short_conv_kernel, name="hyena_short_conv", grid=(B, S // t),
        in_specs=[pl.BlockSpec((1, t, W), lambda b, i: (b, i, 0)),
                  pl.BlockSpec((1, 8, W), lambda b, i: (b, jnp.maximum(i * r - 1, 0), 0)),
                  pl.BlockSpec((1, 8, W), lambda b, i: (b, jnp.minimum((i + 1) * r, S // 8 - 1), 0)),
                  pl.BlockSpec((HY_SHORT, W), lambda b, i: (0, 0)), pl.BlockSpec((1, W), lambda b, i: (0, 0))],
        out_specs=[out, out, out], out_shape=[jax.ShapeDtypeStruct((B, S, HY_CH), F32)] * 3,
        compiler_params=_params("parallel", "parallel"))(cols, cols, cols, conv_w, conv_b.reshape(1, W))


def _hyena_mixer(consts, cols, conv_w, conv_b, spectra, bias, layer):
    B, S, _ = cols.shape
    P2, C = DFT_P2, HY_CH
    P1 = 2 * S // P2
    v, x1, x2 = _short_conv(cols, conv_w, conv_b)
    rows = lambda a: a.reshape(B, P1 // 2, P2, C)
    z = rows(v)
    for order, gate in enumerate((x1, x2)):
        bm = _spectral_filter(consts, _dft_rows(consts, z), spectra[order], layer)
        z = _idft_rows_gated(consts, bm, z, rows(gate), bias[order])
    return z.reshape(B, S, C)


def _route(lt):
    t = lt.shape[1]
    row = lax.broadcasted_iota(jnp.int32, (8, t), 0)
    lg = jnp.where(row < N_GROUPS, lt[0:8], NEG_INF)
    mg = jnp.max(lg, axis=0, keepdims=True)
    p_top = 1.0 / jnp.sum(jnp.exp(lg - mg), axis=0, keepdims=True)
    g_sel = jnp.min(jnp.where(lg == mg, row, 8), axis=0, keepdims=True)
    le = jnp.zeros((8, t), F32)
    for g in range(N_GROUPS):
        le = jnp.where(g_sel == g, lt[8 + 8 * g:16 + 8 * g], le)
    v1 = jnp.max(le, axis=0, keepdims=True)
    i1 = jnp.min(jnp.where(le == v1, row, 8), axis=0, keepdims=True)
    rest = jnp.where(row == i1, NEG_INF, le)
    v2 = jnp.max(rest, axis=0, keepdims=True)
    i2 = jnp.min(jnp.where(rest == v2, row, 8), axis=0, keepdims=True)
    e = jnp.exp(v2 - v1)
    w1 = 1.0 / (1.0 + e)
    base = g_sel * EXPERTS_PER_GROUP
    idx = jnp.where(row == 0, base + i1, jnp.where(row == 1, base + i2, 0))
    gate = jnp.where(row == 0, p_top * w1, jnp.where(row == 1, p_top * (e * w1), 0.0))
    return idx, gate


def _post_kernel(x_ref, g1_ref, sc_ref, sh_ref, ln_ref, a_ref, b_ref, c_ref, d_ref, wo_ref, wr_ref, br_ref,
                 xo_ref, h_ref, idx_ref, gate_ref):
    wo = wo_ref[0]
    y = (_tn(a_ref[0].astype(BF16), wo[0:256]) + _tn(b_ref[0].astype(BF16), wo[256:512])
         + _tn(c_ref[0].astype(BF16), wo[512:768])
         + jnp.dot(d_ref[0].astype(BF16), wo[768:1024], preferred_element_type=F32))
    x = x_ref[0] + g1_ref[0] * y
    xo_ref[0] = x
    h = _rms(x) * ln_ref[...] * (1.0 + sc_ref[0]) + sh_ref[0]
    h_ref[0] = h
    lt = lax.dot_general(wr_ref[0], h, (((1,), (1,)), ((), ())), preferred_element_type=F32, precision=HIGHEST) + br_ref[0]
    idx, gate = _route(lt)
    idx_ref[...] = idx
    gate_ref[...] = gate


def _post_mixer(x, g1, sc2, sh2, ln_g, layer, mix_a, mix_b, mix_c, mix_d, w_out, w_router, b_router, t=512):
    B, S, D = x.shape
    t = min(t, S)
    nt = S // t
    nat = lambda n: pl.BlockSpec((1, t, n), lambda b, i: (b, i, 0))
    fm = pl.BlockSpec((1, 256, t), lambda b, i: (b, 0, i))
    mod = pl.BlockSpec((1, 1, D), lambda b, i: (b, 0, 0))
    lw = lambda a: pl.BlockSpec((1,) + a.shape[1:], lambda b, i: (layer,) + (0,) * (a.ndim - 1))
    tok = pl.BlockSpec((8, t), lambda b, i: (0, b * nt + i))
    return pl.pallas_call(
        _post_kernel, name="mixer_output_router", grid=(B, nt),
        in_specs=[nat(D), mod, mod, mod, pl.BlockSpec((1, D), lambda b, i: (0, 0)), fm, fm, fm, nat(256),
                  lw(w_out), lw(w_router), lw(b_router)],
        out_specs=[nat(D), nat(D), tok, tok],
        out_shape=[jax.ShapeDtypeStruct((B, S, D), F32), jax.ShapeDtypeStruct((B, S, D), F32),
                   jax.ShapeDtypeStruct((8, B * S), jnp.int32), jax.ShapeDtypeStruct((8, B * S), F32)],
        compiler_params=_params("parallel", "parallel"),
    )(x, g1, sc2, sh2, ln_g[layer].reshape(1, D), mix_a, mix_b, mix_c, mix_d, w_out, w_router, b_router)


def _start_row_gather(src_hbm, dst_ref, sem, row_of, priorities=(0, 1)):
    n = dst_ref.shape[0]
    for i in range(n):
        r = (i % 8) * (n // 8) + i // 8
        pltpu.make_async_copy(src_hbm.at[pl.ds(row_of(r), 1)], dst_ref.at[pl.ds(r, 1)], sem).start(
            priority=priorities[i % len(priorities)])


def _wait_row_gather(src_hbm, dst_ref, sem):
    pltpu.make_async_copy(src_hbm.at[pl.ds(0, dst_ref.shape[0])], dst_ref, sem).wait()


def _expert_kernel(be_ref, tok_ref, h_hbm, wg_ref, wu_ref, wd_ref, o_ref, x_buf, sem):
    i = pl.program_id(0)
    last = pl.num_programs(0) - 1
    slot = i % 2

    @pl.when(i == 0)
    def _():
        _start_row_gather(h_hbm, x_buf.at[0], sem.at[0], lambda r: tok_ref[r])

    nxt = jnp.minimum(i + 1, last) * MOE_ROWS
    _start_row_gather(h_hbm, x_buf.at[1 - slot], sem.at[1 - slot], lambda r: tok_ref[nxt + r])
    _wait_row_gather(h_hbm, x_buf.at[slot], sem.at[slot])

    @pl.when(i == last)
    def _():
        _wait_row_gather(h_hbm, x_buf.at[1 - slot], sem.at[1 - slot])

    x = x_buf[slot].astype(BF16)
    g = jnp.dot(x, wg_ref[0, 0].astype(BF16), preferred_element_type=F32)
    u = jnp.dot(x, wu_ref[0, 0].astype(BF16), preferred_element_type=F32)
    a = (g * jax.nn.sigmoid(g) * u).astype(BF16)
    o_ref[...] = jnp.dot(a, wd_ref[0, 0].astype(BF16), preferred_element_type=F32)


def _expert_ffn(h, slot_tok, block_expert, w_gate, w_up, w_down, layer):
    D = h.shape[1]
    P = slot_tok.shape[0]
    wmap = lambda i, be, tok: (layer, be[i], 0, 0)
    return pl.pallas_call(
        _expert_kernel, name="expert_ffn",
        grid_spec=pltpu.PrefetchScalarGridSpec(
            num_scalar_prefetch=2, grid=(P // MOE_ROWS,),
            in_specs=[pl.BlockSpec(memory_space=pl.ANY),
                      pl.BlockSpec((1, 1, D, D_EXPERT), wmap), pl.BlockSpec((1, 1, D, D_EXPERT), wmap),
                      pl.BlockSpec((1, 1, D_EXPERT, D), wmap)],
            out_specs=pl.BlockSpec((MOE_ROWS, D), lambda i, be, tok: (i, 0)),
            scratch_shapes=[pltpu.VMEM((2, MOE_ROWS, D), F32), pltpu.SemaphoreType.DMA((2,))]),
        out_shape=jax.ShapeDtypeStruct((P, D), F32),
        compiler_params=_params("arbitrary"))(block_expert, slot_tok, h, w_gate, w_up, w_down)


def _dispatch_plan(idx, T):
    e = idx[0:TOP_K]
    onehot = (e.reshape(-1)[:, None] == jnp.arange(N_EXPERTS)[None]).astype(jnp.int32)
    before = jnp.cumsum(onehot, axis=0) - onehot
    rank = jnp.sum(before * onehot, axis=1)
    counts = jnp.sum(onehot, axis=0)
    padded = (counts + MOE_ROWS - 1) // MOE_ROWS * MOE_ROWS
    pad_end = jnp.cumsum(padded)
    dest = ((pad_end - padded)[e.reshape(-1)] + rank).reshape(TOP_K, T)
    n_blocks = (TOP_K * T + N_EXPERTS * (MOE_ROWS - 1) + MOE_ROWS - 1) // MOE_ROWS
    starts = jnp.arange(n_blocks, dtype=jnp.int32) * MOE_ROWS
    block_expert = jnp.minimum(jnp.sum((pad_end[None, :] <= starts[:, None]).astype(jnp.int32), axis=1), N_EXPERTS - 1)
    order = jnp.argsort(e.reshape(-1), stable=True).astype(jnp.int32)
    slots = jnp.arange(n_blocks * MOE_ROWS, dtype=jnp.int32)
    slot_expert = jnp.repeat(block_expert, MOE_ROWS)
    rank_in_expert = slots - (pad_end - padded)[slot_expert]
    first = (jnp.cumsum(counts) - counts)[slot_expert]
    valid = rank_in_expert < counts[slot_expert]
    assignment = order[jnp.clip(first + rank_in_expert, 0, TOP_K * T - 1)]
    slot_tok = jnp.where(valid, assignment % T, slots % T).astype(jnp.int32)
    return dest, slot_tok, block_expert.astype(jnp.int32)


def _combine_kernel(dest_ref, x_ref, g2_ref, gate_ref, fg_ref, y_hbm, o_ref, y_buf, sem, *, final, n_tokens):
    t = x_ref.shape[1]
    step = pl.program_id(0) * pl.num_programs(1) + pl.program_id(1)
    last = pl.num_programs(0) * pl.num_programs(1) - 1
    slot = step % 2

    def start(tile, s):
        for k in range(TOP_K):
            _start_row_gather(y_hbm, y_buf.at[s, k], sem.at[s, k], lambda r, k=k: dest_ref[k * n_tokens + tile * t + r])

    def wait(s):
        for k in range(TOP_K):
            _wait_row_gather(y_hbm, y_buf.at[s, k], sem.at[s, k])

    @pl.when(step == 0)
    def _():
        start(0, 0)

    start(jnp.minimum(step + 1, last), 1 - slot)
    wait(slot)

    @pl.when(step == last)
    def _():
        wait(1 - slot)

    gt = gate_ref[...]
    x = x_ref[0] + g2_ref[0] * (gt[:, 0:1] * y_buf[slot, 0] + gt[:, 1:2] * y_buf[slot, 1])
    if final:
        x = _rms(x) * fg_ref[...]
    o_ref[0] = x


def _combine(x, g2, yb, dest, gate_t, final_g, final, t=256):
    B, S, D = x.shape
    t = min(t, S)
    nt = S // t
    nat = pl.BlockSpec((1, t, D), lambda b, i, d: (b, i, 0))
    return pl.pallas_call(
        functools.partial(_combine_kernel, final=final, n_tokens=B * S), name="moe_combine",
        grid_spec=pltpu.PrefetchScalarGridSpec(
            num_scalar_prefetch=1, grid=(B, nt),
            in_specs=[nat, pl.BlockSpec((1, 1, D), lambda b, i, d: (b, 0, 0)),
                      pl.BlockSpec((t, 8), lambda b, i, d: (b * nt + i, 0)),
                      pl.BlockSpec((1, D), lambda b, i, d: (0, 0)),
                      pl.BlockSpec(memory_space=pl.ANY)],
            out_specs=nat,
            scratch_shapes=[pltpu.VMEM((2, TOP_K, t, D), F32), pltpu.SemaphoreType.DMA((2, TOP_K))]),
        out_shape=jax.ShapeDtypeStruct((B, S, D), F32),
        compiler_params=_params("arbitrary", "arbitrary"))(dest.reshape(-1), x, g2, gate_t, final_g.reshape(1, D), yb)


def _moe(x, h, idx, gate, g2, w_gate, w_up, w_down, layer, final_g, final):
    B, S, D = x.shape
    T = B * S
    dest, slot_tok, block_expert = _dispatch_plan(idx, T)
    yb = _expert_ffn(h.reshape(T, D), slot_tok, block_expert, w_gate, w_up, w_down, layer)
    return _combine(x, g2, yb, dest, gate.T, final_g, final)


def kernel(x, c, positions, ln1_g, ln2_g, ada_w, ada_b, w_in, mla_q_norm, mla_kv_norm, mla_w_uq, mla_w_ukv, swa_sink, diff_lambda, diff_norm_g, hy_conv_w, hy_conv_b, hy_w1, hy_b1, hy_w2, hy_b2, hy_w3, hy_b3, hy_w4, hy_sin_freq, hy_bias, w_out, router_g_w, router_g_b, router_e_w, router_e_b, moe_w_gate, moe_w_up, moe_w_down, final_g):
    B, S, D = x.shape
    L = w_in.shape[0]
    mod = _ada_modulation(c, ada_w, ada_b)[:, :B].reshape(L, B, 6, 1, D)
    tab_fm = _rope_tables(positions)
    tab_nat = _natural_rope_table(tab_fm)
    weights = _layer_weight_layouts(w_in, mla_q_norm, mla_kv_norm, mla_w_uq, mla_w_ukv)
    w_out_b = w_out.astype(BF16)
    w_router = jnp.concatenate([router_g_w.transpose(0, 2, 1), jnp.zeros((L, 8 - N_GROUPS, D), F32),
                                router_e_w.transpose(0, 2, 1)], axis=1)
    b_router = jnp.concatenate([router_g_b, jnp.zeros((L, 8 - N_GROUPS), F32), router_e_b], axis=1)[:, :, None]
    consts = _hyena_constants(S)
    filters, stats = _hyena_filters(consts, hy_w1, hy_b1, hy_w2, hy_b2, hy_w3, hy_b3, hy_w4, hy_sin_freq, S)
    spectra = _filter_spectra(consts, filters, stats, S)
    for l in range(L):
        sh1, sc1, g1, sh2, sc2, g2 = (mod[l, :, i] for i in range(6))
        kmla, qn, qp, vmla, kswa, qswa, vswa, kdiff, qdiff, vdiff, hy = _project(
            x, sc1, sh1, ln1_g, l, weights, tab_nat, tab_fm)
        mix_a = _mla_attention(kmla, qn, qp, vmla)
        mix_b = _swa_attention(kswa, qswa, vswa, swa_sink[l])
        lambda_init = 0.8 - 0.6 * math.exp(-0.3 * l)
        mix_c = _diff_attention(kdiff, qdiff, vdiff, diff_lambda[l], diff_norm_g[l], lambda_init)
        mix_d = _hyena_mixer(consts, hy, hy_conv_w[l], hy_conv_b[l], spectra, hy_bias[l], l)
        x, h, idx, gate = _post_mixer(x, g1, sc2, sh2, ln2_g, l, mix_a, mix_b, mix_c, mix_d, w_out_b, w_router, b_router)
        x = _moe(x, h, idx, gate, g2, moe_w_gate, moe_w_up, moe_w_down, l, final_g, final=(l == L - 1))
    return x
```
